```python
import math
import jax
import jax.numpy as jnp
from jax import lax
import numpy as np

D_MODEL = 1024
BATCH = 16
SEQ = 2048
DEPTH = 1

GRID_W = 64
CTX_LEN = 256
NORM_EPS = 1e-6

S5_WIDTH = 512
S5_GROUP = 16
S5_GROUPS = S5_WIDTH // S5_GROUP
S5_STATE = 64
DN_HEADS = 8
DN_HEAD_DIM = 128
DN_WIDTH = DN_HEADS * DN_HEAD_DIM
DN_CONV = 5
DN_CHUNK = 64
PEER_HEADS = 8
PEER_N_KEYS = 128
PEER_N_EXPERTS = PEER_N_KEYS ** 2
PEER_KEY_DIM = 128
PEER_TOPK = 16
PEER_CHUNK = 128

COL_U = 0
COL_K = COL_U + S5_WIDTH
COL_V = COL_K + DN_WIDTH
COL_AB = COL_V + DN_WIDTH
STATE_COLS = COL_AB + 4 * DN_HEADS
COL_Q = STATE_COLS
COL_Z = COL_Q + DN_WIDTH
COL_G5 = COL_Z + DN_WIDTH
COL_GD = COL_G5 + D_MODEL
IN_COLS = COL_GD + D_MODEL

kernel_name = "hybrid_s5_gdn_peer_flow_block"


def rmsnorm(x, w):
    xf = x.astype(jnp.float32)
    y = xf * lax.rsqrt(jnp.mean(xf * xf, axis=-1, keepdims=True) + NORM_EPS)
    return (y * w.astype(jnp.float32)).astype(x.dtype)


def modulate(x, gain, shift, scale):
    return rmsnorm(x, gain) * (1.0 + scale) + shift


def adaln(cvec, w_mod, b_mod, n):
    m = jax.nn.silu(cvec) @ w_mod[:, : n * D_MODEL] + b_mod[: n * D_MODEL]
    return [t[..., None, :] for t in jnp.split(m, n, axis=-1)]


def l2norm(t):
    return t * lax.rsqrt(jnp.sum(t * t, axis=-1, keepdims=True) + 1e-6)


def centred_dwconv(x, w):
    half = w.shape[0] // 2
    length = x.shape[-2]
    pad = [(0, 0)] * (x.ndim - 2) + [(half, half), (0, 0)]
    xp = jnp.pad(x, pad)
    return sum(w[j] * xp[..., j:j + length, :] for j in range(w.shape[0]))


def s5_scan(u, p, d, h0_re, h0_im):
    f32 = jnp.float32
    lam_re = p['s5_lam_re'][d].astype(f32)
    lam_im = p['s5_lam_im'][d].astype(f32)
    step = jnp.exp(p['s5_log_step'][d].astype(f32))[:, None]
    mag = jnp.exp(lam_re * step)
    ab_re, ab_im = mag * jnp.cos(lam_im * step), mag * jnp.sin(lam_im * step)
    den = lam_re * lam_re + lam_im * lam_im
    f_re = ((ab_re - 1.0) * lam_re + ab_im * lam_im) / den
    f_im = (ab_im * lam_re - (ab_re - 1.0) * lam_im) / den
    b_re = p['s5_b_re'][d].astype(f32)
    b_im = p['s5_b_im'][d].astype(f32)
    bb_re = f_re[..., None] * b_re - f_im[..., None] * b_im
    bb_im = f_re[..., None] * b_im + f_im[..., None] * b_re
    x_re = jnp.einsum('btgh,gph->btgp', u, bb_re)
    x_im = jnp.einsum('btgh,gph->btgp', u, bb_im)
    if h0_re is not None:
        x_re = x_re.at[:, 0].add(ab_re * h0_re - ab_im * h0_im)
        x_im = x_im.at[:, 0].add(ab_re * h0_im + ab_im * h0_re)
    t_len = u.shape[1]
    a_re = jnp.broadcast_to(ab_re, (1, t_len) + ab_re.shape)
    a_im = jnp.broadcast_to(ab_im, (1, t_len) + ab_im.shape)

    def combine(e1, e2):
        a1r, a1i, b1r, b1i = e1
        a2r, a2i, b2r, b2i = e2
        return (a2r * a1r - a2i * a1i, a2r * a1i + a2i * a1r,
                a2r * b1r - a2i * b1i + b2r, a2r * b1i + a2i * b1r + b2i)

    _, _, h_re, h_im = lax.associative_scan(combine, (a_re, a_im, x_re, x_im), axis=1)
    return h_re, h_im


def delta_rule_chunked(k, v, g, beta, s0, q):
    bsz, nh, t_len, dk = k.shape
    dv = v.shape[-1]
    n = t_len // DN_CHUNK

    def chunks(t):
        return t.reshape((bsz, nh, n, DN_CHUNK) + t.shape[3:])

    k, v, g, beta = chunks(k), chunks(v), chunks(g), chunks(beta)
    gc = jnp.cumsum(g, axis=-1)
    lower = np.tril(np.ones((DN_CHUNK, DN_CHUNK), dtype=bool))
    strict = np.tril(np.ones((DN_CHUNK, DN_CHUNK), dtype=bool), -1)
    diff = gc[..., :, None] - gc[..., None, :]
    decay = jnp.where(lower, jnp.exp(jnp.minimum(diff, 0.0)), 0.0)
    kb = k * beta[..., None]
    a = jnp.where(strict, jnp.einsum('bhncd,bhnsd->bhncs', kb, k) * decay, 0.0)
    eye = jnp.eye(DN_CHUNK, dtype=a.dtype)
    t_inv = lax.linalg.triangular_solve(a, jnp.broadcast_to(eye, a.shape), left_side=True,
                                        lower=True, unit_diagonal=True)
    u = jnp.einsum('bhncs,bhnsd->bhncd', t_inv, v * beta[..., None])
    w = jnp.einsum('bhncs,bhnsd->bhncd', t_inv, kb * jnp.exp(gc)[..., None])
    g_last = gc[..., -1]
    k_dec = k * jnp.exp(g_last[..., None] - gc)[..., None]
    if s0 is None:
        s0 = jnp.zeros((bsz, nh, dk, dv), jnp.float32)

    def front(t):
        return jnp.moveaxis(t, 2, 0)

    xs = [front(u), front(w), front(k_dec), front(g_last)]
    if q is not None:
        q = chunks(q)
        xs += [front(q * jnp.exp(gc)[..., None]),
               front(jnp.einsum('bhncd,bhnsd->bhncs', q, k) * decay)]

    def step(s, inp):
        u_i, w_i, kd_i, gl_i = inp[:4]
        v_new = u_i - jnp.einsum('bhcd,bhde->bhce', w_i, s)
        s_next = s * jnp.exp(gl_i)[..., None, None] + jnp.einsum('bhcd,bhce->bhde', kd_i, v_new)
        if len(inp) == 4:
            return s_next, None
        qd_i, at_i = inp[4:]
        o_i = jnp.einsum('bhcd,bhde->bhce', qd_i, s) + jnp.einsum('bhcs,bhse->bhce', at_i, v_new)
        return s_next, o_i

    s_fin, o = lax.scan(step, s0, tuple(xs))
    if q is None:
        return None, s_fin
    return jnp.moveaxis(o, 0, 2).reshape(bsz, nh, t_len, dv), s_fin


def token_mixers(h, p, init, rows, need_out):
    bsz, t_len, _ = h.shape
    f32 = jnp.float32
    w_in = p['w_in'] if need_out else p['w_in'][:, :STATE_COLS]
    proj = h @ w_in

    u = proj[..., COL_U:COL_K].astype(f32).reshape(bsz, t_len, S5_GROUPS, S5_GROUP)
    h0 = (None, None, None, None) if init is None else init['s5']
    sf_re, sf_im = s5_scan(u, p, 0, h0[0], h0[1])
    sb_re, sb_im = s5_scan(jnp.flip(u, 1), p, 1, h0[2], h0[3])
    s5_final = (sf_re[:, -1], sf_im[:, -1], sb_re[:, -1], sb_im[:, -1])

    if need_out:
        conv_in = jnp.concatenate([proj[..., COL_K:COL_AB], proj[..., COL_Q:COL_Z]], axis=-1)
        conv_w = p['dn_conv_w']
    else:
        conv_in = proj[..., COL_K:COL_AB]
        conv_w = p['dn_conv_w'][:, :2 * DN_WIDTH]
    if rows is None:
        conv_out = centred_dwconv(conv_in, conv_w)
    else:
        conv_out = centred_dwconv(conv_in.reshape(bsz, rows, GRID_W, -1), conv_w).reshape(bsz, t_len, -1)
    kvq = jax.nn.silu(conv_out).astype(f32)

    def heads(t):
        return t.reshape(bsz, t_len, DN_HEADS, DN_HEAD_DIM).transpose(0, 2, 1, 3)

    k = l2norm(heads(kvq[..., :DN_WIDTH]))
    v = heads(kvq[..., DN_WIDTH:2 * DN_WIDTH])
    q = l2norm(heads(kvq[..., 2 * DN_WIDTH:])) * (DN_HEAD_DIM ** -0.5) if need_out else None
    ab = proj[..., COL_AB:STATE_COLS].astype(f32).reshape(bsz, t_len, 4, DN_HEADS).transpose(2, 0, 3, 1)
    a_log = p['dn_a_log'].astype(f32)
    dt_bias = p['dn_dt_bias'].astype(f32)
    g_f = -jnp.exp(a_log[0])[:, None] * jax.nn.softplus(ab[0] + dt_bias[0][:, None])
    beta_f = jax.nn.sigmoid(ab[1])
    g_b = -jnp.exp(a_log[1])[:, None] * jax.nn.softplus(ab[2] + dt_bias[1][:, None])
    beta_b = jax.nn.sigmoid(ab[3])
    d0 = (None, None) if init is None else init['dn']

    def flipt(t):
        return None if t is None else jnp.flip(t, axis=2)

    o_f, dn_sf = delta_rule_chunked(k, v, g_f, beta_f, d0[0], q)
    o_b, dn_sb = delta_rule_chunked(flipt(k), flipt(v), flipt(g_b), flipt(beta_b), d0[1], flipt(q))
    states = {'s5': s5_final, 'dn': (dn_sf, dn_sb)}
    if not need_out:
        return None, states

    c_re = p['s5_c_re'].astype(f32)
    c_im = p['s5_c_im'].astype(f32)
    y_f = jnp.einsum('btgp,ghp->btgh', sf_re, c_re[0]) - jnp.einsum('btgp,ghp->btgh', sf_im, c_im[0])
    y_b = jnp.einsum('btgp,ghp->btgh', sb_re, c_re[1]) - jnp.einsum('btgp,ghp->btgh', sb_im, c_im[1])
    y5 = y_f + jnp.flip(y_b, 1) + p['s5_d'].astype(f32).reshape(S5_GROUPS, S5_GROUP) * u
    y5 = jax.nn.gelu(y5.reshape(bsz, t_len, S5_WIDTH)).astype(h.dtype)
    y5 = y5 * jax.nn.sigmoid(y5 @ p['s5_glu_w'] + p['s5_glu_b'])

    o = (o_f + jnp.flip(o_b, 2)).transpose(0, 2, 1, 3)
    z = proj[..., COL_Z:COL_G5].reshape(bsz, t_len, DN_HEADS, DN_HEAD_DIM)
    y_dn = (rmsnorm(o, p['dn_norm_w']).astype(h.dtype) * jax.nn.silu(z)).reshape(bsz, t_len, DN_WIDTH)

    gate5 = jax.nn.sigmoid(proj[..., COL_G5:COL_GD])
    gated = jax.nn.sigmoid(proj[..., COL_GD:IN_COLS])
    merged = gate5 * (y5 @ p['w_branch_s5']) + gated * (y_dn @ p['w_branch_dn'])
    return merged @ p['w_out'], states


def peer_ffn(h, p):
    bsz, t_len, d = h.shape
    n_tok = bsz * t_len
    hf = h.reshape(n_tok, d)
    q = (hf @ p['peer_w_q']).reshape(n_tok, PEER_HEADS, 2, PEER_KEY_DIM)
    s = jnp.einsum('nhjd,hjkd->nhjk', q, p['peer_sub_keys']).astype(jnp.float32)
    sv, si = lax.top_k(s, PEER_TOPK)
    cand = sv[..., 0, :, None] + sv[..., 1, None, :]
    cand_idx = si[..., 0, :, None] * PEER_N_KEYS + si[..., 1, None, :]
    top_v, top_pos = lax.top_k(cand.reshape(n_tok, PEER_HEADS, PEER_TOPK * PEER_TOPK), PEER_TOPK)
    idx = jnp.take_along_axis(cand_idx.reshape(n_tok, PEER_HEADS, PEER_TOPK * PEER_TOPK), top_pos, axis=-1)
    gate = jax.nn.softmax(top_v, axis=-1).astype(h.dtype)
    n_chunks = n_tok // PEER_CHUNK
    u_tab, v_tab = p['peer_u'], p['peer_v']

    def expert_block(args):
        hc, ic, gc = args
        act = jax.nn.gelu(jnp.einsum('cd,chkd->chk', hc, jnp.take(u_tab, ic, axis=0)))
        return jnp.einsum('chk,chkd->cd', act * gc, jnp.take(v_tab, ic, axis=0))

    out = lax.map(expert_block, (hf.reshape(n_chunks, PEER_CHUNK, d),
                                 idx.reshape(n_chunks, PEER_CHUNK, PEER_HEADS, PEER_TOPK),
                                 gate.reshape(n_chunks, PEER_CHUNK, PEER_HEADS, PEER_TOPK)))
    return out.reshape(bsz, t_len, d)


def hybrid_layer(x, ctx, c, c_ctx, p, rows, last):
    m_ctx = adaln(c_ctx, p['w_mod'], p['b_mod'], 2 if last else 6)
    hc = modulate(ctx, p['norm_pre_mix'], m_ctx[0], m_ctx[1])
    ctx_mix, ctx_states = token_mixers(hc, p, None, None, not last)
    if not last:
        ctx = ctx + m_ctx[2] * rmsnorm(ctx_mix, p['norm_post_mix'])
        hc2 = modulate(ctx, p['norm_pre_ffn'], m_ctx[3], m_ctx[4])
        ctx = ctx + m_ctx[5] * rmsnorm(peer_ffn(hc2, p), p['norm_post_ffn'])
    m = adaln(c, p['w_mod'], p['b_mod'], 6)
    hx = modulate(x, p['norm_pre_mix'], m[0], m[1])
    x_mix, _ = token_mixers(hx, p, ctx_states, rows, True)
    x = x + m[2] * rmsnorm(x_mix, p['norm_post_mix'])
    hx2 = modulate(x, p['norm_pre_ffn'], m[3], m[4])
    x = x + m[5] * rmsnorm(peer_ffn(hx2, p), p['norm_post_ffn'])
    return x, ctx


def setup_inputs(seed: int = 0) -> dict:
    key = jax.random.key(seed)
    ks = jax.random.split(key, 32)
    f32 = jnp.float32

    def nrm(k, shape, s):
        return s * jax.random.normal(k, shape, f32)

    L, D = DEPTH, D_MODEL
    G, P, H = S5_GROUPS, S5_STATE, S5_GROUP
    lam_im0 = jnp.pi * jnp.arange(P, dtype=f32)
    log_step = jax.random.uniform(ks[13], (L, 2, G), f32, math.log(1e-3), math.log(1e-1))
    a_log = jnp.log(jax.random.uniform(ks[22], (L, 2, DN_HEADS), f32, 1.0, 16.0))
    dt = jnp.exp(jax.random.uniform(ks[23], (L, 2, DN_HEADS), f32, math.log(1e-3), math.log(1e-1)))
    dt_bias = dt + jnp.log(-jnp.expm1(-dt))
    return {
        'x': nrm(ks[0], (BATCH, SEQ, D), 1.0),
        'c': nrm(ks[1], (BATCH, D), 1.0),
        'ctx': nrm(ks[2], (BATCH, CTX_LEN, D), 1.0),
        'c_ctx': nrm(ks[3], (D,), 1.0),
        'w_mod': nrm(ks[4], (L, D, 6 * D), 0.5 * D ** -0.5),
        'b_mod': nrm(ks[5], (L, 6 * D), 0.01),
        'norm_pre_mix': 1.0 + nrm(ks[6], (L, D), 0.02),
        'norm_post_mix': 1.0 + nrm(ks[7], (L, D), 0.02),
        'norm_pre_ffn': 1.0 + nrm(ks[8], (L, D), 0.02),
        'norm_post_ffn': 1.0 + nrm(ks[9], (L, D), 0.02),
        'w_in': nrm(ks[10], (L, D, IN_COLS), D ** -0.5),
        's5_lam_re': -0.5 + nrm(ks[11], (L, 2, G, P), 0.01),
        's5_lam_im': lam_im0 + nrm(ks[12], (L, 2, G, P), 0.01),
        's5_log_step': log_step,
        's5_b_re': nrm(ks[14], (L, 2, G, P, H), (2 * H) ** -0.5),
        's5_b_im': nrm(ks[15], (L, 2, G, P, H), (2 * H) ** -0.5),
        's5_c_re': nrm(ks[16], (L, 2, G, H, P), (2 * P) ** -0.5),
        's5_c_im': nrm(ks[17], (L, 2, G, H, P), (2 * P) ** -0.5),
        's5_d': nrm(ks[18], (L, S5_WIDTH), 0.5),
        's5_glu_w': nrm(ks[19], (L, S5_WIDTH, S5_WIDTH), S5_WIDTH ** -0.5),
        's5_glu_b': nrm(ks[20], (L, S5_WIDTH), 0.01),
        'dn_conv_w': nrm(ks[21], (L, DN_CONV, 3 * DN_WIDTH), DN_CONV ** -0.5),
        'dn_a_log': a_log,
        'dn_dt_bias': dt_bias,
        'dn_norm_w': 1.0 + nrm(ks[24], (L, DN_HEAD_DIM), 0.02),
        'w_branch_s5': nrm(ks[25], (L, S5_WIDTH, D), S5_WIDTH ** -0.5),
        'w_branch_dn': nrm(ks[26], (L, DN_WIDTH, D), DN_WIDTH ** -0.5),
        'w_out': nrm(ks[27], (L, D, D), D ** -0.5),
        'peer_w_q': nrm(ks[28], (L, D, PEER_HEADS * 2 * PEER_KEY_DIM), D ** -0.5),
        'peer_sub_keys': nrm(ks[29], (L, PEER_HEADS, 2, PEER_N_KEYS, PEER_KEY_DIM), PEER_KEY_DIM ** -0.5),
        'peer_u': nrm(ks[30], (L, PEER_N_EXPERTS, D), D ** -0.5),
        'peer_v': nrm(ks[31], (L, PEER_N_EXPERTS, D), (PEER_HEADS * PEER_TOPK) ** -0.5),
    }


def reference(x, c, ctx, c_ctx, w_mod, b_mod, norm_pre_mix, norm_post_mix, norm_pre_ffn,
              norm_post_ffn, w_in, s5_lam_re, s5_lam_im, s5_log_step, s5_b_re, s5_b_im,
              s5_c_re, s5_c_im, s5_d, s5_glu_w, s5_glu_b, dn_conv_w, dn_a_log, dn_dt_bias,
              dn_norm_w, w_branch_s5, w_branch_dn, w_out, peer_w_q, peer_sub_keys, peer_u, peer_v):
    rows = x.shape[1] // GRID_W
    for l in range(DEPTH):
        p = {
            'w_mod': w_mod[l], 'b_mod': b_mod[l],
            'norm_pre_mix': norm_pre_mix[l], 'norm_post_mix': norm_post_mix[l],
            'norm_pre_ffn': norm_pre_ffn[l], 'norm_post_ffn': norm_post_ffn[l],
            'w_in': w_in[l],
            's5_lam_re': s5_lam_re[l], 's5_lam_im': s5_lam_im[l], 's5_log_step': s5_log_step[l],
            's5_b_re': s5_b_re[l], 's5_b_im': s5_b_im[l], 's5_c_re': s5_c_re[l], 's5_c_im': s5_c_im[l],
            's5_d': s5_d[l], 's5_glu_w': s5_glu_w[l], 's5_glu_b': s5_glu_b[l],
            'dn_conv_w': dn_conv_w[l], 'dn_a_log': dn_a_log[l], 'dn_dt_bias': dn_dt_bias[l],
            'dn_norm_w': dn_norm_w[l],
            'w_branch_s5': w_branch_s5[l], 'w_branch_dn': w_branch_dn[l], 'w_out': w_out[l],
            'peer_w_q': peer_w_q[l], 'peer_sub_keys': peer_sub_keys[l],
            'peer_u': peer_u[l], 'peer_v': peer_v[l],
        }
        x, ctx = hybrid_layer(x, ctx, c, c_ctx, p, rows, l == DEPTH - 1)
    return x
```

```python
import functools
import math

import jax
import jax.numpy as jnp
from jax import lax
from jax.experimental import pallas as pl
from jax.experimental.pallas import tpu as pltpu

F32 = jnp.float32
BF16 = jnp.bfloat16

D_MODEL = 1024
NORM_EPS = 1e-6
GRID_W = 64
S5_WIDTH = 512
S5_GROUP = 16
S5_GROUPS = 32
S5_STATE = 64
S5_COLS = S5_GROUPS * S5_STATE
DN_HEADS = 8
DN_HEAD_DIM = 128
DN_WIDTH = 1024
DN_CONV = 5
DN_CHUNK = 64
PEER_HEADS = 8
PEER_N_KEYS = 128
PEER_KEY_DIM = 128
PEER_TOPK = 16
LANES = 128

COL_U = 0
COL_K = 512
COL_V = 1536
COL_AB = 2560
COL_Q = 2592
COL_Z = 3616
COL_G5 = 4640
COL_GD = 5664

VMEM_LIMIT = 56 * 1024 * 1024


def _dot(a, b):
    return jnp.dot(a, b, preferred_element_type=F32)


def _dot_nt(a, b):
    return lax.dot_general(a, b, (((1,), (1,)), ((), ())), preferred_element_type=F32)


def _dot_tn(a, b):
    return lax.dot_general(a, b, (((0,), (0,)), ((), ())), preferred_element_type=F32)


def _sigmoid(x):
    return jax.nn.sigmoid(x)


def _silu(x):
    return x * _sigmoid(x)


def _gelu(x):
    c = math.sqrt(2.0 / math.pi)
    return x * (0.5 * (1.0 + jnp.tanh(c * (x + 0.044715 * (x * x * x)))))


def _softplus(x):
    return jnp.maximum(x, 0.0) + jnp.log1p(jnp.exp(-jnp.abs(x)))


def _rms(x):
    return x * lax.rsqrt(jnp.mean(x * x, axis=-1, keepdims=True) + NORM_EPS)


def _params(sem):
    return pltpu.CompilerParams(dimension_semantics=sem, vmem_limit_bytes=VMEM_LIMIT)


def _adaln_kernel(c_ref, w_ref, b_ref, o_ref):
    c = c_ref[...]
    o_ref[...] = _dot(_silu(c).astype(BF16), w_ref[...]) + b_ref[...]


def _adaln(cv, w_mod, b_mod):
    rows = cv.shape[0]
    ncol = w_mod.shape[1] // D_MODEL
    return pl.pallas_call(
        _adaln_kernel,
        grid=(ncol,),
        in_specs=[pl.BlockSpec((rows, D_MODEL), lambda j: (0, 0)),
                  pl.BlockSpec((D_MODEL, D_MODEL), lambda j: (0, j)),
                  pl.BlockSpec((1, D_MODEL), lambda j: (0, j))],
        out_specs=pl.BlockSpec((rows, D_MODEL), lambda j: (0, j)),
        out_shape=jax.ShapeDtypeStruct((rows, w_mod.shape[1]), F32),
        compiler_params=_params(("arbitrary",)),
        name="adaln",
    )(cv, w_mod, b_mod)


def _inproj_kernel(x_ref, mod_ref, gain_ref, wm_ref, wu_ref, wab_ref, wabt_ref,
                   main_ref, u_ref, ab_ref, abt_ref, hx_scr):
    j = pl.program_id(2)

    @pl.when(j == 0)
    def _():
        h = _rms(x_ref[...]) * gain_ref[...]
        h = h * (1.0 + mod_ref[1:2, :]) + mod_ref[0:1, :]
        hb = h.astype(BF16)
        hx_scr[...] = hb
        u_ref[...] = _dot(hb, wu_ref[...])
        ab_ref[...] = _dot(hb, wab_ref[...])
        abt_ref[...] = _dot_nt(wabt_ref[...], hb)

    main_ref[...] = _dot(hx_scr[...], wm_ref[...])


def _inproj(x, mods, gain, w_main, w_u, w_ab, w_abt, tm, tn):
    bsz, t_len, d = x.shape
    cm = w_main.shape[1]
    grid = (bsz, t_len // tm, cm // tn)
    return pl.pallas_call(
        _inproj_kernel,
        grid=grid,
        in_specs=[pl.BlockSpec((None, tm, d), lambda b, i, j: (b, i, 0)),
                  pl.BlockSpec((None, 6, d), lambda b, i, j: (b, 0, 0)),
                  pl.BlockSpec((1, d), lambda b, i, j: (0, 0)),
                  pl.BlockSpec((d, tn), lambda b, i, j: (0, j)),
                  pl.BlockSpec((d, S5_WIDTH), lambda b, i, j: (0, 0)),
                  pl.BlockSpec((d, LANES), lambda b, i, j: (0, 0)),
                  pl.BlockSpec((32, d), lambda b, i, j: (0, 0))],
        out_specs=[pl.BlockSpec((None, tm, tn), lambda b, i, j: (b, i, j)),
                   pl.BlockSpec((None, tm, S5_WIDTH), lambda b, i, j: (b, i, 0)),
                   pl.BlockSpec((None, tm, LANES), lambda b, i, j: (b, i, 0)),
                   pl.BlockSpec((None, 32, tm), lambda b, i, j: (b, 0, i))],
        out_shape=[jax.ShapeDtypeStruct((bsz, t_len, cm), F32),
                   jax.ShapeDtypeStruct((bsz, t_len, S5_WIDTH), F32),
                   jax.ShapeDtypeStruct((bsz, t_len, LANES), F32),
                   jax.ShapeDtypeStruct((bsz, 32, t_len), F32)],
        scratch_shapes=[pltpu.VMEM((tm, d), BF16)],
        compiler_params=_params(("arbitrary", "arbitrary", "arbitrary")),
        name="inproj",
    )(x, mods, gain, w_main, w_u, w_ab, w_abt)


def _s5_kernel(uf_ref, ub_ref, wbf_ref, wbb_ref, lamf_ref, lamb_ref, cf_ref, cb_ref,
               h0f_ref, h0b_ref, *rest, tc, bsz, need_out):
    if need_out:
        yf_ref, yb_ref, hff_ref, hfb_ref, xsf, xsb, hcf, hcb = rest
    else:
        hff_ref, hfb_ref, xsf, xsb, hcf, hcb = rest
        yf_ref = yb_ref = None
    i = pl.program_id(0)
    half = S5_COLS
    cw = 512
    nblk = half // cw

    @pl.when(i == 0)
    def _():
        hcf[...] = h0f_ref[...]
        hcb[...] = h0b_ref[...]

    dirs = ((uf_ref, wbf_ref, lamf_ref, cf_ref, yf_ref, xsf, hcf, False),
            (ub_ref, wbb_ref, lamb_ref, cb_ref, yb_ref, xsb, hcb, True))
    for u_ref, wb_ref, lam_ref, c_ref, y_ref, xs, hc, rev in dirs:
        for j in range(4):
            lhs = jnp.concatenate(
                [u_ref[:, t * S5_WIDTH + j * LANES: t * S5_WIDTH + (j + 1) * LANES] for t in range(tc)],
                axis=0).astype(BF16)
            r = _dot(lhs, wb_ref[j])
            xs[:, j * cw:(j + 1) * cw] = r[:, :cw]
            xs[:, half + j * cw: half + (j + 1) * cw] = r[:, cw:]
        for cb in range(nblk):
            lo, hi = cb * cw, (cb + 1) * cw
            ar = jnp.broadcast_to(lam_ref[0:1, lo:hi], (bsz, cw))
            ai = jnp.broadcast_to(lam_ref[1:2, lo:hi], (bsz, cw))

            def body(s, carry, lo=lo, hi=hi, ar=ar, ai=ai, xs=xs, rev=rev):
                hr, hi_ = carry
                t = (tc - 1 - s) if rev else s
                row = pl.multiple_of(t * bsz, bsz)
                xr = xs[pl.ds(row, bsz), lo:hi]
                xi = xs[pl.ds(row, bsz), half + lo: half + hi]
                nr = ar * hr - ai * hi_ + xr
                ni = ar * hi_ + ai * hr + xi
                xs[pl.ds(row, bsz), lo:hi] = nr
                xs[pl.ds(row, bsz), half + lo: half + hi] = ni
                return nr, ni

            hr, hi_ = lax.fori_loop(0, tc, body, (hc[:, lo:hi], hc[:, half + lo: half + hi]))
            hc[:, lo:hi] = hr
            hc[:, half + lo: half + hi] = hi_
        if need_out:
            for j in range(4):
                hre = xs[:, j * cw:(j + 1) * cw].astype(BF16)
                him = xs[:, half + j * cw: half + (j + 1) * cw].astype(BF16)
                y = _dot(hre, c_ref[j, 0]) + _dot(him, c_ref[j, 1])
                for t in range(tc):
                    y_ref[:, t * S5_WIDTH + j * LANES: t * S5_WIDTH + (j + 1) * LANES] = y[t * bsz:(t + 1) * bsz]
    hff_ref[...] = hcf[...]
    hfb_ref[...] = hcb[...]


def _s5(u2, wbf, wbb, lamf, lamb, cf, cb, h0f, h0b, tc, need_out):
    bsz = u2.shape[0]
    t_len = u2.shape[1] // S5_WIDTH
    nc = t_len // tc
    blk = tc * S5_WIDTH
    const2 = lambda i: (0, 0)
    const3 = lambda i: (0, 0, 0)
    const4 = lambda i: (0, 0, 0, 0)
    st_spec = pl.BlockSpec((bsz, 2 * S5_COLS), const2)
    st_shape = jax.ShapeDtypeStruct((bsz, 2 * S5_COLS), F32)
    out_specs = [st_spec, st_spec]
    out_shape = [st_shape, st_shape]
    if need_out:
        y_shape = jax.ShapeDtypeStruct(u2.shape, F32)
        out_specs = [pl.BlockSpec((bsz, blk), lambda i: (0, i)),
                     pl.BlockSpec((bsz, blk), lambda i: (0, nc - 1 - i))] + out_specs
        out_shape = [y_shape, y_shape] + out_shape
    return pl.pallas_call(
        functools.partial(_s5_kernel, tc=tc, bsz=bsz, need_out=need_out),
        grid=(nc,),
        in_specs=[pl.BlockSpec((bsz, blk), lambda i: (0, i)),
                  pl.BlockSpec((bsz, blk), lambda i: (0, nc - 1 - i)),
                  pl.BlockSpec(wbf.shape, const3), pl.BlockSpec(wbb.shape, const3),
                  pl.BlockSpec(lamf.shape, const2), pl.BlockSpec(lamb.shape, const2),
                  pl.BlockSpec(cf.shape, const4), pl.BlockSpec(cb.shape, const4),
                  st_spec, st_spec],
        out_specs=out_specs,
        out_shape=out_shape,
        scratch_shapes=[pltpu.VMEM((tc * bsz, 2 * S5_COLS), F32),
                        pltpu.VMEM((tc * bsz, 2 * S5_COLS), F32),
                        pltpu.VMEM((bsz, 2 * S5_COLS), F32),
                        pltpu.VMEM((bsz, 2 * S5_COLS), F32)],
        compiler_params=_params(("arbitrary",)),
        name="s5_scan",
    )(u2, u2, wbf, wbb, lamf, lamb, cf, cb, h0f, h0b)


def _dn_kernel(kvq_ref, cw_ref, ab_ref, abt_ref, acoef_ref, dtb_ref, acoeft_ref, dtbt_ref, s0_ref,
               *rest, blk_len, row_len, rev, need_out):
    if need_out:
        o_ref, sfin_ref, s_scr, act_scr = rest
    else:
        sfin_ref, s_scr, act_scr = rest
        o_ref = None
    ncols = act_scr.shape[1]
    cs = DN_CHUNK
    nch = blk_len // cs
    goff = 16 if rev else 0
    boff = goff + 8

    @pl.when(pl.program_id(1) == 0)
    def _():
        s_scr[...] = s0_ref[...]

    tpos = lax.broadcasted_iota(jnp.int32, (blk_len, LANES), 0) % row_len
    valid = [(tpos + (j - 2) >= 0) & (tpos + (j - 2) < row_len) for j in range(DN_CONV)]
    for cbk in range(ncols // LANES):
        cols = slice(cbk * LANES, (cbk + 1) * LANES)
        xc = kvq_ref[:, cols]
        acc = xc * cw_ref[2:3, cols]
        for j in (0, 1, 3, 4):
            sh = (2 - j) % blk_len
            acc = acc + jnp.where(valid[j], pltpu.roll(xc, sh, axis=0), 0.0) * cw_ref[j:j + 1, cols]
        act_scr[:, cols] = _silu(acc)

    ab = ab_ref[...]
    g_cols = acoef_ref[...] * _softplus(ab + dtb_ref[...])
    beta_cols = _sigmoid(ab)
    g_rows = acoeft_ref[...] * _softplus(abt_ref[...] + dtbt_ref[...])

    ri = lax.broadcasted_iota(jnp.int32, (cs, cs), 0)
    ci = lax.broadcasted_iota(jnp.int32, (cs, cs), 1)
    incl = (ri <= ci) if rev else (ri >= ci)
    strict = incl & (ri != ci)
    tri = incl.astype(F32)
    eye = (ri == ci).astype(F32)
    last = 0 if rev else cs - 1
    scale = DN_HEAD_DIM ** -0.5

    order = range(nch - 1, -1, -1) if rev else range(nch)
    for c in order:
        rows = slice(c * cs, (c + 1) * cs)
        gc_cols = jnp.dot(tri, g_cols[rows, :], precision=lax.Precision.HIGHEST, preferred_element_type=F32)
        gc_rows = lax.dot_general(g_rows[:, rows], tri, (((1,), (1,)), ((), ())),
                                  precision=lax.Precision.HIGHEST, preferred_element_type=F32)
        for h in range(DN_HEADS):
            hc = slice(h * DN_HEAD_DIM, (h + 1) * DN_HEAD_DIM)
            gch = gc_cols[:, goff + h: goff + h + 1]
            gcr = gc_rows[goff + h: goff + h + 1, :]
            dec = jnp.where(incl, jnp.exp(jnp.minimum(gch - gcr, 0.0)), 0.0)
            bh = beta_cols[rows, boff + h: boff + h + 1]
            kh = act_scr[rows, hc]
            kh = kh * lax.rsqrt(jnp.sum(kh * kh, axis=-1, keepdims=True) + 1e-6)
            vh = act_scr[rows, DN_WIDTH + h * DN_HEAD_DIM: DN_WIDTH + (h + 1) * DN_HEAD_DIM]
            kb = kh * bh
            khb = kh.astype(BF16)
            a = jnp.where(strict, _dot_nt(kb.astype(BF16), khb) * dec, 0.0)
            p = eye - a
            m = _dot(a.astype(BF16), a.astype(BF16))
            for lvl in range(5):
                mb = m.astype(BF16)
                if lvl < 4:
                    r = _dot(jnp.concatenate([m, p], axis=0).astype(BF16), mb)
                    m = r[:cs]
                    p = p + r[cs:]
                else:
                    p = p + _dot(p.astype(BF16), mb)
            eg = jnp.exp(gch)
            uw = _dot(p.astype(BF16), jnp.concatenate([vh * bh, kb * eg], axis=-1).astype(BF16))
            u_i = uw[:, :DN_HEAD_DIM]
            w_i = uw[:, DN_HEAD_DIM:]
            gl = gch[last:last + 1, :]
            kd = kh * jnp.exp(gl - gch)
            s_h = s_scr[h]
            s_hb = s_h.astype(BF16)
            v_new = u_i - _dot(w_i.astype(BF16), s_hb)
            v_newb = v_new.astype(BF16)
            if need_out:
                qh = act_scr[rows, 2 * DN_WIDTH + h * DN_HEAD_DIM: 2 * DN_WIDTH + (h + 1) * DN_HEAD_DIM]
                qh = qh * lax.rsqrt(jnp.sum(qh * qh, axis=-1, keepdims=True) + 1e-6) * scale
                at = jnp.where(incl, _dot_nt(qh.astype(BF16), khb) * dec, 0.0)
                o_ref[rows, hc] = _dot((qh * eg).astype(BF16), s_hb) + _dot(at.astype(BF16), v_newb)
            s_scr[h] = s_h * jnp.exp(gl) + _dot_tn(kd.astype(BF16), v_newb)
    sfin_ref[...] = s_scr[...]


def _deltanet(kvq, ncols, conv_w, ab, abt, acoef, dtb, acoeft, dtbt, s0, blk_len, row_len, rev, need_out):
    bsz, t_len, _ = kvq.shape
    nb = t_len // blk_len
    bi = (lambda b, i: (b, nb - 1 - i, 0)) if rev else (lambda b, i: (b, i, 0))
    bit = (lambda b, i: (b, 0, nb - 1 - i)) if rev else (lambda b, i: (b, 0, i))
    c2 = lambda b, i: (0, 0)
    st_spec = pl.BlockSpec((None, DN_HEADS, DN_HEAD_DIM, DN_HEAD_DIM), lambda b, i: (b, 0, 0, 0))
    st_shape = jax.ShapeDtypeStruct((bsz, DN_HEADS, DN_HEAD_DIM, DN_HEAD_DIM), F32)
    out_specs = [st_spec]
    out_shape = [st_shape]
    if need_out:
        out_specs = [pl.BlockSpec((None, blk_len, DN_WIDTH), bi)] + out_specs
        out_shape = [jax.ShapeDtypeStruct((bsz, t_len, DN_WIDTH), F32)] + out_shape
    return pl.pallas_call(
        functools.partial(_dn_kernel, blk_len=blk_len, row_len=row_len, rev=rev, need_out=need_out),
        grid=(bsz, nb),
        in_specs=[pl.BlockSpec((None, blk_len, ncols), bi),
                  pl.BlockSpec((DN_CONV, ncols), c2),
                  pl.BlockSpec((None, blk_len, LANES), bi),
                  pl.BlockSpec((None, 32, blk_len), bit),
                  pl.BlockSpec((1, LANES), c2), pl.BlockSpec((1, LANES), c2),
                  pl.BlockSpec((32, 1), c2), pl.BlockSpec((32, 1), c2),
                  st_spec],
        out_specs=out_specs,
        out_shape=out_shape,
        scratch_shapes=[pltpu.VMEM((DN_HEADS, DN_HEAD_DIM, DN_HEAD_DIM), F32),
                        pltpu.VMEM((blk_len, ncols), F32)],
        compiler_params=_params(("arbitrary", "arbitrary")),
        name="deltanet_bwd" if rev else "deltanet_fwd",
    )(kvq, conv_w, ab, abt, acoef, dtb, acoeft, dtbt, s0)


def _post_kernel(x_ref, mod_ref, yf_ref, yb_ref, u_ref, of_ref, ob_ref, z_ref, g5_ref, gd_ref,
                 s5d_ref, glw_ref, glb_ref, dnw_ref, wb5_ref, wbd_ref, wo_ref, gain_ref, o_ref):
    y5 = _gelu(yf_ref[...] + yb_ref[...] + s5d_ref[...] * u_ref[...])
    y5 = y5 * _sigmoid(_dot(y5.astype(BF16), glw_ref[...]) + glb_ref[...])
    o = of_ref[...] + ob_ref[...]
    parts = []
    for h in range(DN_HEADS):
        oh = o[:, h * DN_HEAD_DIM:(h + 1) * DN_HEAD_DIM]
        parts.append(_rms(oh) * dnw_ref[...])
    y_dn = jnp.concatenate(parts, axis=-1) * _silu(z_ref[...])
    merged = (_sigmoid(g5_ref[...]) * _dot(y5.astype(BF16), wb5_ref[...])
              + _sigmoid(gd_ref[...]) * _dot(y_dn.astype(BF16), wbd_ref[...]))
    x_mix = _dot(merged.astype(BF16), wo_ref[...])
    o_ref[...] = x_ref[...] + mod_ref[2:3, :] * (_rms(x_mix) * gain_ref[...])


def _post(x, mods, yf, yb, u, o_f, o_b, main, s5d, glw, glb, dnw, wb5, wbd, wo, gain, tm):
    bsz, t_len, d = x.shape
    tok = lambda w: pl.BlockSpec((None, tm, w), lambda b, i: (b, i, 0))
    colblk = lambda k: pl.BlockSpec((None, tm, d), lambda b, i, k=k: (b, i, k))
    full = lambda a: pl.BlockSpec(a.shape, lambda b, i: (0,) * a.ndim)
    return pl.pallas_call(
        _post_kernel,
        grid=(bsz, t_len // tm),
        in_specs=[tok(d), pl.BlockSpec((None, 6, d), lambda b, i: (b, 0, 0)),
                  tok(S5_WIDTH), tok(S5_WIDTH), tok(S5_WIDTH), tok(d), tok(d),
                  colblk(3), colblk(4), colblk(5),
                  full(s5d), full(glw), full(glb), full(dnw), full(wb5), full(wbd), full(wo), full(gain)],
        out_specs=tok(d),
        out_shape=jax.ShapeDtypeStruct(x.shape, F32),
        compiler_params=_params(("arbitrary", "arbitrary")),
        name="mixer_out",
    )(x, mods, yf, yb, u, o_f, o_b, main, main, main, s5d, glw, glb, dnw, wb5, wbd, wo, gain)


_CAND_COUNTS = [PEER_TOPK // (a + 1) for a in range(PEER_TOPK)]
_N_CAND = sum(_CAND_COUNTS)
_CAND_ROWS = -(-_N_CAND // 8) * 8


def _extract_topk(work, k, out_vals_ref=None):
    nrows = work.shape[0]
    iota = lax.broadcasted_iota(jnp.int32, work.shape, 0)
    vals = []
    for r in range(k):
        m = jnp.max(work, axis=0, keepdims=True)
        idx = jnp.min(jnp.where(work == m, iota, nrows), axis=0, keepdims=True)
        work = jnp.where(iota == idx, -jnp.inf, work)
        vals.append(m)
        if out_vals_ref is not None:
            out_vals_ref[r:r + 1, :] = m
    return vals, work


def _route_kernel(x_ref, mod_ref, gain_ref, wq_ref, keys_ref,
                  ht_ref, s1_ref, e1_ref, s2_ref, p2_ref, tau_ref, sv_scr, cand_scr):
    h = _rms(x_ref[...]) * gain_ref[...]
    h = h * (1.0 + mod_ref[4:5, :]) + mod_ref[3:4, :]
    hb = h.astype(BF16)
    ht_ref[...] = h.T.astype(BF16)
    q = _dot(hb, wq_ref[...]).astype(BF16)
    tm = q.shape[0]
    for hd in range(PEER_HEADS):
        svals = []
        works = []
        scores = []
        for j in range(2):
            col = (hd * 2 + j) * PEER_KEY_DIM
            s = _dot_nt(keys_ref[hd, j], q[:, col: col + PEER_KEY_DIM])
            vals, work = _extract_topk(s, PEER_TOPK, sv_scr.at[j])
            scores.append(s)
            svals.append(vals)
            works.append(work)
        cand_scr[...] = jnp.full(cand_scr.shape, -jnp.inf, F32)
        off = 0
        for a in range(PEER_TOPK):
            nb = _CAND_COUNTS[a]
            cand_scr[off: off + nb, :] = svals[0][a] + sv_scr[1, 0:nb, :]
            off += nb
        top, _ = _extract_topk(cand_scr[...], PEER_TOPK)
        m0 = top[0]
        z = jnp.exp(top[0] - m0)
        for r in range(1, PEER_TOPK):
            z = z + jnp.exp(top[r] - m0)
        tau_ref[hd:hd + 1, :] = top[PEER_TOPK - 1]
        s1_ref[hd] = scores[0]
        s2_ref[hd] = scores[1]
        e1_ref[hd] = jnp.where(works[0] == -jnp.inf, jnp.exp(scores[0] - svals[0][0]), 0.0)
        p2_ref[hd] = jnp.where(works[1] == -jnp.inf, jnp.exp(scores[1] - svals[1][0]), 0.0) / z


def _route(x1, mods, gain, wq, keys, tm):
    bsz, t_len, d = x1.shape
    n_tok = bsz * t_len
    nt = t_len // tm
    tile = lambda b, i: (0, 0, b * nt + i)
    hk = pl.BlockSpec((PEER_HEADS, PEER_N_KEYS, tm), tile)
    hk_shape = jax.ShapeDtypeStruct((PEER_HEADS, PEER_N_KEYS, n_tok), F32)
    return pl.pallas_call(
        _route_kernel,
        grid=(bsz, nt),
        in_specs=[pl.BlockSpec((None, tm, d), lambda b, i: (b, i, 0)),
                  pl.BlockSpec((None, 6, d), lambda b, i: (b, 0, 0)),
                  pl.BlockSpec((1, d), lambda b, i: (0, 0)),
                  pl.BlockSpec(wq.shape, lambda b, i: (0, 0)),
                  pl.BlockSpec(keys.shape, lambda b, i: (0, 0, 0, 0))],
        out_specs=[pl.BlockSpec((d, tm), lambda b, i: (0, b * nt + i)),
                   hk, hk, hk, hk,
                   pl.BlockSpec((PEER_HEADS, tm), lambda b, i: (0, b * nt + i))],
        out_shape=[jax.ShapeDtypeStruct((d, n_tok), BF16),
                   hk_shape, hk_shape, hk_shape, hk_shape,
                   jax.ShapeDtypeStruct((PEER_HEADS, n_tok), F32)],
        scratch_shapes=[pltpu.VMEM((2, PEER_TOPK, tm), F32),
                        pltpu.VMEM((_CAND_ROWS, tm), F32)],
        compiler_params=_params(("arbitrary", "arbitrary")),
        name="peer_route",
    )(x1, mods, gain, wq, keys)


def _peer_kernel(ht_ref, u_ref, vt_ref, s1_ref, e1_ref, s2_ref, p2_ref, tau_ref, x1_ref, mod_ref, gain_ref,
                 o_ref, acc, *, nsub):
    e = pl.program_id(2)

    @pl.when(e == 0)
    def _():
        acc[...] = jnp.zeros(acc.shape, F32)

    g = _gelu(_dot(u_ref[...], ht_ref[...]))
    parts = []
    for ii in range(nsub):
        w = None
        for hd in range(PEER_HEADS):
            s = s2_ref[hd] + s1_ref[hd, ii:ii + 1, :]
            contrib = jnp.where(s >= tau_ref[hd:hd + 1, :], p2_ref[hd] * e1_ref[hd, ii:ii + 1, :], 0.0)
            w = contrib if w is None else w + contrib
        parts.append((g[ii * PEER_N_KEYS:(ii + 1) * PEER_N_KEYS] * w).astype(BF16))
    acc[...] += _dot(vt_ref[...], jnp.concatenate(parts, axis=0))

    @pl.when(e == pl.num_programs(2) - 1)
    def _():
        out = acc[...].T
        o_ref[...] = x1_ref[...] + mod_ref[5:6, :] * (_rms(out) * gain_ref[...])


def _peer(ht, u_tab, vt_tab, s1, e1, s2, p2, tau, x1, mods, gain, tn, eb):
    bsz, t_len, d = x1.shape
    nt = t_len // tn
    n_exp = u_tab.shape[0]
    nsub = eb // PEER_N_KEYS
    ne = n_exp // eb
    n_tok = bsz * t_len
    s1v = s1.reshape(PEER_HEADS, ne, nsub, n_tok)
    e1v = e1.reshape(PEER_HEADS, ne, nsub, n_tok)
    sub = pl.BlockSpec((PEER_HEADS, None, nsub, tn), lambda b, i, e: (0, e, 0, b * nt + i))
    hk = pl.BlockSpec((PEER_HEADS, PEER_N_KEYS, tn), lambda b, i, e: (0, 0, b * nt + i))
    return pl.pallas_call(
        functools.partial(_peer_kernel, nsub=nsub),
        grid=(bsz, nt, ne),
        in_specs=[pl.BlockSpec((d, tn), lambda b, i, e: (0, b * nt + i)),
                  pl.BlockSpec((eb, d), lambda b, i, e: (e, 0)),
                  pl.BlockSpec((d, eb), lambda b, i, e: (0, e)),
                  sub, sub, hk, hk,
                  pl.BlockSpec((PEER_HEADS, tn), lambda b, i, e: (0, b * nt + i)),
                  pl.BlockSpec((None, tn, d), lambda b, i, e: (b, i, 0)),
                  pl.BlockSpec((None, 6, d), lambda b, i, e: (b, 0, 0)),
                  pl.BlockSpec((1, d), lambda b, i, e: (0, 0))],
        out_specs=pl.BlockSpec((None, tn, d), lambda b, i, e: (b, i, 0)),
        out_shape=jax.ShapeDtypeStruct(x1.shape, F32),
        scratch_shapes=[pltpu.VMEM((d, tn), F32)],
        compiler_params=_params(("arbitrary", "arbitrary", "arbitrary")),
        name="peer_experts",
    )(ht, u_tab, vt_tab, s1v, e1v, s2, p2, tau, x1, mods, gain)


def _s5_coeffs(lam_re, lam_im, log_step, b_re, b_im, c_re, c_im):
    step = jnp.exp(log_step)[:, None]
    mag = jnp.exp(lam_re * step)
    ab_re, ab_im = mag * jnp.cos(lam_im * step), mag * jnp.sin(lam_im * step)
    den = lam_re * lam_re + lam_im * lam_im
    f_re = ((ab_re - 1.0) * lam_re + ab_im * lam_im) / den
    f_im = (ab_im * lam_re - (ab_re - 1.0) * lam_im) / den
    bb_re = f_re[..., None] * b_re - f_im[..., None] * b_im
    bb_im = f_re[..., None] * b_im + f_im[..., None] * b_re
    lam = jnp.stack([ab_re.reshape(-1), ab_im.reshape(-1)])
    eye8 = jnp.eye(8, dtype=F32)

    def in_blockdiag(bb):
        t = bb.reshape(4, 8, S5_STATE, S5_GROUP).transpose(0, 1, 3, 2)
        return jnp.einsum('jghp,gk->jghkp', t, eye8).reshape(4, LANES, 8 * S5_STATE)

    def out_blockdiag(cc):
        t = cc.reshape(4, 8, S5_GROUP, S5_STATE).transpose(0, 1, 3, 2)
        return jnp.einsum('jgph,gk->jgpkh', t, eye8).reshape(4, 8 * S5_STATE, LANES)

    wb = jnp.concatenate([in_blockdiag(bb_re), in_blockdiag(bb_im)], axis=-1).astype(BF16)
    cmat = jnp.stack([out_blockdiag(c_re), -out_blockdiag(c_im)], axis=1).astype(BF16)
    return wb, lam, cmat


def _gate_rows(a_log, dt_bias):
    neg_a = -jnp.exp(a_log)
    acoef = jnp.zeros((LANES,), F32).at[0:8].set(neg_a[0]).at[16:24].set(neg_a[1])
    dtb = jnp.zeros((LANES,), F32).at[0:8].set(dt_bias[0]).at[16:24].set(dt_bias[1])
    return acoef[None, :], dtb[None, :], acoef[:32, None], dtb[:32, None]


def kernel(x, c, ctx, c_ctx, w_mod, b_mod, norm_pre_mix, norm_post_mix, norm_pre_ffn, norm_post_ffn, w_in, s5_lam_re, s5_lam_im, s5_log_step, s5_b_re, s5_b_im, s5_c_re, s5_c_im, s5_d, s5_glu_w, s5_glu_b, dn_conv_w, dn_a_log, dn_dt_bias, dn_norm_w, w_branch_s5, w_branch_dn, w_out, peer_w_q, peer_sub_keys, peer_u, peer_v):
    bsz, t_len, d = x.shape
    ctx_len = ctx.shape[1]
    l = 0

    pad = (-(bsz + 1)) % 8
    cv = jnp.concatenate([c, c_ctx[None, :], jnp.zeros((pad, d), F32)], axis=0)
    mod_all = _adaln(cv, w_mod[l].astype(BF16), b_mod[l][None, :])
    mods = mod_all[:bsz].reshape(bsz, 6, d)
    mods_ctx = jnp.broadcast_to(mod_all[bsz].reshape(1, 6, d), (bsz, 6, d))

    w = w_in[l]
    w_main = jnp.concatenate([w[:, COL_K:COL_AB], w[:, COL_Q:]], axis=1).astype(BF16)
    w_ctx = w[:, COL_K:COL_AB].astype(BF16)
    w_u = w[:, COL_U:COL_K].astype(BF16)
    w_ab32 = w[:, COL_AB:COL_Q]
    w_ab = jnp.pad(w_ab32, ((0, 0), (0, LANES - 32))).astype(BF16)
    w_abt = w_ab32.T.astype(BF16)
    gain_pre = norm_pre_mix[l][None, :]

    main_c, u_c, ab_c, abt_c = _inproj(ctx, mods_ctx, gain_pre, w_ctx, w_u, w_ab, w_abt, tm=ctx_len, tn=2 * DN_WIDTH)
    main_x, u_x, ab_x, abt_x = _inproj(x, mods, gain_pre, w_main, w_u, w_ab, w_abt, tm=min(1024, t_len), tn=1536)

    wbf, lamf, cf = _s5_coeffs(s5_lam_re[l, 0], s5_lam_im[l, 0], s5_log_step[l, 0], s5_b_re[l, 0], s5_b_im[l, 0],
                               s5_c_re[l, 0], s5_c_im[l, 0])
    wbb, lamb, cb = _s5_coeffs(s5_lam_re[l, 1], s5_lam_im[l, 1], s5_log_step[l, 1], s5_b_re[l, 1], s5_b_im[l, 1],
                               s5_c_re[l, 1], s5_c_im[l, 1])
    zeros_h = jnp.zeros((bsz, 2 * S5_COLS), F32)
    tc = 32
    hf_c, hb_c = _s5(u_c.reshape(bsz, ctx_len * S5_WIDTH), wbf, wbb, lamf, lamb, cf, cb, zeros_h, zeros_h, tc, False)
    yf, yb, _, _ = _s5(u_x.reshape(bsz, t_len * S5_WIDTH), wbf, wbb, lamf, lamb, cf, cb, hf_c, hb_c, tc, True)
    yf = yf.reshape(bsz, t_len, S5_WIDTH)
    yb = yb.reshape(bsz, t_len, S5_WIDTH)

    acoef, dtb, acoeft, dtbt = _gate_rows(dn_a_log[l], dn_dt_bias[l])
    conv_w = dn_conv_w[l]
    zeros_s = jnp.zeros((bsz, DN_HEADS, DN_HEAD_DIM, DN_HEAD_DIM), F32)
    dn_args = (acoef, dtb, acoeft, dtbt)
    (sf_c,) = _deltanet(main_c, 2 * DN_WIDTH, conv_w[:, :2 * DN_WIDTH], ab_c, abt_c, *dn_args, zeros_s,
                        ctx_len, ctx_len, False, False)
    (sb_c,) = _deltanet(main_c, 2 * DN_WIDTH, conv_w[:, :2 * DN_WIDTH], ab_c, abt_c, *dn_args, zeros_s,
                        ctx_len, ctx_len, True, False)
    blk_len = min(256, t_len)
    o_f, _ = _deltanet(main_x, 3 * DN_WIDTH, conv_w, ab_x, abt_x, *dn_args, sf_c, blk_len, GRID_W, False, True)
    o_b, _ = _deltanet(main_x, 3 * DN_WIDTH, conv_w, ab_x, abt_x, *dn_args, sb_c, blk_len, GRID_W, True, True)

    x1 = _post(x, mods, yf, yb, u_x, o_f, o_b, main_x,
               s5_d[l][None, :], s5_glu_w[l].astype(BF16), s5_glu_b[l][None, :], dn_norm_w[l][None, :],
               w_branch_s5[l].astype(BF16), w_branch_dn[l].astype(BF16), w_out[l].astype(BF16),
               norm_post_mix[l][None, :], tm=min(512, t_len))

    ht, s1, e1, s2, p2, tau = _route(x1, mods, norm_pre_ffn[l][None, :], peer_w_q[l].astype(BF16),
                                     peer_sub_keys[l].astype(BF16), tm=min(256, t_len))
    out = _peer(ht, peer_u[l].astype(BF16), peer_v[l].T.astype(BF16), s1, e1, s2, p2, tau, x1, mods,
                norm_post_ffn[l][None, :], tn=min(512, t_len), eb=512)
    return out
```

```python
import functools
import math

import jax
import jax.numpy as jnp
from jax import lax
from jax.experimental import pallas as pl
from jax.experimental.pallas import tpu as pltpu

F32 = jnp.float32
BF16 = jnp.bfloat16

D_MODEL = 1024
NORM_EPS = 1e-6
GRID_W = 64
S5_WIDTH = 512
S5_GROUP = 16
S5_GROUPS = 32
S5_STATE = 64
S5_COLS = S5_GROUPS * S5_STATE
DN_HEADS = 8
DN_HEAD_DIM = 128
DN_WIDTH = 1024
DN_CONV = 5
DN_CHUNK = 64
PEER_HEADS = 8
PEER_N_KEYS = 128
PEER_KEY_DIM = 128
PEER_TOPK = 16
LANES = 128

COL_U = 0
COL_K = 512
COL_V = 1536
COL_AB = 2560
COL_Q = 2592
COL_Z = 3616
COL_G5 = 4640
COL_GD = 5664

VMEM_LIMIT = 56 * 1024 * 1024


def _dot(a, b):
    return jnp.dot(a, b, preferred_element_type=F32)


def _dot_nt(a, b):
    return lax.dot_general(a, b, (((1,), (1,)), ((), ())), preferred_element_type=F32)


def _dot_tn(a, b):
    return lax.dot_general(a, b, (((0,), (0,)), ((), ())), preferred_element_type=F32)


def _sigmoid(x):
    return jax.nn.sigmoid(x)


def _silu(x):
    return x * _sigmoid(x)


def _gelu(x):
    c = math.sqrt(2.0 / math.pi)
    return x * (0.5 * (1.0 + jnp.tanh(c * (x + 0.044715 * (x * x * x)))))


def _softplus(x):
    return jnp.maximum(x, 0.0) + jnp.log1p(jnp.exp(-jnp.abs(x)))


def _rms(x):
    return x * lax.rsqrt(jnp.mean(x * x, axis=-1, keepdims=True) + NORM_EPS)


def _params(sem):
    return pltpu.CompilerParams(dimension_semantics=sem, vmem_limit_bytes=VMEM_LIMIT)


def _adaln_kernel(c_ref, w_ref, b_ref, o_ref):
    c = c_ref[...]
    o_ref[...] = _dot(_silu(c).astype(BF16), w_ref[...]) + b_ref[...]


def _adaln(cv, w_mod, b_mod):
    rows = cv.shape[0]
    ncol = w_mod.shape[1] // D_MODEL
    return pl.pallas_call(
        _adaln_kernel,
        grid=(ncol,),
        in_specs=[pl.BlockSpec((rows, D_MODEL), lambda j: (0, 0)),
                  pl.BlockSpec((D_MODEL, D_MODEL), lambda j: (0, j)),
                  pl.BlockSpec((1, D_MODEL), lambda j: (0, j))],
        out_specs=pl.BlockSpec((rows, D_MODEL), lambda j: (0, j)),
        out_shape=jax.ShapeDtypeStruct((rows, w_mod.shape[1]), F32),
        compiler_params=_params(("arbitrary",)),
        name="adaln",
    )(cv, w_mod, b_mod)


def _inproj_kernel(x_ref, mod_ref, gain_ref, wm_ref, wu_ref, wab_ref, wabt_ref,
                   main_ref, u_ref, ab_ref, abt_ref, hx_scr):
    j = pl.program_id(2)

    @pl.when(j == 0)
    def _():
        h = _rms(x_ref[...]) * gain_ref[...]
        h = h * (1.0 + mod_ref[1:2, :]) + mod_ref[0:1, :]
        hb = h.astype(BF16)
        hx_scr[...] = hb
        u_ref[...] = _dot(hb, wu_ref[...])
        ab_ref[...] = _dot(hb, wab_ref[...])
        abt_ref[...] = _dot_nt(wabt_ref[...], hb)

    main_ref[...] = _dot(hx_scr[...], wm_ref[...])


def _inproj(x, mods, gain, w_main, w_u, w_ab, w_abt, tm, tn):
    bsz, t_len, d = x.shape
    cm = w_main.shape[1]
    grid = (bsz, t_len // tm, cm // tn)
    return pl.pallas_call(
        _inproj_kernel,
        grid=grid,
        in_specs=[pl.BlockSpec((None, tm, d), lambda b, i, j: (b, i, 0)),
                  pl.BlockSpec((None, 6, d), lambda b, i, j: (b, 0, 0)),
                  pl.BlockSpec((1, d), lambda b, i, j: (0, 0)),
                  pl.BlockSpec((d, tn), lambda b, i, j: (0, j)),
                  pl.BlockSpec((d, S5_WIDTH), lambda b, i, j: (0, 0)),
                  pl.BlockSpec((d, LANES), lambda b, i, j: (0, 0)),
                  pl.BlockSpec((32, d), lambda b, i, j: (0, 0))],
        out_specs=[pl.BlockSpec((None, tm, tn), lambda b, i, j: (b, i, j)),
                   pl.BlockSpec((None, tm, S5_WIDTH), lambda b, i, j: (b, i, 0)),
                   pl.BlockSpec((None, tm, LANES), lambda b, i, j: (b, i, 0)),
                   pl.BlockSpec((None, 32, tm), lambda b, i, j: (b, 0, i))],
        out_shape=[jax.ShapeDtypeStruct((bsz, t_len, cm), F32),
                   jax.ShapeDtypeStruct((bsz, t_len, S5_WIDTH), F32),
                   jax.ShapeDtypeStruct((bsz, t_len, LANES), F32),
                   jax.ShapeDtypeStruct((bsz, 32, t_len), F32)],
        scratch_shapes=[pltpu.VMEM((tm, d), BF16)],
        compiler_params=_params(("arbitrary", "arbitrary", "arbitrary")),
        name="inproj",
    )(x, mods, gain, w_main, w_u, w_ab, w_abt)


def _s5_kernel(uf_ref, ub_ref, wbf_ref, wbb_ref, lamf_ref, lamb_ref, cf_ref, cb_ref,
               h0f_ref, h0b_ref, *rest, tc, bsz, need_out):
    if need_out:
        yf_ref, yb_ref, hff_ref, hfb_ref, xsf, xsb, hcf, hcb = rest
    else:
        hff_ref, hfb_ref, xsf, xsb, hcf, hcb = rest
        yf_ref = yb_ref = None
    i = pl.program_id(0)
    half = S5_COLS
    cw = 512
    nblk = half // cw

    @pl.when(i == 0)
    def _():
        hcf[...] = h0f_ref[...]
        hcb[...] = h0b_ref[...]

    dirs = ((uf_ref, wbf_ref, lamf_ref, cf_ref, yf_ref, xsf, hcf, False),
            (ub_ref, wbb_ref, lamb_ref, cb_ref, yb_ref, xsb, hcb, True))
    for u_ref, wb_ref, lam_ref, c_ref, y_ref, xs, hc, rev in dirs:
        for j in range(4):
            lhs = jnp.concatenate(
                [u_ref[:, t * S5_WIDTH + j * LANES: t * S5_WIDTH + (j + 1) * LANES] for t in range(tc)],
                axis=0).astype(BF16)
            r = _dot(lhs, wb_ref[j])
            xs[:, j * cw:(j + 1) * cw] = r[:, :cw]
            xs[:, half + j * cw: half + (j + 1) * cw] = r[:, cw:]
        for cb in range(nblk):
            lo, hi = cb * cw, (cb + 1) * cw
            ar = jnp.broadcast_to(lam_ref[0:1, lo:hi], (bsz, cw))
            ai = jnp.broadcast_to(lam_ref[1:2, lo:hi], (bsz, cw))

            def body(s, carry, lo=lo, hi=hi, ar=ar, ai=ai, xs=xs, rev=rev):
                hr, hi_ = carry
                t = (tc - 1 - s) if rev else s
                row = pl.multiple_of(t * bsz, bsz)
                xr = xs[pl.ds(row, bsz), lo:hi]
                xi = xs[pl.ds(row, bsz), half + lo: half + hi]
                nr = ar * hr - ai * hi_ + xr
                ni = ar * hi_ + ai * hr + xi
                xs[pl.ds(row, bsz), lo:hi] = nr
                xs[pl.ds(row, bsz), half + lo: half + hi] = ni
                return nr, ni

            hr, hi_ = lax.fori_loop(0, tc, body, (hc[:, lo:hi], hc[:, half + lo: half + hi]))
            hc[:, lo:hi] = hr
            hc[:, half + lo: half + hi] = hi_
        if need_out:
            for j in range(4):
                hre = xs[:, j * cw:(j + 1) * cw].astype(BF16)
                him = xs[:, half + j * cw: half + (j + 1) * cw].astype(BF16)
                y = _dot(hre, c_ref[j, 0]) + _dot(him, c_ref[j, 1])
                for t in range(tc):
                    y_ref[:, t * S5_WIDTH + j * LANES: t * S5_WIDTH + (j + 1) * LANES] = y[t * bsz:(t + 1) * bsz]
    hff_ref[...] = hcf[...]
    hfb_ref[...] = hcb[...]


def _s5(u2, wbf, wbb, lamf, lamb, cf, cb, h0f, h0b, tc, need_out):
    bsz = u2.shape[0]
    t_len = u2.shape[1] // S5_WIDTH
    nc = t_len // tc
    blk = tc * S5_WIDTH
    const2 = lambda i: (0, 0)
    const3 = lambda i: (0, 0, 0)
    const4 = lambda i: (0, 0, 0, 0)
    st_spec = pl.BlockSpec((bsz, 2 * S5_COLS), const2)
    st_shape = jax.ShapeDtypeStruct((bsz, 2 * S5_COLS), F32)
    out_specs = [st_spec, st_spec]
    out_shape = [st_shape, st_shape]
    if need_out:
        y_shape = jax.ShapeDtypeStruct(u2.shape, F32)
        out_specs = [pl.BlockSpec((bsz, blk), lambda i: (0, i)),
                     pl.BlockSpec((bsz, blk), lambda i: (0, nc - 1 - i))] + out_specs
        out_shape = [y_shape, y_shape] + out_shape
    return pl.pallas_call(
        functools.partial(_s5_kernel, tc=tc, bsz=bsz, need_out=need_out),
        grid=(nc,),
        in_specs=[pl.BlockSpec((bsz, blk), lambda i: (0, i)),
                  pl.BlockSpec((bsz, blk), lambda i: (0, nc - 1 - i)),
                  pl.BlockSpec(wbf.shape, const3), pl.BlockSpec(wbb.shape, const3),
                  pl.BlockSpec(lamf.shape, const2), pl.BlockSpec(lamb.shape, const2),
                  pl.BlockSpec(cf.shape, const4), pl.BlockSpec(cb.shape, const4),
                  st_spec, st_spec],
        out_specs=out_specs,
        out_shape=out_shape,
        scratch_shapes=[pltpu.VMEM((tc * bsz, 2 * S5_COLS), F32),
                        pltpu.VMEM((tc * bsz, 2 * S5_COLS), F32),
                        pltpu.VMEM((bsz, 2 * S5_COLS), F32),
                        pltpu.VMEM((bsz, 2 * S5_COLS), F32)],
        compiler_params=_params(("arbitrary",)),
        name="s5_scan",
    )(u2, u2, wbf, wbb, lamf, lamb, cf, cb, h0f, h0b)


def _dn_kernel(kvq_ref, cw_ref, ab_ref, abt_ref, acoef_ref, dtb_ref, acoeft_ref, dtbt_ref, s0_ref,
               *rest, blk_len, row_len, rev, need_out):
    if need_out:
        o_ref, sfin_ref, s_scr, act_scr = rest
    else:
        sfin_ref, s_scr, act_scr = rest
        o_ref = None
    ncols = act_scr.shape[1]
    cs = DN_CHUNK
    nch = blk_len // cs
    goff = 16 if rev else 0
    boff = goff + 8

    @pl.when(pl.program_id(1) == 0)
    def _():
        s_scr[...] = s0_ref[...]

    tpos = lax.broadcasted_iota(jnp.int32, (blk_len, LANES), 0) % row_len
    valid = [(tpos + (j - 2) >= 0) & (tpos + (j - 2) < row_len) for j in range(DN_CONV)]
    for cbk in range(ncols // LANES):
        cols = slice(cbk * LANES, (cbk + 1) * LANES)
        xc = kvq_ref[:, cols]
        acc = xc * cw_ref[2:3, cols]
        for j in (0, 1, 3, 4):
            sh = (2 - j) % blk_len
            acc = acc + jnp.where(valid[j], pltpu.roll(xc, sh, axis=0), 0.0) * cw_ref[j:j + 1, cols]
        act_scr[:, cols] = _silu(acc)

    ab = ab_ref[...]
    g_cols = acoef_ref[...] * _softplus(ab + dtb_ref[...])
    beta_cols = _sigmoid(ab)
    g_rows = acoeft_ref[...] * _softplus(abt_ref[...] + dtbt_ref[...])

    ri = lax.broadcasted_iota(jnp.int32, (cs, cs), 0)
    ci = lax.broadcasted_iota(jnp.int32, (cs, cs), 1)
    incl = (ri <= ci) if rev else (ri >= ci)
    strict = incl & (ri != ci)
    tri = incl.astype(F32)
    eye = (ri == ci).astype(F32)
    last = 0 if rev else cs - 1
    scale = DN_HEAD_DIM ** -0.5

    order = list(range(nch - 1, -1, -1)) if rev else list(range(nch))
    units = [(c, h) for c in order for h in range(DN_HEADS)]
    nu = len(units)
    gc_cols = {}
    gc_rows = {}
    for c in order:
        rows = slice(c * cs, (c + 1) * cs)
        gc_cols[c] = jnp.dot(tri, g_cols[rows, :], precision=lax.Precision.HIGHEST, preferred_element_type=F32)
        gc_rows[c] = lax.dot_general(g_rows[:, rows], tri, (((1,), (1,)), ((), ())),
                                     precision=lax.Precision.HIGHEST, preferred_element_type=F32)

    def head_block(base, c, h):
        return act_scr[c * cs:(c + 1) * cs, base + h * DN_HEAD_DIM: base + (h + 1) * DN_HEAD_DIM]

    def l2n(t):
        return t * lax.rsqrt(jnp.sum(t * t, axis=-1, keepdims=True) + 1e-6)

    gch = [gc_cols[c][:, goff + h: goff + h + 1] for c, h in units]
    gcr = [gc_rows[c][goff + h: goff + h + 1, :] for c, h in units]
    dec = [jnp.where(incl, jnp.exp(jnp.minimum(gch[u] - gcr[u], 0.0)), 0.0) for u in range(nu)]
    bh = [beta_cols[c * cs:(c + 1) * cs, boff + h: boff + h + 1] for c, h in units]
    k = [l2n(head_block(0, c, h)) for c, h in units]
    kbf = [t.astype(BF16) for t in k]
    kb = [k[u] * bh[u] for u in range(nu)]
    kkt = [_dot_nt(kb[u].astype(BF16), kbf[u]) for u in range(nu)]
    a = [jnp.where(strict, kkt[u] * dec[u], 0.0) for u in range(nu)]
    abf = [t.astype(BF16) for t in a]
    m = [_dot(t, t) for t in abf]
    p = [eye - t for t in a]
    for lvl in range(5):
        mb = [t.astype(BF16) for t in m]
        if lvl < 4:
            r = [_dot(jnp.concatenate([m[u], p[u]], axis=0).astype(BF16), mb[u]) for u in range(nu)]
            m = [t[:cs] for t in r]
            p = [p[u] + r[u][cs:] for u in range(nu)]
        else:
            r = [_dot(p[u].astype(BF16), mb[u]) for u in range(nu)]
            p = [p[u] + r[u] for u in range(nu)]
    eg = [jnp.exp(t) for t in gch]
    rhs = [jnp.concatenate([head_block(DN_WIDTH, c, h) * bh[u], kb[u] * eg[u]], axis=-1).astype(BF16)
           for u, (c, h) in enumerate(units)]
    uw = [_dot(p[u].astype(BF16), rhs[u]) for u in range(nu)]
    gl = [t[last:last + 1, :] for t in gch]
    kd = [(k[u] * jnp.exp(gl[u] - gch[u])).astype(BF16) for u in range(nu)]
    if need_out:
        q = [l2n(head_block(2 * DN_WIDTH, c, h)) * scale for c, h in units]
        qkt = [_dot_nt(q[u].astype(BF16), kbf[u]) for u in range(nu)]
        at = [jnp.where(incl, qkt[u] * dec[u], 0.0).astype(BF16) for u in range(nu)]
        qd = [(q[u] * eg[u]).astype(BF16) for u in range(nu)]

    for ci, c in enumerate(order):
        us = range(ci * DN_HEADS, (ci + 1) * DN_HEADS)
        s_old = [s_scr[h] for h in range(DN_HEADS)]
        s_bf = [t.astype(BF16) for t in s_old]
        ws = [_dot(uw[u][:, DN_HEAD_DIM:].astype(BF16), s_bf[h]) for h, u in enumerate(us)]
        if need_out:
            qs = [_dot(qd[u], s_bf[h]) for h, u in enumerate(us)]
        vn = [(uw[u][:, :DN_HEAD_DIM] - ws[h]).astype(BF16) for h, u in enumerate(us)]
        if need_out:
            av = [_dot(at[u], vn[h]) for h, u in enumerate(us)]
        kv = [_dot_tn(kd[u], vn[h]) for h, u in enumerate(us)]
        for h, u in enumerate(us):
            if need_out:
                o_ref[c * cs:(c + 1) * cs, h * DN_HEAD_DIM:(h + 1) * DN_HEAD_DIM] = qs[h] + av[h]
            s_scr[h] = s_old[h] * jnp.exp(gl[u]) + kv[h]
    sfin_ref[...] = s_scr[...]


def _deltanet(kvq, ncols, conv_w, ab, abt, acoef, dtb, acoeft, dtbt, s0, blk_len, row_len, rev, need_out):
    bsz, t_len, _ = kvq.shape
    nb = t_len // blk_len
    bi = (lambda b, i: (b, nb - 1 - i, 0)) if rev else (lambda b, i: (b, i, 0))
    bit = (lambda b, i: (b, 0, nb - 1 - i)) if rev else (lambda b, i: (b, 0, i))
    c2 = lambda b, i: (0, 0)
    st_spec = pl.BlockSpec((None, DN_HEADS, DN_HEAD_DIM, DN_HEAD_DIM), lambda b, i: (b, 0, 0, 0))
    st_shape = jax.ShapeDtypeStruct((bsz, DN_HEADS, DN_HEAD_DIM, DN_HEAD_DIM), F32)
    out_specs = [st_spec]
    out_shape = [st_shape]
    if need_out:
        out_specs = [pl.BlockSpec((None, blk_len, DN_WIDTH), bi)] + out_specs
        out_shape = [jax.ShapeDtypeStruct((bsz, t_len, DN_WIDTH), F32)] + out_shape
    return pl.pallas_call(
        functools.partial(_dn_kernel, blk_len=blk_len, row_len=row_len, rev=rev, need_out=need_out),
        grid=(bsz, nb),
        in_specs=[pl.BlockSpec((None, blk_len, ncols), bi),
                  pl.BlockSpec((DN_CONV, ncols), c2),
                  pl.BlockSpec((None, blk_len, LANES), bi),
                  pl.BlockSpec((None, 32, blk_len), bit),
                  pl.BlockSpec((1, LANES), c2), pl.BlockSpec((1, LANES), c2),
                  pl.BlockSpec((32, 1), c2), pl.BlockSpec((32, 1), c2),
                  st_spec],
        out_specs=out_specs,
        out_shape=out_shape,
        scratch_shapes=[pltpu.VMEM((DN_HEADS, DN_HEAD_DIM, DN_HEAD_DIM), F32),
                        pltpu.VMEM((blk_len, ncols), F32)],
        compiler_params=_params(("arbitrary", "arbitrary")),
        name="deltanet_bwd" if rev else "deltanet_fwd",
    )(kvq, conv_w, ab, abt, acoef, dtb, acoeft, dtbt, s0)


def _post_kernel(x_ref, mod_ref, yf_ref, yb_ref, u_ref, of_ref, ob_ref, z_ref, g5_ref, gd_ref,
                 s5d_ref, glw_ref, glb_ref, dnw_ref, wb5_ref, wbd_ref, wo_ref, gain_ref, o_ref):
    y5 = _gelu(yf_ref[...] + yb_ref[...] + s5d_ref[...] * u_ref[...])
    y5 = y5 * _sigmoid(_dot(y5.astype(BF16), glw_ref[...]) + glb_ref[...])
    o = of_ref[...] + ob_ref[...]
    parts = []
    for h in range(DN_HEADS):
        oh = o[:, h * DN_HEAD_DIM:(h + 1) * DN_HEAD_DIM]
        parts.append(_rms(oh) * dnw_ref[...])
    y_dn = jnp.concatenate(parts, axis=-1) * _silu(z_ref[...])
    merged = (_sigmoid(g5_ref[...]) * _dot(y5.astype(BF16), wb5_ref[...])
              + _sigmoid(gd_ref[...]) * _dot(y_dn.astype(BF16), wbd_ref[...]))
    x_mix = _dot(merged.astype(BF16), wo_ref[...])
    o_ref[...] = x_ref[...] + mod_ref[2:3, :] * (_rms(x_mix) * gain_ref[...])


def _post(x, mods, yf, yb, u, o_f, o_b, main, s5d, glw, glb, dnw, wb5, wbd, wo, gain, tm):
    bsz, t_len, d = x.shape
    tok = lambda w: pl.BlockSpec((None, tm, w), lambda b, i: (b, i, 0))
    colblk = lambda k: pl.BlockSpec((None, tm, d), lambda b, i, k=k: (b, i, k))
    full = lambda a: pl.BlockSpec(a.shape, lambda b, i: (0,) * a.ndim)
    return pl.pallas_call(
        _post_kernel,
        grid=(bsz, t_len // tm),
        in_specs=[tok(d), pl.BlockSpec((None, 6, d), lambda b, i: (b, 0, 0)),
                  tok(S5_WIDTH), tok(S5_WIDTH), tok(S5_WIDTH), tok(d), tok(d),
                  colblk(3), colblk(4), colblk(5),
                  full(s5d), full(glw), full(glb), full(dnw), full(wb5), full(wbd), full(wo), full(gain)],
        out_specs=tok(d),
        out_shape=jax.ShapeDtypeStruct(x.shape, F32),
        compiler_params=_params(("arbitrary", "arbitrary")),
        name="mixer_out",
    )(x, mods, yf, yb, u, o_f, o_b, main, main, main, s5d, glw, glb, dnw, wb5, wbd, wo, gain)


_CAND_COUNTS = [PEER_TOPK // (a + 1) for a in range(PEER_TOPK)]
_N_CAND = sum(_CAND_COUNTS)
_CAND_ROWS = -(-_N_CAND // 8) * 8


def _extract_topk(work, k, out_vals_ref=None, want_rank=False):
    nrows = work.shape[0]
    iota = lax.broadcasted_iota(jnp.int32, work.shape, 0)
    rank = jnp.full(work.shape, float(k), F32) if want_rank else None
    vals = []
    for r in range(k):
        m = jnp.max(work, axis=0, keepdims=True)
        idx = jnp.min(jnp.where(work == m, iota, nrows), axis=0, keepdims=True)
        sel = iota == idx
        work = jnp.where(sel, -jnp.inf, work)
        if want_rank:
            rank = jnp.where(sel, float(r), rank)
        vals.append(m)
        if out_vals_ref is not None:
            out_vals_ref[r:r + 1, :] = m
    return vals, work, rank


def _route_kernel(x_ref, mod_ref, gain_ref, wq_ref, keys_ref,
                  ht_ref, c1_ref, e1_ref, r2_ref, p2_ref, sv_scr, cand_scr):
    h = _rms(x_ref[...]) * gain_ref[...]
    h = h * (1.0 + mod_ref[4:5, :]) + mod_ref[3:4, :]
    hb = h.astype(BF16)
    ht_ref[...] = h.T.astype(BF16)
    q = _dot(hb, wq_ref[...]).astype(BF16)
    for hd in range(PEER_HEADS):
        svals = []
        ranks = []
        scores = []
        for j in range(2):
            col = (hd * 2 + j) * PEER_KEY_DIM
            s = _dot_nt(keys_ref[hd, j], q[:, col: col + PEER_KEY_DIM])
            vals, _, rank = _extract_topk(s, PEER_TOPK, sv_scr.at[j], want_rank=True)
            scores.append(s)
            svals.append(vals)
            ranks.append(rank)
        cand_scr[...] = jnp.full(cand_scr.shape, -jnp.inf, F32)
        off = 0
        for a in range(PEER_TOPK):
            nb = _CAND_COUNTS[a]
            cand_scr[off: off + nb, :] = svals[0][a] + sv_scr[1, 0:nb, :]
            off += nb
        top, left, _ = _extract_topk(cand_scr[...], PEER_TOPK)
        m0 = top[0]
        z = jnp.exp(top[0] - m0)
        for r in range(1, PEER_TOPK):
            z = z + jnp.exp(top[r] - m0)
        taken = (left == -jnp.inf).astype(F32)
        c1 = jnp.zeros(ranks[0].shape, F32)
        off = 0
        for a in range(PEER_TOPK):
            nb = _CAND_COUNTS[a]
            cnt = jnp.sum(taken[off: off + nb, :], axis=0, keepdims=True)
            c1 = jnp.where(ranks[0] == float(a), cnt, c1)
            off += nb
        c1_ref[hd] = c1
        r2_ref[hd] = ranks[1]
        e1_ref[hd] = jnp.where(ranks[0] < float(PEER_TOPK), jnp.exp(scores[0] - svals[0][0]), 0.0)
        p2_ref[hd] = jnp.where(ranks[1] < float(PEER_TOPK), jnp.exp(scores[1] - svals[1][0]), 0.0) / z


def _route(x1, mods, gain, wq, keys, tm):
    bsz, t_len, d = x1.shape
    n_tok = bsz * t_len
    nt = t_len // tm
    tile = lambda b, i: (0, 0, b * nt + i)
    hk = pl.BlockSpec((PEER_HEADS, PEER_N_KEYS, tm), tile)
    hk_shape = jax.ShapeDtypeStruct((PEER_HEADS, PEER_N_KEYS, n_tok), F32)
    return pl.pallas_call(
        _route_kernel,
        grid=(bsz, nt),
        in_specs=[pl.BlockSpec((None, tm, d), lambda b, i: (b, i, 0)),
                  pl.BlockSpec((None, 6, d), lambda b, i: (b, 0, 0)),
                  pl.BlockSpec((1, d), lambda b, i: (0, 0)),
                  pl.BlockSpec(wq.shape, lambda b, i: (0, 0)),
                  pl.BlockSpec(keys.shape, lambda b, i: (0, 0, 0, 0))],
        out_specs=[pl.BlockSpec((d, tm), lambda b, i: (0, b * nt + i)),
                   hk, hk, hk, hk],
        out_shape=[jax.ShapeDtypeStruct((d, n_tok), BF16),
                   hk_shape, hk_shape, hk_shape, hk_shape],
        scratch_shapes=[pltpu.VMEM((2, PEER_TOPK, tm), F32),
                        pltpu.VMEM((_CAND_ROWS, tm), F32)],
        compiler_params=_params(("arbitrary", "arbitrary")),
        name="peer_route",
    )(x1, mods, gain, wq, keys)


def _peer_kernel(ht_ref, u_ref, vt_ref, c1_ref, e1_ref, r2_ref, p2_ref, x1_ref, mod_ref, gain_ref,
                 o_ref, acc, act_scr, a_scr, *, nsub):
    e = pl.program_id(2)

    @pl.when(e == 0)
    def _():
        acc[...] = jnp.zeros(acc.shape, F32)

    act_scr[...] = _dot(u_ref[...], ht_ref[...])
    for ii in range(nsub):
        rows = slice(ii * PEER_N_KEYS, (ii + 1) * PEER_N_KEYS)
        for ct in range(act_scr.shape[1] // LANES):
            cols = slice(ct * LANES, (ct + 1) * LANES)
            w = None
            for hd in range(PEER_HEADS):
                contrib = jnp.where(r2_ref[hd, :, cols] < c1_ref[hd, ii:ii + 1, cols],
                                    p2_ref[hd, :, cols] * e1_ref[hd, ii:ii + 1, cols], 0.0)
                w = contrib if w is None else w + contrib
            a_scr[rows, cols] = (_gelu(act_scr[rows, cols]) * w).astype(BF16)
    acc[...] += _dot(vt_ref[...], a_scr[...])

    @pl.when(e == pl.num_programs(2) - 1)
    def _():
        out = acc[...].T
        o_ref[...] = x1_ref[...] + mod_ref[5:6, :] * (_rms(out) * gain_ref[...])


def _peer(ht, u_tab, vt_tab, c1, e1, r2, p2, x1, mods, gain, tn, eb):
    bsz, t_len, d = x1.shape
    nt = t_len // tn
    n_exp = u_tab.shape[0]
    nsub = eb // PEER_N_KEYS
    ne = n_exp // eb
    n_tok = bsz * t_len
    c1v = c1.reshape(PEER_HEADS, ne, nsub, n_tok)
    e1v = e1.reshape(PEER_HEADS, ne, nsub, n_tok)
    sub = pl.BlockSpec((PEER_HEADS, None, nsub, tn), lambda b, i, e: (0, e, 0, b * nt + i))
    hk = pl.BlockSpec((PEER_HEADS, PEER_N_KEYS, tn), lambda b, i, e: (0, 0, b * nt + i))
    return pl.pallas_call(
        functools.partial(_peer_kernel, nsub=nsub),
        grid=(bsz, nt, ne),
        in_specs=[pl.BlockSpec((d, tn), lambda b, i, e: (0, b * nt + i)),
                  pl.BlockSpec((eb, d), lambda b, i, e: (e, 0)),
                  pl.BlockSpec((d, eb), lambda b, i, e: (0, e)),
                  sub, sub, hk, hk,
                  pl.BlockSpec((None, tn, d), lambda b, i, e: (b, i, 0)),
                  pl.BlockSpec((None, 6, d), lambda b, i, e: (b, 0, 0)),
                  pl.BlockSpec((1, d), lambda b, i, e: (0, 0))],
        out_specs=pl.BlockSpec((None, tn, d), lambda b, i, e: (b, i, 0)),
        out_shape=jax.ShapeDtypeStruct(x1.shape, F32),
        scratch_shapes=[pltpu.VMEM((d, tn), F32), pltpu.VMEM((eb, tn), F32), pltpu.VMEM((eb, tn), BF16)],
        compiler_params=_params(("arbitrary", "arbitrary", "arbitrary")),
        name="peer_experts",
    )(ht, u_tab, vt_tab, c1v, e1v, r2, p2, x1, mods, gain)


def _s5_coeffs(lam_re, lam_im, log_step, b_re, b_im, c_re, c_im):
    step = jnp.exp(log_step)[:, None]
    mag = jnp.exp(lam_re * step)
    ab_re, ab_im = mag * jnp.cos(lam_im * step), mag * jnp.sin(lam_im * step)
    den = lam_re * lam_re + lam_im * lam_im
    f_re = ((ab_re - 1.0) * lam_re + ab_im * lam_im) / den
    f_im = (ab_im * lam_re - (ab_re - 1.0) * lam_im) / den
    bb_re = f_re[..., None] * b_re - f_im[..., None] * b_im
    bb_im = f_re[..., None] * b_im + f_im[..., None] * b_re
    lam = jnp.stack([ab_re.reshape(-1), ab_im.reshape(-1)])
    eye8 = jnp.eye(8, dtype=F32)

    def in_blockdiag(bb):
        t = bb.reshape(4, 8, S5_STATE, S5_GROUP).transpose(0, 1, 3, 2)
        return jnp.einsum('jghp,gk->jghkp', t, eye8).reshape(4, LANES, 8 * S5_STATE)

    def out_blockdiag(cc):
        t = cc.reshape(4, 8, S5_GROUP, S5_STATE).transpose(0, 1, 3, 2)
        return jnp.einsum('jgph,gk->jgpkh', t, eye8).reshape(4, 8 * S5_STATE, LANES)

    wb = jnp.concatenate([in_blockdiag(bb_re), in_blockdiag(bb_im)], axis=-1).astype(BF16)
    cmat = jnp.stack([out_blockdiag(c_re), -out_blockdiag(c_im)], axis=1).astype(BF16)
    return wb, lam, cmat


def _gate_rows(a_log, dt_bias):
    neg_a = -jnp.exp(a_log)
    acoef = jnp.zeros((LANES,), F32).at[0:8].set(neg_a[0]).at[16:24].set(neg_a[1])
    dtb = jnp.zeros((LANES,), F32).at[0:8].set(dt_bias[0]).at[16:24].set(dt_bias[1])
    return acoef[None, :], dtb[None, :], acoef[:32, None], dtb[:32, None]


def kernel(x, c, ctx, c_ctx, w_mod, b_mod, norm_pre_mix, norm_post_mix, norm_pre_ffn, norm_post_ffn, w_in, s5_lam_re, s5_lam_im, s5_log_step, s5_b_re, s5_b_im, s5_c_re, s5_c_im, s5_d, s5_glu_w, s5_glu_b, dn_conv_w, dn_a_log, dn_dt_bias, dn_norm_w, w_branch_s5, w_branch_dn, w_out, peer_w_q, peer_sub_keys, peer_u, peer_v):
    bsz, t_len, d = x.shape
    ctx_len = ctx.shape[1]
    l = 0

    pad = (-(bsz + 1)) % 8
    cv = jnp.concatenate([c, c_ctx[None, :], jnp.zeros((pad, d), F32)], axis=0)
    mod_all = _adaln(cv, w_mod[l].astype(BF16), b_mod[l][None, :])
    mods = mod_all[:bsz].reshape(bsz, 6, d)
    mods_ctx = jnp.broadcast_to(mod_all[bsz].reshape(1, 6, d), (bsz, 6, d))

    w = w_in[l]
    w_main = jnp.concatenate([w[:, COL_K:COL_AB], w[:, COL_Q:]], axis=1).astype(BF16)
    w_ctx = w[:, COL_K:COL_AB].astype(BF16)
    w_u = w[:, COL_U:COL_K].astype(BF16)
    w_ab32 = w[:, COL_AB:COL_Q]
    w_ab = jnp.pad(w_ab32, ((0, 0), (0, LANES - 32))).astype(BF16)
    w_abt = w_ab32.T.astype(BF16)
    gain_pre = norm_pre_mix[l][None, :]

    main_c, u_c, ab_c, abt_c = _inproj(ctx, mods_ctx, gain_pre, w_ctx, w_u, w_ab, w_abt, tm=ctx_len, tn=2 * DN_WIDTH)
    main_x, u_x, ab_x, abt_x = _inproj(x, mods, gain_pre, w_main, w_u, w_ab, w_abt, tm=min(1024, t_len), tn=1536)

    wbf, lamf, cf = _s5_coeffs(s5_lam_re[l, 0], s5_lam_im[l, 0], s5_log_step[l, 0], s5_b_re[l, 0], s5_b_im[l, 0],
                               s5_c_re[l, 0], s5_c_im[l, 0])
    wbb, lamb, cb = _s5_coeffs(s5_lam_re[l, 1], s5_lam_im[l, 1], s5_log_step[l, 1], s5_b_re[l, 1], s5_b_im[l, 1],
                               s5_c_re[l, 1], s5_c_im[l, 1])
    zeros_h = jnp.zeros((bsz, 2 * S5_COLS), F32)
    tc = 32
    hf_c, hb_c = _s5(u_c.reshape(bsz, ctx_len * S5_WIDTH), wbf, wbb, lamf, lamb, cf, cb, zeros_h, zeros_h, tc, False)
    yf, yb, _, _ = _s5(u_x.reshape(bsz, t_len * S5_WIDTH), wbf, wbb, lamf, lamb, cf, cb, hf_c, hb_c, tc, True)
    yf = yf.reshape(bsz, t_len, S5_WIDTH)
    yb = yb.reshape(bsz, t_len, S5_WIDTH)

    acoef, dtb, acoeft, dtbt = _gate_rows(dn_a_log[l], dn_dt_bias[l])
    conv_w = dn_conv_w[l]
    zeros_s = jnp.zeros((bsz, DN_HEADS, DN_HEAD_DIM, DN_HEAD_DIM), F32)
    dn_args = (acoef, dtb, acoeft, dtbt)
    (sf_c,) = _deltanet(main_c, 2 * DN_WIDTH, conv_w[:, :2 * DN_WIDTH], ab_c, abt_c, *dn_args, zeros_s,
                        ctx_len, ctx_len, False, False)
    (sb_c,) = _deltanet(main_c, 2 * DN_WIDTH, conv_w[:, :2 * DN_WIDTH], ab_c, abt_c, *dn_args, zeros_s,
                        ctx_len, ctx_len, True, False)
    blk_len = min(256, t_len)
    o_f, _ = _deltanet(main_x, 3 * DN_WIDTH, conv_w, ab_x, abt_x, *dn_args, sf_c, blk_len, GRID_W, False, True)
    o_b, _ = _deltanet(main_x, 3 * DN_WIDTH, conv_w, ab_x, abt_x, *dn_args, sb_c, blk_len, GRID_W, True, True)

    x1 = _post(x, mods, yf, yb, u_x, o_f, o_b, main_x,
               s5_d[l][None, :], s5_glu_w[l].astype(BF16), s5_glu_b[l][None, :], dn_norm_w[l][None, :],
               w_branch_s5[l].astype(BF16), w_branch_dn[l].astype(BF16), w_out[l].astype(BF16),
               norm_post_mix[l][None, :], tm=min(512, t_len))

    ht, c1, e1, r2, p2 = _route(x1, mods, norm_pre_ffn[l][None, :], peer_w_q[l].astype(BF16),
                                peer_sub_keys[l].astype(BF16), tm=min(256, t_len))
    out = _peer(ht, peer_u[l].astype(BF16), peer_v[l].T.astype(BF16), c1, e1, r2, p2, x1, mods,
                norm_post_ffn[l][None, :], tn=min(512, t_len), eb=512)
    return out
```

```python
import functools
import math

import jax
import jax.numpy as jnp
from jax import lax
from jax.experimental import pallas as pl
from jax.experimental.pallas import tpu as pltpu

F32 = jnp.float32
BF16 = jnp.bfloat16

D_MODEL = 1024
NORM_EPS = 1e-6
GRID_W = 64
S5_WIDTH = 512
S5_GROUP = 16
S5_GROUPS = 32
S5_STATE = 64
S5_COLS = S5_GROUPS * S5_STATE
DN_HEADS = 8
DN_HEAD_DIM = 128
DN_WIDTH = 1024
DN_CONV = 5
DN_CHUNK = 64
PEER_HEADS = 8
PEER_N_KEYS = 128
PEER_KEY_DIM = 128
PEER_TOPK = 16
LANES = 128

COL_U = 0
COL_K = 512
COL_V = 1536
COL_AB = 2560
COL_Q = 2592
COL_Z = 3616
COL_G5 = 4640
COL_GD = 5664

VMEM_LIMIT = 56 * 1024 * 1024


def _dot(a, b):
    return jnp.dot(a, b, preferred_element_type=F32)


def _dot_nt(a, b):
    return lax.dot_general(a, b, (((1,), (1,)), ((), ())), preferred_element_type=F32)


def _dot_tn(a, b):
    return lax.dot_general(a, b, (((0,), (0,)), ((), ())), preferred_element_type=F32)


def _sigmoid(x):
    return jax.nn.sigmoid(x)


def _silu(x):
    return x * _sigmoid(x)


def _gelu(x):
    c = math.sqrt(2.0 / math.pi)
    return x * (0.5 * (1.0 + jnp.tanh(c * (x + 0.044715 * (x * x * x)))))


def _softplus(x):
    return jnp.maximum(x, 0.0) + jnp.log1p(jnp.exp(-jnp.abs(x)))


def _rms(x):
    return x * lax.rsqrt(jnp.mean(x * x, axis=-1, keepdims=True) + NORM_EPS)


def _params(sem):
    return pltpu.CompilerParams(dimension_semantics=sem, vmem_limit_bytes=VMEM_LIMIT)


def _adaln_kernel(c_ref, w_ref, b_ref, o_ref):
    c = c_ref[...]
    o_ref[...] = _dot(_silu(c).astype(BF16), w_ref[...]) + b_ref[...]


def _adaln(cv, w_mod, b_mod):
    rows = cv.shape[0]
    ncol = w_mod.shape[1] // D_MODEL
    return pl.pallas_call(
        _adaln_kernel,
        grid=(ncol,),
        in_specs=[pl.BlockSpec((rows, D_MODEL), lambda j: (0, 0)),
                  pl.BlockSpec((D_MODEL, D_MODEL), lambda j: (0, j)),
                  pl.BlockSpec((1, D_MODEL), lambda j: (0, j))],
        out_specs=pl.BlockSpec((rows, D_MODEL), lambda j: (0, j)),
        out_shape=jax.ShapeDtypeStruct((rows, w_mod.shape[1]), F32),
        compiler_params=_params(("arbitrary",)),
        name="adaln",
    )(cv, w_mod, b_mod)


def _inproj_kernel(x_ref, mod_ref, gain_ref, wm_ref, wu_ref, wab_ref, wabt_ref,
                   main_ref, u_ref, ab_ref, abt_ref, hx_scr):
    j = pl.program_id(2)

    @pl.when(j == 0)
    def _():
        h = _rms(x_ref[...]) * gain_ref[...]
        h = h * (1.0 + mod_ref[1:2, :]) + mod_ref[0:1, :]
        hb = h.astype(BF16)
        hx_scr[...] = hb
        u_ref[...] = _dot(hb, wu_ref[...])
        ab_ref[...] = _dot(hb, wab_ref[...])
        abt_ref[...] = _dot_nt(wabt_ref[...], hb)

    main_ref[...] = _dot(hx_scr[...], wm_ref[...])


def _inproj(x, mods, gain, w_main, w_u, w_ab, w_abt, tm, tn):
    bsz, t_len, d = x.shape
    cm = w_main.shape[1]
    grid = (bsz, t_len // tm, cm // tn)
    return pl.pallas_call(
        _inproj_kernel,
        grid=grid,
        in_specs=[pl.BlockSpec((None, tm, d), lambda b, i, j: (b, i, 0)),
                  pl.BlockSpec((None, 6, d), lambda b, i, j: (b, 0, 0)),
                  pl.BlockSpec((1, d), lambda b, i, j: (0, 0)),
                  pl.BlockSpec((d, tn), lambda b, i, j: (0, j)),
                  pl.BlockSpec((d, S5_WIDTH), lambda b, i, j: (0, 0)),
                  pl.BlockSpec((d, LANES), lambda b, i, j: (0, 0)),
                  pl.BlockSpec((32, d), lambda b, i, j: (0, 0))],
        out_specs=[pl.BlockSpec((None, tm, tn), lambda b, i, j: (b, i, j)),
                   pl.BlockSpec((None, tm, S5_WIDTH), lambda b, i, j: (b, i, 0)),
                   pl.BlockSpec((None, tm, LANES), lambda b, i, j: (b, i, 0)),
                   pl.BlockSpec((None, 32, tm), lambda b, i, j: (b, 0, i))],
        out_shape=[jax.ShapeDtypeStruct((bsz, t_len, cm), F32),
                   jax.ShapeDtypeStruct((bsz, t_len, S5_WIDTH), F32),
                   jax.ShapeDtypeStruct((bsz, t_len, LANES), F32),
                   jax.ShapeDtypeStruct((bsz, 32, t_len), F32)],
        scratch_shapes=[pltpu.VMEM((tm, d), BF16)],
        compiler_params=_params(("arbitrary", "arbitrary", "arbitrary")),
        name="inproj",
    )(x, mods, gain, w_main, w_u, w_ab, w_abt)


def _s5_kernel(uf_ref, ub_ref, wbf_ref, wbb_ref, lamf_ref, lamb_ref, cf_ref, cb_ref,
               h0f_ref, h0b_ref, *rest, tc, bsz, need_out):
    if need_out:
        yf_ref, yb_ref, hff_ref, hfb_ref, xsf, xsb, hcf, hcb = rest
    else:
        hff_ref, hfb_ref, xsf, xsb, hcf, hcb = rest
        yf_ref = yb_ref = None
    i = pl.program_id(0)
    half = S5_COLS
    cw = 512
    nblk = half // cw

    @pl.when(i == 0)
    def _():
        hcf[...] = h0f_ref[...]
        hcb[...] = h0b_ref[...]

    dirs = ((uf_ref, wbf_ref, lamf_ref, cf_ref, yf_ref, xsf, hcf, False),
            (ub_ref, wbb_ref, lamb_ref, cb_ref, yb_ref, xsb, hcb, True))
    for u_ref, wb_ref, lam_ref, c_ref, y_ref, xs, hc, rev in dirs:
        for j in range(4):
            lhs = jnp.concatenate(
                [u_ref[:, t * S5_WIDTH + j * LANES: t * S5_WIDTH + (j + 1) * LANES] for t in range(tc)],
                axis=0).astype(BF16)
            r = _dot(lhs, wb_ref[j])
            xs[:, j * cw:(j + 1) * cw] = r[:, :cw]
            xs[:, half + j * cw: half + (j + 1) * cw] = r[:, cw:]
        for cb in range(nblk):
            lo, hi = cb * cw, (cb + 1) * cw
            ar = jnp.broadcast_to(lam_ref[0:1, lo:hi], (bsz, cw))
            ai = jnp.broadcast_to(lam_ref[1:2, lo:hi], (bsz, cw))

            def body(s, carry, lo=lo, hi=hi, ar=ar, ai=ai, xs=xs, rev=rev):
                hr, hi_ = carry
                t = (tc - 1 - s) if rev else s
                row = pl.multiple_of(t * bsz, bsz)
                xr = xs[pl.ds(row, bsz), lo:hi]
                xi = xs[pl.ds(row, bsz), half + lo: half + hi]
                nr = ar * hr - ai * hi_ + xr
                ni = ar * hi_ + ai * hr + xi
                xs[pl.ds(row, bsz), lo:hi] = nr
                xs[pl.ds(row, bsz), half + lo: half + hi] = ni
                return nr, ni

            hr, hi_ = lax.fori_loop(0, tc, body, (hc[:, lo:hi], hc[:, half + lo: half + hi]))
            hc[:, lo:hi] = hr
            hc[:, half + lo: half + hi] = hi_
        if need_out:
            for j in range(4):
                hre = xs[:, j * cw:(j + 1) * cw].astype(BF16)
                him = xs[:, half + j * cw: half + (j + 1) * cw].astype(BF16)
                y = _dot(hre, c_ref[j, 0]) + _dot(him, c_ref[j, 1])
                for t in range(tc):
                    y_ref[:, t * S5_WIDTH + j * LANES: t * S5_WIDTH + (j + 1) * LANES] = y[t * bsz:(t + 1) * bsz]
    hff_ref[...] = hcf[...]
    hfb_ref[...] = hcb[...]


def _s5(u2, wbf, wbb, lamf, lamb, cf, cb, h0f, h0b, tc, need_out):
    bsz = u2.shape[0]
    t_len = u2.shape[1] // S5_WIDTH
    nc = t_len // tc
    blk = tc * S5_WIDTH
    const2 = lambda i: (0, 0)
    const3 = lambda i: (0, 0, 0)
    const4 = lambda i: (0, 0, 0, 0)
    st_spec = pl.BlockSpec((bsz, 2 * S5_COLS), const2)
    st_shape = jax.ShapeDtypeStruct((bsz, 2 * S5_COLS), F32)
    out_specs = [st_spec, st_spec]
    out_shape = [st_shape, st_shape]
    if need_out:
        y_shape = jax.ShapeDtypeStruct(u2.shape, F32)
        out_specs = [pl.BlockSpec((bsz, blk), lambda i: (0, i)),
                     pl.BlockSpec((bsz, blk), lambda i: (0, nc - 1 - i))] + out_specs
        out_shape = [y_shape, y_shape] + out_shape
    return pl.pallas_call(
        functools.partial(_s5_kernel, tc=tc, bsz=bsz, need_out=need_out),
        grid=(nc,),
        in_specs=[pl.BlockSpec((bsz, blk), lambda i: (0, i)),
                  pl.BlockSpec((bsz, blk), lambda i: (0, nc - 1 - i)),
                  pl.BlockSpec(wbf.shape, const3), pl.BlockSpec(wbb.shape, const3),
                  pl.BlockSpec(lamf.shape, const2), pl.BlockSpec(lamb.shape, const2),
                  pl.BlockSpec(cf.shape, const4), pl.BlockSpec(cb.shape, const4),
                  st_spec, st_spec],
        out_specs=out_specs,
        out_shape=out_shape,
        scratch_shapes=[pltpu.VMEM((tc * bsz, 2 * S5_COLS), F32),
                        pltpu.VMEM((tc * bsz, 2 * S5_COLS), F32),
                        pltpu.VMEM((bsz, 2 * S5_COLS), F32),
                        pltpu.VMEM((bsz, 2 * S5_COLS), F32)],
        compiler_params=_params(("arbitrary",)),
        name="s5_scan",
    )(u2, u2, wbf, wbb, lamf, lamb, cf, cb, h0f, h0b)


def _dn_kernel(kvq_ref, cw_ref, ab_ref, abt_ref, acoef_ref, dtb_ref, acoeft_ref, dtbt_ref, s0_ref,
               *rest, blk_len, row_len, rev, need_out):
    if need_out:
        o_ref, sfin_ref, s_scr, act_scr = rest
    else:
        sfin_ref, s_scr, act_scr = rest
        o_ref = None
    ncols = act_scr.shape[1]
    cs = DN_CHUNK
    nch = blk_len // cs
    goff = 16 if rev else 0
    boff = goff + 8

    @pl.when(pl.program_id(1) == 0)
    def _():
        s_scr[...] = s0_ref[...]

    tpos = lax.broadcasted_iota(jnp.int32, (blk_len, LANES), 0) % row_len
    valid = [(tpos + (j - 2) >= 0) & (tpos + (j - 2) < row_len) for j in range(DN_CONV)]
    for cbk in range(ncols // LANES):
        cols = slice(cbk * LANES, (cbk + 1) * LANES)
        xc = kvq_ref[:, cols]
        acc = xc * cw_ref[2:3, cols]
        for j in (0, 1, 3, 4):
            sh = (2 - j) % blk_len
            acc = acc + jnp.where(valid[j], pltpu.roll(xc, sh, axis=0), 0.0) * cw_ref[j:j + 1, cols]
        act_scr[:, cols] = _silu(acc)

    ab = ab_ref[...]
    g_cols = acoef_ref[...] * _softplus(ab + dtb_ref[...])
    beta_cols = _sigmoid(ab)
    g_rows = acoeft_ref[...] * _softplus(abt_ref[...] + dtbt_ref[...])

    ri = lax.broadcasted_iota(jnp.int32, (cs, cs), 0)
    ci = lax.broadcasted_iota(jnp.int32, (cs, cs), 1)
    incl = (ri <= ci) if rev else (ri >= ci)
    strict = incl & (ri != ci)
    tri = incl.astype(F32)
    eye = (ri == ci).astype(F32)
    last = 0 if rev else cs - 1
    scale = DN_HEAD_DIM ** -0.5

    order = list(range(nch - 1, -1, -1)) if rev else list(range(nch))
    units = [(c, h) for c in order for h in range(DN_HEADS)]
    nu = len(units)
    gc_cols = {}
    gc_rows = {}
    for c in order:
        rows = slice(c * cs, (c + 1) * cs)
        gc_cols[c] = jnp.dot(tri, g_cols[rows, :], precision=lax.Precision.HIGHEST, preferred_element_type=F32)
        gc_rows[c] = lax.dot_general(g_rows[:, rows], tri, (((1,), (1,)), ((), ())),
                                     precision=lax.Precision.HIGHEST, preferred_element_type=F32)

    def head_block(base, c, h):
        return act_scr[c * cs:(c + 1) * cs, base + h * DN_HEAD_DIM: base + (h + 1) * DN_HEAD_DIM]

    def l2n(t):
        return t * lax.rsqrt(jnp.sum(t * t, axis=-1, keepdims=True) + 1e-6)

    gch = [gc_cols[c][:, goff + h: goff + h + 1] for c, h in units]
    gcr = [gc_rows[c][goff + h: goff + h + 1, :] for c, h in units]
    dec = [jnp.where(incl, jnp.exp(jnp.minimum(gch[u] - gcr[u], 0.0)), 0.0) for u in range(nu)]
    bh = [beta_cols[c * cs:(c + 1) * cs, boff + h: boff + h + 1] for c, h in units]
    k = [l2n(head_block(0, c, h)) for c, h in units]
    kbf = [t.astype(BF16) for t in k]
    kb = [k[u] * bh[u] for u in range(nu)]
    kkt = [_dot_nt(kb[u].astype(BF16), kbf[u]) for u in range(nu)]
    a = [jnp.where(strict, kkt[u] * dec[u], 0.0) for u in range(nu)]
    abf = [t.astype(BF16) for t in a]
    m = [_dot(t, t) for t in abf]
    p = [eye - t for t in a]
    for lvl in range(5):
        mb = [t.astype(BF16) for t in m]
        if lvl < 4:
            r = [_dot(jnp.concatenate([m[u], p[u]], axis=0).astype(BF16), mb[u]) for u in range(nu)]
            m = [t[:cs] for t in r]
            p = [p[u] + r[u][cs:] for u in range(nu)]
        else:
            r = [_dot(p[u].astype(BF16), mb[u]) for u in range(nu)]
            p = [p[u] + r[u] for u in range(nu)]
    eg = [jnp.exp(t) for t in gch]
    rhs = [jnp.concatenate([head_block(DN_WIDTH, c, h) * bh[u], kb[u] * eg[u]], axis=-1).astype(BF16)
           for u, (c, h) in enumerate(units)]
    uw = [_dot(p[u].astype(BF16), rhs[u]) for u in range(nu)]
    gl = [t[last:last + 1, :] for t in gch]
    kd = [(k[u] * jnp.exp(gl[u] - gch[u])).astype(BF16) for u in range(nu)]
    if need_out:
        q = [l2n(head_block(2 * DN_WIDTH, c, h)) * scale for c, h in units]
        qkt = [_dot_nt(q[u].astype(BF16), kbf[u]) for u in range(nu)]
        at = [jnp.where(incl, qkt[u] * dec[u], 0.0).astype(BF16) for u in range(nu)]
        qd = [(q[u] * eg[u]).astype(BF16) for u in range(nu)]

    for ci, c in enumerate(order):
        us = range(ci * DN_HEADS, (ci + 1) * DN_HEADS)
        s_old = [s_scr[h] for h in range(DN_HEADS)]
        s_bf = [t.astype(BF16) for t in s_old]
        ws = [_dot(uw[u][:, DN_HEAD_DIM:].astype(BF16), s_bf[h]) for h, u in enumerate(us)]
        if need_out:
            qs = [_dot(qd[u], s_bf[h]) for h, u in enumerate(us)]
        vn = [(uw[u][:, :DN_HEAD_DIM] - ws[h]).astype(BF16) for h, u in enumerate(us)]
        if need_out:
            av = [_dot(at[u], vn[h]) for h, u in enumerate(us)]
        kv = [_dot_tn(kd[u], vn[h]) for h, u in enumerate(us)]
        for h, u in enumerate(us):
            if need_out:
                o_ref[c * cs:(c + 1) * cs, h * DN_HEAD_DIM:(h + 1) * DN_HEAD_DIM] = qs[h] + av[h]
            s_scr[h] = s_old[h] * jnp.exp(gl[u]) + kv[h]
    sfin_ref[...] = s_scr[...]


def _deltanet(kvq, ncols, conv_w, ab, abt, acoef, dtb, acoeft, dtbt, s0, blk_len, row_len, rev, need_out):
    bsz, t_len, _ = kvq.shape
    nb = t_len // blk_len
    bi = (lambda b, i: (b, nb - 1 - i, 0)) if rev else (lambda b, i: (b, i, 0))
    bit = (lambda b, i: (b, 0, nb - 1 - i)) if rev else (lambda b, i: (b, 0, i))
    c2 = lambda b, i: (0, 0)
    st_spec = pl.BlockSpec((None, DN_HEADS, DN_HEAD_DIM, DN_HEAD_DIM), lambda b, i: (b, 0, 0, 0))
    st_shape = jax.ShapeDtypeStruct((bsz, DN_HEADS, DN_HEAD_DIM, DN_HEAD_DIM), F32)
    out_specs = [st_spec]
    out_shape = [st_shape]
    if need_out:
        out_specs = [pl.BlockSpec((None, blk_len, DN_WIDTH), bi)] + out_specs
        out_shape = [jax.ShapeDtypeStruct((bsz, t_len, DN_WIDTH), F32)] + out_shape
    return pl.pallas_call(
        functools.partial(_dn_kernel, blk_len=blk_len, row_len=row_len, rev=rev, need_out=need_out),
        grid=(bsz, nb),
        in_specs=[pl.BlockSpec((None, blk_len, ncols), bi),
                  pl.BlockSpec((DN_CONV, ncols), c2),
                  pl.BlockSpec((None, blk_len, LANES), bi),
                  pl.BlockSpec((None, 32, blk_len), bit),
                  pl.BlockSpec((1, LANES), c2), pl.BlockSpec((1, LANES), c2),
                  pl.BlockSpec((32, 1), c2), pl.BlockSpec((32, 1), c2),
                  st_spec],
        out_specs=out_specs,
        out_shape=out_shape,
        scratch_shapes=[pltpu.VMEM((DN_HEADS, DN_HEAD_DIM, DN_HEAD_DIM), F32),
                        pltpu.VMEM((blk_len, ncols), F32)],
        compiler_params=_params(("arbitrary", "arbitrary")),
        name="deltanet_bwd" if rev else "deltanet_fwd",
    )(kvq, conv_w, ab, abt, acoef, dtb, acoeft, dtbt, s0)


def _post_kernel(x_ref, mod_ref, yf_ref, yb_ref, u_ref, of_ref, ob_ref, z_ref, g5_ref, gd_ref,
                 s5d_ref, glw_ref, glb_ref, dnw_ref, wb5_ref, wbd_ref, wo_ref, gain_ref, o_ref):
    y5 = _gelu(yf_ref[...] + yb_ref[...] + s5d_ref[...] * u_ref[...])
    y5 = y5 * _sigmoid(_dot(y5.astype(BF16), glw_ref[...]) + glb_ref[...])
    o = of_ref[...] + ob_ref[...]
    parts = []
    for h in range(DN_HEADS):
        oh = o[:, h * DN_HEAD_DIM:(h + 1) * DN_HEAD_DIM]
        parts.append(_rms(oh) * dnw_ref[...])
    y_dn = jnp.concatenate(parts, axis=-1) * _silu(z_ref[...])
    merged = (_sigmoid(g5_ref[...]) * _dot(y5.astype(BF16), wb5_ref[...])
              + _sigmoid(gd_ref[...]) * _dot(y_dn.astype(BF16), wbd_ref[...]))
    x_mix = _dot(merged.astype(BF16), wo_ref[...])
    o_ref[...] = x_ref[...] + mod_ref[2:3, :] * (_rms(x_mix) * gain_ref[...])


def _post(x, mods, yf, yb, u, o_f, o_b, main, s5d, glw, glb, dnw, wb5, wbd, wo, gain, tm):
    bsz, t_len, d = x.shape
    tok = lambda w: pl.BlockSpec((None, tm, w), lambda b, i: (b, i, 0))
    colblk = lambda k: pl.BlockSpec((None, tm, d), lambda b, i, k=k: (b, i, k))
    full = lambda a: pl.BlockSpec(a.shape, lambda b, i: (0,) * a.ndim)
    return pl.pallas_call(
        _post_kernel,
        grid=(bsz, t_len // tm),
        in_specs=[tok(d), pl.BlockSpec((None, 6, d), lambda b, i: (b, 0, 0)),
                  tok(S5_WIDTH), tok(S5_WIDTH), tok(S5_WIDTH), tok(d), tok(d),
                  colblk(3), colblk(4), colblk(5),
                  full(s5d), full(glw), full(glb), full(dnw), full(wb5), full(wbd), full(wo), full(gain)],
        out_specs=tok(d),
        out_shape=jax.ShapeDtypeStruct(x.shape, F32),
        compiler_params=_params(("arbitrary", "arbitrary")),
        name="mixer_out",
    )(x, mods, yf, yb, u, o_f, o_b, main, main, main, s5d, glw, glb, dnw, wb5, wbd, wo, gain)


_CAND_COUNTS = [PEER_TOPK // (a + 1) for a in range(PEER_TOPK)]
_N_CAND = sum(_CAND_COUNTS)
_CAND_ROWS = -(-_N_CAND // 8) * 8


def _extract_topk(work, k, out_vals_ref=None, want_rank=False):
    nrows = work.shape[0]
    iota = lax.broadcasted_iota(jnp.int32, work.shape, 0)
    rank = jnp.full(work.shape, float(k), F32) if want_rank else None
    vals = []
    for r in range(k):
        m = jnp.max(work, axis=0, keepdims=True)
        idx = jnp.min(jnp.where(work == m, iota, nrows), axis=0, keepdims=True)
        sel = iota == idx
        work = jnp.where(sel, -jnp.inf, work)
        if want_rank:
            rank = jnp.where(sel, float(r), rank)
        vals.append(m)
        if out_vals_ref is not None:
            out_vals_ref[r:r + 1, :] = m
    return vals, work, rank


def _route_kernel(x_ref, mod_ref, gain_ref, wq_ref, keys_ref,
                  ht_ref, c1_ref, e1_ref, r2_ref, p2_ref, sv_scr, cand_scr):
    h = _rms(x_ref[...]) * gain_ref[...]
    h = h * (1.0 + mod_ref[4:5, :]) + mod_ref[3:4, :]
    hb = h.astype(BF16)
    ht_ref[...] = h.T.astype(BF16)
    q = _dot(hb, wq_ref[...]).astype(BF16)
    for hd in range(PEER_HEADS):
        svals = []
        ranks = []
        scores = []
        for j in range(2):
            col = (hd * 2 + j) * PEER_KEY_DIM
            s = _dot_nt(keys_ref[hd, j], q[:, col: col + PEER_KEY_DIM])
            vals, _, rank = _extract_topk(s, PEER_TOPK, sv_scr.at[j], want_rank=True)
            scores.append(s)
            svals.append(vals)
            ranks.append(rank)
        cand_scr[...] = jnp.full(cand_scr.shape, -jnp.inf, F32)
        off = 0
        for a in range(PEER_TOPK):
            nb = _CAND_COUNTS[a]
            cand_scr[off: off + nb, :] = svals[0][a] + sv_scr[1, 0:nb, :]
            off += nb
        top, left, _ = _extract_topk(cand_scr[...], PEER_TOPK)
        m0 = top[0]
        z = jnp.exp(top[0] - m0)
        for r in range(1, PEER_TOPK):
            z = z + jnp.exp(top[r] - m0)
        taken = (left == -jnp.inf).astype(F32)
        c1 = jnp.zeros(ranks[0].shape, F32)
        off = 0
        for a in range(PEER_TOPK):
            nb = _CAND_COUNTS[a]
            cnt = jnp.sum(taken[off: off + nb, :], axis=0, keepdims=True)
            c1 = jnp.where(ranks[0] == float(a), cnt, c1)
            off += nb
        c1_ref[hd] = c1
        r2_ref[hd] = ranks[1]
        e1_ref[hd] = jnp.where(ranks[0] < float(PEER_TOPK), jnp.exp(scores[0] - svals[0][0]), 0.0)
        p2_ref[hd] = jnp.where(ranks[1] < float(PEER_TOPK), jnp.exp(scores[1] - svals[1][0]), 0.0) / z


def _route(x1, mods, gain, wq, keys, tm):
    bsz, t_len, d = x1.shape
    n_tok = bsz * t_len
    nt = t_len // tm
    tile = lambda b, i: (0, 0, b * nt + i)
    hk = pl.BlockSpec((PEER_HEADS, PEER_N_KEYS, tm), tile)
    hk_shape = jax.ShapeDtypeStruct((PEER_HEADS, PEER_N_KEYS, n_tok), F32)
    return pl.pallas_call(
        _route_kernel,
        grid=(bsz, nt),
        in_specs=[pl.BlockSpec((None, tm, d), lambda b, i: (b, i, 0)),
                  pl.BlockSpec((None, 6, d), lambda b, i: (b, 0, 0)),
                  pl.BlockSpec((1, d), lambda b, i: (0, 0)),
                  pl.BlockSpec(wq.shape, lambda b, i: (0, 0)),
                  pl.BlockSpec(keys.shape, lambda b, i: (0, 0, 0, 0))],
        out_specs=[pl.BlockSpec((d, tm), lambda b, i: (0, b * nt + i)),
                   hk, hk, hk, hk],
        out_shape=[jax.ShapeDtypeStruct((d, n_tok), BF16),
                   hk_shape, hk_shape, hk_shape, hk_shape],
        scratch_shapes=[pltpu.VMEM((2, PEER_TOPK, tm), F32),
                        pltpu.VMEM((_CAND_ROWS, tm), F32)],
        compiler_params=_params(("arbitrary", "arbitrary")),
        name="peer_route",
    )(x1, mods, gain, wq, keys)


def _peer_kernel(ht_ref, u_ref, vt_ref, c1_ref, e1_ref, r2_ref, p2_ref, x1_ref, mod_ref, gain_ref,
                 o_ref, acc, act0, act1, a0, a1, *, nsub):
    e = pl.program_id(2)
    first = (pl.program_id(0) == 0) & (pl.program_id(1) == 0) & (e == 0)
    act_scr = (act0, act1)
    a_scr = (a0, a1)

    @pl.when(first)
    def _():
        for buf in act_scr:
            buf[...] = jnp.zeros(buf.shape, F32)
        for buf in a_scr:
            buf[...] = jnp.zeros(buf.shape, BF16)

    @pl.when(e == 2)
    def _():
        acc[...] = jnp.zeros(acc.shape, F32)

    half = PEER_N_KEYS // 2

    ncol = act0.shape[1] // LANES
    npiece = 2 * ncol
    eb = act0.shape[0]
    dm = acc.shape[0]

    def step(cur, prev):
        for ct in range(ncol):
            cols = slice(ct * LANES, (ct + 1) * LANES)
            for rh in range(2):
                pc = ct * 2 + rh
                urows = slice(pc * eb // npiece, (pc + 1) * eb // npiece)
                act_scr[cur][urows, :] = _dot(u_ref[urows, :], ht_ref[...])
                orows = slice(pc * dm // npiece, (pc + 1) * dm // npiece)
                acc[orows, :] += _dot(vt_ref[orows, :], a_scr[cur][...])
                krows = slice(rh * half, (rh + 1) * half)
                ws = [None] * nsub
                for hd in range(PEER_HEADS):
                    r2t = r2_ref[hd, krows, cols]
                    p2t = p2_ref[hd, krows, cols]
                    for ii in range(nsub):
                        contrib = jnp.where(r2t < c1_ref[hd, ii:ii + 1, cols],
                                            p2t * e1_ref[hd, ii:ii + 1, cols], 0.0)
                        ws[ii] = contrib if ws[ii] is None else ws[ii] + contrib
                for ii in range(nsub):
                    rows = slice(ii * PEER_N_KEYS + rh * half, ii * PEER_N_KEYS + (rh + 1) * half)
                    a_scr[prev][rows, cols] = (_gelu(act_scr[prev][rows, cols]) * ws[ii]).astype(BF16)

    @pl.when(e % 2 == 0)
    def _():
        step(0, 1)

    @pl.when(e % 2 == 1)
    def _():
        step(1, 0)

    @pl.when(e == pl.num_programs(2) - 1)
    def _():
        out = acc[...].T
        o_ref[...] = x1_ref[...] + mod_ref[5:6, :] * (_rms(out) * gain_ref[...])


def _peer(ht, u_tab, vt_tab, c1, e1, r2, p2, x1, mods, gain, tn, eb):
    bsz, t_len, d = x1.shape
    nt = t_len // tn
    n_exp = u_tab.shape[0]
    nsub = eb // PEER_N_KEYS
    ne = n_exp // eb
    n_tok = bsz * t_len
    c1v = c1.reshape(PEER_HEADS, ne, nsub, n_tok)
    e1v = e1.reshape(PEER_HEADS, ne, nsub, n_tok)
    blk = lambda e, lag: jnp.clip(e - lag, 0, ne - 1)
    sub = pl.BlockSpec((PEER_HEADS, None, nsub, tn), lambda b, i, e: (0, blk(e, 1), 0, b * nt + i))
    hk = pl.BlockSpec((PEER_HEADS, PEER_N_KEYS, tn), lambda b, i, e: (0, 0, b * nt + i))
    return pl.pallas_call(
        functools.partial(_peer_kernel, nsub=nsub),
        grid=(bsz, nt, ne + 2),
        in_specs=[pl.BlockSpec((d, tn), lambda b, i, e: (0, b * nt + i)),
                  pl.BlockSpec((eb, d), lambda b, i, e: (blk(e, 0), 0)),
                  pl.BlockSpec((d, eb), lambda b, i, e: (0, blk(e, 2))),
                  sub, sub, hk, hk,
                  pl.BlockSpec((None, tn, d), lambda b, i, e: (b, i, 0)),
                  pl.BlockSpec((None, 6, d), lambda b, i, e: (b, 0, 0)),
                  pl.BlockSpec((1, d), lambda b, i, e: (0, 0))],
        out_specs=pl.BlockSpec((None, tn, d), lambda b, i, e: (b, i, 0)),
        out_shape=jax.ShapeDtypeStruct(x1.shape, F32),
        scratch_shapes=[pltpu.VMEM((d, tn), F32), pltpu.VMEM((eb, tn), F32), pltpu.VMEM((eb, tn), F32),
                        pltpu.VMEM((eb, tn), BF16), pltpu.VMEM((eb, tn), BF16)],
        compiler_params=_params(("arbitrary", "arbitrary", "arbitrary")),
        name="peer_experts",
    )(ht, u_tab, vt_tab, c1v, e1v, r2, p2, x1, mods, gain)


def _s5_coeffs(lam_re, lam_im, log_step, b_re, b_im, c_re, c_im):
    step = jnp.exp(log_step)[:, None]
    mag = jnp.exp(lam_re * step)
    ab_re, ab_im = mag * jnp.cos(lam_im * step), mag * jnp.sin(lam_im * step)
    den = lam_re * lam_re + lam_im * lam_im
    f_re = ((ab_re - 1.0) * lam_re + ab_im * lam_im) / den
    f_im = (ab_im * lam_re - (ab_re - 1.0) * lam_im) / den
    bb_re = f_re[..., None] * b_re - f_im[..., None] * b_im
    bb_im = f_re[..., None] * b_im + f_im[..., None] * b_re
    lam = jnp.stack([ab_re.reshape(-1), ab_im.reshape(-1)])
    eye8 = jnp.eye(8, dtype=F32)

    def in_blockdiag(bb):
        t = bb.reshape(4, 8, S5_STATE, S5_GROUP).transpose(0, 1, 3, 2)
        return jnp.einsum('jghp,gk->jghkp', t, eye8).reshape(4, LANES, 8 * S5_STATE)

    def out_blockdiag(cc):
        t = cc.reshape(4, 8, S5_GROUP, S5_STATE).transpose(0, 1, 3, 2)
        return jnp.einsum('jgph,gk->jgpkh', t, eye8).reshape(4, 8 * S5_STATE, LANES)

    wb = jnp.concatenate([in_blockdiag(bb_re), in_blockdiag(bb_im)], axis=-1).astype(BF16)
    cmat = jnp.stack([out_blockdiag(c_re), -out_blockdiag(c_im)], axis=1).astype(BF16)
    return wb, lam, cmat


def _gate_rows(a_log, dt_bias):
    neg_a = -jnp.exp(a_log)
    acoef = jnp.zeros((LANES,), F32).at[0:8].set(neg_a[0]).at[16:24].set(neg_a[1])
    dtb = jnp.zeros((LANES,), F32).at[0:8].set(dt_bias[0]).at[16:24].set(dt_bias[1])
    return acoef[None, :], dtb[None, :], acoef[:32, None], dtb[:32, None]


def kernel(x, c, ctx, c_ctx, w_mod, b_mod, norm_pre_mix, norm_post_mix, norm_pre_ffn, norm_post_ffn, w_in, s5_lam_re, s5_lam_im, s5_log_step, s5_b_re, s5_b_im, s5_c_re, s5_c_im, s5_d, s5_glu_w, s5_glu_b, dn_conv_w, dn_a_log, dn_dt_bias, dn_norm_w, w_branch_s5, w_branch_dn, w_out, peer_w_q, peer_sub_keys, peer_u, peer_v):
    bsz, t_len, d = x.shape
    ctx_len = ctx.shape[1]
    l = 0

    pad = (-(bsz + 1)) % 8
    cv = jnp.concatenate([c, c_ctx[None, :], jnp.zeros((pad, d), F32)], axis=0)
    mod_all = _adaln(cv, w_mod[l].astype(BF16), b_mod[l][None, :])
    mods = mod_all[:bsz].reshape(bsz, 6, d)
    mods_ctx = jnp.broadcast_to(mod_all[bsz].reshape(1, 6, d), (bsz, 6, d))

    w = w_in[l]
    w_main = jnp.concatenate([w[:, COL_K:COL_AB], w[:, COL_Q:]], axis=1).astype(BF16)
    w_ctx = w[:, COL_K:COL_AB].astype(BF16)
    w_u = w[:, COL_U:COL_K].astype(BF16)
    w_ab32 = w[:, COL_AB:COL_Q]
    w_ab = jnp.pad(w_ab32, ((0, 0), (0, LANES - 32))).astype(BF16)
    w_abt = w_ab32.T.astype(BF16)
    gain_pre = norm_pre_mix[l][None, :]

    main_c, u_c, ab_c, abt_c = _inproj(ctx, mods_ctx, gain_pre, w_ctx, w_u, w_ab, w_abt, tm=ctx_len, tn=2 * DN_WIDTH)
    main_x, u_x, ab_x, abt_x = _inproj(x, mods, gain_pre, w_main, w_u, w_ab, w_abt, tm=min(1024, t_len), tn=1536)

    wbf, lamf, cf = _s5_coeffs(s5_lam_re[l, 0], s5_lam_im[l, 0], s5_log_step[l, 0], s5_b_re[l, 0], s5_b_im[l, 0],
                               s5_c_re[l, 0], s5_c_im[l, 0])
    wbb, lamb, cb = _s5_coeffs(s5_lam_re[l, 1], s5_lam_im[l, 1], s5_log_step[l, 1], s5_b_re[l, 1], s5_b_im[l, 1],
                               s5_c_re[l, 1], s5_c_im[l, 1])
    zeros_h = jnp.zeros((bsz, 2 * S5_COLS), F32)
    tc = 32
    hf_c, hb_c = _s5(u_c.reshape(bsz, ctx_len * S5_WIDTH), wbf, wbb, lamf, lamb, cf, cb, zeros_h, zeros_h, tc, False)
    yf, yb, _, _ = _s5(u_x.reshape(bsz, t_len * S5_WIDTH), wbf, wbb, lamf, lamb, cf, cb, hf_c, hb_c, tc, True)
    yf = yf.reshape(bsz, t_len, S5_WIDTH)
    yb = yb.reshape(bsz, t_len, S5_WIDTH)

    acoef, dtb, acoeft, dtbt = _gate_rows(dn_a_log[l], dn_dt_bias[l])
    conv_w = dn_conv_w[l]
    zeros_s = jnp.zeros((bsz, DN_HEADS, DN_HEAD_DIM, DN_HEAD_DIM), F32)
    dn_args = (acoef, dtb, acoeft, dtbt)
    (sf_c,) = _deltanet(main_c, 2 * DN_WIDTH, conv_w[:, :2 * DN_WIDTH], ab_c, abt_c, *dn_args, zeros_s,
                        ctx_len, ctx_len, False, False)
    (sb_c,) = _deltanet(main_c, 2 * DN_WIDTH, conv_w[:, :2 * DN_WIDTH], ab_c, abt_c, *dn_args, zeros_s,
                        ctx_len, ctx_len, True, False)
    blk_len = min(256, t_len)
    o_f, _ = _deltanet(main_x, 3 * DN_WIDTH, conv_w, ab_x, abt_x, *dn_args, sf_c, blk_len, GRID_W, False, True)
    o_b, _ = _deltanet(main_x, 3 * DN_WIDTH, conv_w, ab_x, abt_x, *dn_args, sb_c, blk_len, GRID_W, True, True)

    x1 = _post(x, mods, yf, yb, u_x, o_f, o_b, main_x,
               s5_d[l][None, :], s5_glu_w[l].astype(BF16), s5_glu_b[l][None, :], dn_norm_w[l][None, :],
               w_branch_s5[l].astype(BF16), w_branch_dn[l].astype(BF16), w_out[l].astype(BF16),
               norm_post_mix[l][None, :], tm=min(512, t_len))

    ht, c1, e1, r2, p2 = _route(x1, mods, norm_pre_ffn[l][None, :], peer_w_q[l].astype(BF16),
                                peer_sub_keys[l].astype(BF16), tm=min(256, t_len))
    out = _peer(ht, peer_u[l].astype(BF16), peer_v[l].T.astype(BF16), c1, e1, r2, p2, x1, mods,
                norm_post_ffn[l][None, :], tn=min(512, t_len), eb=512)
    return out
```

```python
import functools
import math

import jax
import jax.numpy as jnp
from jax import lax
from jax.experimental import pallas as pl
from jax.experimental.pallas import tpu as pltpu

F32 = jnp.float32
BF16 = jnp.bfloat16

D_MODEL = 1024
NORM_EPS = 1e-6
GRID_W = 64
S5_WIDTH = 512
S5_GROUP = 16
S5_GROUPS = 32
S5_STATE = 64
S5_COLS = S5_GROUPS * S5_STATE
DN_HEADS = 8
DN_HEAD_DIM = 128
DN_WIDTH = 1024
DN_CONV = 5
DN_CHUNK = 64
PEER_HEADS = 8
PEER_N_KEYS = 128
PEER_KEY_DIM = 128
PEER_TOPK = 16
LANES = 128

COL_U = 0
COL_K = 512
COL_V = 1536
COL_AB = 2560
COL_Q = 2592
COL_Z = 3616
COL_G5 = 4640
COL_GD = 5664

VMEM_LIMIT = 56 * 1024 * 1024


def _dot(a, b):
    return jnp.dot(a, b, preferred_element_type=F32)


def _dot_nt(a, b):
    return lax.dot_general(a, b, (((1,), (1,)), ((), ())), preferred_element_type=F32)


def _dot_tn(a, b):
    return lax.dot_general(a, b, (((0,), (0,)), ((), ())), preferred_element_type=F32)


def _sigmoid(x):
    return jax.nn.sigmoid(x)


def _silu(x):
    return x * _sigmoid(x)


def _gelu(x):
    c = math.sqrt(2.0 / math.pi)
    return x * (0.5 * (1.0 + jnp.tanh(c * (x + 0.044715 * (x * x * x)))))


def _softplus(x):
    return jnp.maximum(x, 0.0) + jnp.log1p(jnp.exp(-jnp.abs(x)))


def _rms(x):
    return x * lax.rsqrt(jnp.mean(x * x, axis=-1, keepdims=True) + NORM_EPS)


def _params(sem):
    return pltpu.CompilerParams(dimension_semantics=sem, vmem_limit_bytes=VMEM_LIMIT)


def _adaln_kernel(c_ref, w_ref, b_ref, o_ref):
    c = c_ref[...]
    o_ref[...] = _dot(_silu(c).astype(BF16), w_ref[...]) + b_ref[...]


def _adaln(cv, w_mod, b_mod):
    rows = cv.shape[0]
    ncol = w_mod.shape[1] // D_MODEL
    return pl.pallas_call(
        _adaln_kernel,
        grid=(ncol,),
        in_specs=[pl.BlockSpec((rows, D_MODEL), lambda j: (0, 0)),
                  pl.BlockSpec((D_MODEL, D_MODEL), lambda j: (0, j)),
                  pl.BlockSpec((1, D_MODEL), lambda j: (0, j))],
        out_specs=pl.BlockSpec((rows, D_MODEL), lambda j: (0, j)),
        out_shape=jax.ShapeDtypeStruct((rows, w_mod.shape[1]), F32),
        compiler_params=_params(("arbitrary",)),
        name="adaln",
    )(cv, w_mod, b_mod)


def _inproj_kernel(x_ref, mod_ref, gain_ref, wm_ref, wu_ref, wab_ref, wabt_ref,
                   main_ref, u_ref, ab_ref, abt_ref, hx_scr):
    j = pl.program_id(2)

    @pl.when(j == 0)
    def _():
        h = _rms(x_ref[...]) * gain_ref[...]
        h = h * (1.0 + mod_ref[1:2, :]) + mod_ref[0:1, :]
        hb = h.astype(BF16)
        hx_scr[...] = hb
        u_ref[...] = _dot(hb, wu_ref[...])
        ab_ref[...] = _dot(hb, wab_ref[...])
        abt_ref[...] = _dot_nt(wabt_ref[...], hb)

    main_ref[...] = _dot(hx_scr[...], wm_ref[...])


def _inproj(x, mods, gain, w_main, w_u, w_ab, w_abt, tm, tn):
    bsz, t_len, d = x.shape
    cm = w_main.shape[1]
    grid = (bsz, t_len // tm, cm // tn)
    return pl.pallas_call(
        _inproj_kernel,
        grid=grid,
        in_specs=[pl.BlockSpec((None, tm, d), lambda b, i, j: (b, i, 0)),
                  pl.BlockSpec((None, 6, d), lambda b, i, j: (b, 0, 0)),
                  pl.BlockSpec((1, d), lambda b, i, j: (0, 0)),
                  pl.BlockSpec((d, tn), lambda b, i, j: (0, j)),
                  pl.BlockSpec((d, S5_WIDTH), lambda b, i, j: (0, 0)),
                  pl.BlockSpec((d, LANES), lambda b, i, j: (0, 0)),
                  pl.BlockSpec((32, d), lambda b, i, j: (0, 0))],
        out_specs=[pl.BlockSpec((None, tm, tn), lambda b, i, j: (b, i, j)),
                   pl.BlockSpec((None, tm, S5_WIDTH), lambda b, i, j: (b, i, 0)),
                   pl.BlockSpec((None, tm, LANES), lambda b, i, j: (b, i, 0)),
                   pl.BlockSpec((None, 32, tm), lambda b, i, j: (b, 0, i))],
        out_shape=[jax.ShapeDtypeStruct((bsz, t_len, cm), F32),
                   jax.ShapeDtypeStruct((bsz, t_len, S5_WIDTH), F32),
                   jax.ShapeDtypeStruct((bsz, t_len, LANES), F32),
                   jax.ShapeDtypeStruct((bsz, 32, t_len), F32)],
        scratch_shapes=[pltpu.VMEM((tm, d), BF16)],
        compiler_params=_params(("arbitrary", "arbitrary", "arbitrary")),
        name="inproj",
    )(x, mods, gain, w_main, w_u, w_ab, w_abt)


def _s5_kernel(uf_ref, ub_ref, wbf_ref, wbb_ref, lamf_ref, lamb_ref, cf_ref, cb_ref,
               h0f_ref, h0b_ref, *rest, tc, bsz, need_out):
    if need_out:
        yf_ref, yb_ref, hff_ref, hfb_ref, xsf, xsb, hcf, hcb = rest
    else:
        hff_ref, hfb_ref, xsf, xsb, hcf, hcb = rest
        yf_ref = yb_ref = None
    i = pl.program_id(0)
    half = S5_COLS
    cw = 512
    nblk = half // cw

    @pl.when(i == 0)
    def _():
        hcf[...] = h0f_ref[...]
        hcb[...] = h0b_ref[...]

    dirs = ((uf_ref, wbf_ref, lamf_ref, cf_ref, yf_ref, xsf, hcf, False),
            (ub_ref, wbb_ref, lamb_ref, cb_ref, yb_ref, xsb, hcb, True))
    for u_ref, wb_ref, lam_ref, c_ref, y_ref, xs, hc, rev in dirs:
        for j in range(4):
            lhs = jnp.concatenate(
                [u_ref[:, t * S5_WIDTH + j * LANES: t * S5_WIDTH + (j + 1) * LANES] for t in range(tc)],
                axis=0).astype(BF16)
            r = _dot(lhs, wb_ref[j])
            xs[:, j * cw:(j + 1) * cw] = r[:, :cw]
            xs[:, half + j * cw: half + (j + 1) * cw] = r[:, cw:]
        for cb in range(nblk):
            lo, hi = cb * cw, (cb + 1) * cw
            ar = jnp.broadcast_to(lam_ref[0:1, lo:hi], (bsz, cw))
            ai = jnp.broadcast_to(lam_ref[1:2, lo:hi], (bsz, cw))

            def body(s, carry, lo=lo, hi=hi, ar=ar, ai=ai, xs=xs, rev=rev):
                hr, hi_ = carry
                t = (tc - 1 - s) if rev else s
                row = pl.multiple_of(t * bsz, bsz)
                xr = xs[pl.ds(row, bsz), lo:hi]
                xi = xs[pl.ds(row, bsz), half + lo: half + hi]
                nr = ar * hr - ai * hi_ + xr
                ni = ar * hi_ + ai * hr + xi
                xs[pl.ds(row, bsz), lo:hi] = nr
                xs[pl.ds(row, bsz), half + lo: half + hi] = ni
                return nr, ni

            hr, hi_ = lax.fori_loop(0, tc, body, (hc[:, lo:hi], hc[:, half + lo: half + hi]))
            hc[:, lo:hi] = hr
            hc[:, half + lo: half + hi] = hi_
        if need_out:
            for j in range(4):
                hre = xs[:, j * cw:(j + 1) * cw].astype(BF16)
                him = xs[:, half + j * cw: half + (j + 1) * cw].astype(BF16)
                y = _dot(hre, c_ref[j, 0]) + _dot(him, c_ref[j, 1])
                for t in range(tc):
                    y_ref[:, t * S5_WIDTH + j * LANES: t * S5_WIDTH + (j + 1) * LANES] = y[t * bsz:(t + 1) * bsz]
    hff_ref[...] = hcf[...]
    hfb_ref[...] = hcb[...]


def _s5(u2, wbf, wbb, lamf, lamb, cf, cb, h0f, h0b, tc, need_out):
    bsz = u2.shape[0]
    t_len = u2.shape[1] // S5_WIDTH
    nc = t_len // tc
    blk = tc * S5_WIDTH
    const2 = lambda i: (0, 0)
    const3 = lambda i: (0, 0, 0)
    const4 = lambda i: (0, 0, 0, 0)
    st_spec = pl.BlockSpec((bsz, 2 * S5_COLS), const2)
    st_shape = jax.ShapeDtypeStruct((bsz, 2 * S5_COLS), F32)
    out_specs = [st_spec, st_spec]
    out_shape = [st_shape, st_shape]
    if need_out:
        y_shape = jax.ShapeDtypeStruct(u2.shape, F32)
        out_specs = [pl.BlockSpec((bsz, blk), lambda i: (0, i)),
                     pl.BlockSpec((bsz, blk), lambda i: (0, nc - 1 - i))] + out_specs
        out_shape = [y_shape, y_shape] + out_shape
    return pl.pallas_call(
        functools.partial(_s5_kernel, tc=tc, bsz=bsz, need_out=need_out),
        grid=(nc,),
        in_specs=[pl.BlockSpec((bsz, blk), lambda i: (0, i)),
                  pl.BlockSpec((bsz, blk), lambda i: (0, nc - 1 - i)),
                  pl.BlockSpec(wbf.shape, const3), pl.BlockSpec(wbb.shape, const3),
                  pl.BlockSpec(lamf.shape, const2), pl.BlockSpec(lamb.shape, const2),
                  pl.BlockSpec(cf.shape, const4), pl.BlockSpec(cb.shape, const4),
                  st_spec, st_spec],
        out_specs=out_specs,
        out_shape=out_shape,
        scratch_shapes=[pltpu.VMEM((tc * bsz, 2 * S5_COLS), F32),
                        pltpu.VMEM((tc * bsz, 2 * S5_COLS), F32),
                        pltpu.VMEM((bsz, 2 * S5_COLS), F32),
                        pltpu.VMEM((bsz, 2 * S5_COLS), F32)],
        compiler_params=_params(("arbitrary",)),
        name="s5_scan",
    )(u2, u2, wbf, wbb, lamf, lamb, cf, cb, h0f, h0b)


def _dn_kernel(kvq_ref, cw_ref, ab_ref, abt_ref, acoef_ref, dtb_ref, acoeft_ref, dtbt_ref, s0_ref,
               *rest, blk_len, row_len, rev, need_out):
    if need_out:
        o_ref, sfin_ref, s_scr, act_scr = rest
    else:
        sfin_ref, s_scr, act_scr = rest
        o_ref = None
    ncols = act_scr.shape[1]
    cs = DN_CHUNK
    nch = blk_len // cs
    goff = 16 if rev else 0
    boff = goff + 8

    @pl.when(pl.program_id(1) == 0)
    def _():
        s_scr[...] = s0_ref[...]

    tpos = lax.broadcasted_iota(jnp.int32, (blk_len, LANES), 0) % row_len
    valid = [(tpos + (j - 2) >= 0) & (tpos + (j - 2) < row_len) for j in range(DN_CONV)]
    for cbk in range(ncols // LANES):
        cols = slice(cbk * LANES, (cbk + 1) * LANES)
        xc = kvq_ref[:, cols]
        acc = xc * cw_ref[2:3, cols]
        for j in (0, 1, 3, 4):
            sh = (2 - j) % blk_len
            acc = acc + jnp.where(valid[j], pltpu.roll(xc, sh, axis=0), 0.0) * cw_ref[j:j + 1, cols]
        act_scr[:, cols] = _silu(acc)

    ab = ab_ref[...]
    g_cols = acoef_ref[...] * _softplus(ab + dtb_ref[...])
    beta_cols = _sigmoid(ab)
    g_rows = acoeft_ref[...] * _softplus(abt_ref[...] + dtbt_ref[...])

    ri = lax.broadcasted_iota(jnp.int32, (cs, cs), 0)
    ci = lax.broadcasted_iota(jnp.int32, (cs, cs), 1)
    incl = (ri <= ci) if rev else (ri >= ci)
    strict = incl & (ri != ci)
    tri = incl.astype(F32)
    eye = (ri == ci).astype(F32)
    last = 0 if rev else cs - 1
    scale = DN_HEAD_DIM ** -0.5

    order = list(range(nch - 1, -1, -1)) if rev else list(range(nch))
    units = [(c, h) for c in order for h in range(DN_HEADS)]
    nu = len(units)
    gc_cols = {}
    gc_rows = {}
    for c in order:
        rows = slice(c * cs, (c + 1) * cs)
        gc_cols[c] = jnp.dot(tri, g_cols[rows, :], precision=lax.Precision.HIGHEST, preferred_element_type=F32)
        gc_rows[c] = lax.dot_general(g_rows[:, rows], tri, (((1,), (1,)), ((), ())),
                                     precision=lax.Precision.HIGHEST, preferred_element_type=F32)

    def head_block(base, c, h):
        return act_scr[c * cs:(c + 1) * cs, base + h * DN_HEAD_DIM: base + (h + 1) * DN_HEAD_DIM]

    def l2n(t):
        return t * lax.rsqrt(jnp.sum(t * t, axis=-1, keepdims=True) + 1e-6)

    gch = [gc_cols[c][:, goff + h: goff + h + 1] for c, h in units]
    gcr = [gc_rows[c][goff + h: goff + h + 1, :] for c, h in units]
    dec = [jnp.where(incl, jnp.exp(jnp.minimum(gch[u] - gcr[u], 0.0)), 0.0) for u in range(nu)]
    bh = [beta_cols[c * cs:(c + 1) * cs, boff + h: boff + h + 1] for c, h in units]
    k = [l2n(head_block(0, c, h)) for c, h in units]
    kbf = [t.astype(BF16) for t in k]
    kb = [k[u] * bh[u] for u in range(nu)]
    kkt = [_dot_nt(kb[u].astype(BF16), kbf[u]) for u in range(nu)]
    a = [jnp.where(strict, kkt[u] * dec[u], 0.0) for u in range(nu)]
    abf = [t.astype(BF16) for t in a]
    m = [_dot(t, t) for t in abf]
    p = [eye - t for t in a]
    for lvl in range(5):
        mb = [t.astype(BF16) for t in m]
        if lvl < 4:
            r = [_dot(jnp.concatenate([m[u], p[u]], axis=0).astype(BF16), mb[u]) for u in range(nu)]
            m = [t[:cs] for t in r]
            p = [p[u] + r[u][cs:] for u in range(nu)]
        else:
            r = [_dot(p[u].astype(BF16), mb[u]) for u in range(nu)]
            p = [p[u] + r[u] for u in range(nu)]
    eg = [jnp.exp(t) for t in gch]
    rhs = [jnp.concatenate([head_block(DN_WIDTH, c, h) * bh[u], kb[u] * eg[u]], axis=-1).astype(BF16)
           for u, (c, h) in enumerate(units)]
    uw = [_dot(p[u].astype(BF16), rhs[u]) for u in range(nu)]
    gl = [t[last:last + 1, :] for t in gch]
    kd = [(k[u] * jnp.exp(gl[u] - gch[u])).astype(BF16) for u in range(nu)]
    if need_out:
        q = [l2n(head_block(2 * DN_WIDTH, c, h)) * scale for c, h in units]
        qkt = [_dot_nt(q[u].astype(BF16), kbf[u]) for u in range(nu)]
        at = [jnp.where(incl, qkt[u] * dec[u], 0.0).astype(BF16) for u in range(nu)]
        qd = [(q[u] * eg[u]).astype(BF16) for u in range(nu)]

    for ci, c in enumerate(order):
        us = range(ci * DN_HEADS, (ci + 1) * DN_HEADS)
        s_old = [s_scr[h] for h in range(DN_HEADS)]
        s_bf = [t.astype(BF16) for t in s_old]
        ws = [_dot(uw[u][:, DN_HEAD_DIM:].astype(BF16), s_bf[h]) for h, u in enumerate(us)]
        if need_out:
            qs = [_dot(qd[u], s_bf[h]) for h, u in enumerate(us)]
        vn = [(uw[u][:, :DN_HEAD_DIM] - ws[h]).astype(BF16) for h, u in enumerate(us)]
        if need_out:
            av = [_dot(at[u], vn[h]) for h, u in enumerate(us)]
        kv = [_dot_tn(kd[u], vn[h]) for h, u in enumerate(us)]
        for h, u in enumerate(us):
            if need_out:
                o_ref[c * cs:(c + 1) * cs, h * DN_HEAD_DIM:(h + 1) * DN_HEAD_DIM] = qs[h] + av[h]
            s_scr[h] = s_old[h] * jnp.exp(gl[u]) + kv[h]
    sfin_ref[...] = s_scr[...]


def _deltanet(kvq, ncols, conv_w, ab, abt, acoef, dtb, acoeft, dtbt, s0, blk_len, row_len, rev, need_out):
    bsz, t_len, _ = kvq.shape
    nb = t_len // blk_len
    bi = (lambda b, i: (b, nb - 1 - i, 0)) if rev else (lambda b, i: (b, i, 0))
    bit = (lambda b, i: (b, 0, nb - 1 - i)) if rev else (lambda b, i: (b, 0, i))
    c2 = lambda b, i: (0, 0)
    st_spec = pl.BlockSpec((None, DN_HEADS, DN_HEAD_DIM, DN_HEAD_DIM), lambda b, i: (b, 0, 0, 0))
    st_shape = jax.ShapeDtypeStruct((bsz, DN_HEADS, DN_HEAD_DIM, DN_HEAD_DIM), F32)
    out_specs = [st_spec]
    out_shape = [st_shape]
    if need_out:
        out_specs = [pl.BlockSpec((None, blk_len, DN_WIDTH), bi)] + out_specs
        out_shape = [jax.ShapeDtypeStruct((bsz, t_len, DN_WIDTH), F32)] + out_shape
    return pl.pallas_call(
        functools.partial(_dn_kernel, blk_len=blk_len, row_len=row_len, rev=rev, need_out=need_out),
        grid=(bsz, nb),
        in_specs=[pl.BlockSpec((None, blk_len, ncols), bi),
                  pl.BlockSpec((DN_CONV, ncols), c2),
                  pl.BlockSpec((None, blk_len, LANES), bi),
                  pl.BlockSpec((None, 32, blk_len), bit),
                  pl.BlockSpec((1, LANES), c2), pl.BlockSpec((1, LANES), c2),
                  pl.BlockSpec((32, 1), c2), pl.BlockSpec((32, 1), c2),
                  st_spec],
        out_specs=out_specs,
        out_shape=out_shape,
        scratch_shapes=[pltpu.VMEM((DN_HEADS, DN_HEAD_DIM, DN_HEAD_DIM), F32),
                        pltpu.VMEM((blk_len, ncols), F32)],
        compiler_params=_params(("arbitrary", "arbitrary")),
        name="deltanet_bwd" if rev else "deltanet_fwd",
    )(kvq, conv_w, ab, abt, acoef, dtb, acoeft, dtbt, s0)


def _post_kernel(x_ref, mod_ref, yf_ref, yb_ref, u_ref, of_ref, ob_ref, z_ref, g5_ref, gd_ref,
                 s5d_ref, glw_ref, glb_ref, dnw_ref, wb5_ref, wbd_ref, wo_ref, gain_ref, o_ref):
    y5 = _gelu(yf_ref[...] + yb_ref[...] + s5d_ref[...] * u_ref[...])
    y5 = y5 * _sigmoid(_dot(y5.astype(BF16), glw_ref[...]) + glb_ref[...])
    o = of_ref[...] + ob_ref[...]
    parts = []
    for h in range(DN_HEADS):
        oh = o[:, h * DN_HEAD_DIM:(h + 1) * DN_HEAD_DIM]
        parts.append(_rms(oh) * dnw_ref[...])
    y_dn = jnp.concatenate(parts, axis=-1) * _silu(z_ref[...])
    merged = (_sigmoid(g5_ref[...]) * _dot(y5.astype(BF16), wb5_ref[...])
              + _sigmoid(gd_ref[...]) * _dot(y_dn.astype(BF16), wbd_ref[...]))
    x_mix = _dot(merged.astype(BF16), wo_ref[...])
    o_ref[...] = x_ref[...] + mod_ref[2:3, :] * (_rms(x_mix) * gain_ref[...])


def _post(x, mods, yf, yb, u, o_f, o_b, main, s5d, glw, glb, dnw, wb5, wbd, wo, gain, tm):
    bsz, t_len, d = x.shape
    tok = lambda w: pl.BlockSpec((None, tm, w), lambda b, i: (b, i, 0))
    colblk = lambda k: pl.BlockSpec((None, tm, d), lambda b, i, k=k: (b, i, k))
    full = lambda a: pl.BlockSpec(a.shape, lambda b, i: (0,) * a.ndim)
    return pl.pallas_call(
        _post_kernel,
        grid=(bsz, t_len // tm),
        in_specs=[tok(d), pl.BlockSpec((None, 6, d), lambda b, i: (b, 0, 0)),
                  tok(S5_WIDTH), tok(S5_WIDTH), tok(S5_WIDTH), tok(d), tok(d),
                  colblk(3), colblk(4), colblk(5),
                  full(s5d), full(glw), full(glb), full(dnw), full(wb5), full(wbd), full(wo), full(gain)],
        out_specs=tok(d),
        out_shape=jax.ShapeDtypeStruct(x.shape, F32),
        compiler_params=_params(("arbitrary", "arbitrary")),
        name="mixer_out",
    )(x, mods, yf, yb, u, o_f, o_b, main, main, main, s5d, glw, glb, dnw, wb5, wbd, wo, gain)


_CAND_COUNTS = [PEER_TOPK // (a + 1) for a in range(PEER_TOPK)]
_N_CAND = sum(_CAND_COUNTS)
_CAND_ROWS = -(-_N_CAND // 8) * 8


def _extract_topk(work, k, out_vals_ref=None, want_rank=False):
    nrows = work.shape[0]
    iota = lax.broadcasted_iota(jnp.int32, work.shape, 0)
    rank = jnp.full(work.shape, float(k), F32) if want_rank else None
    vals = []
    for r in range(k):
        m = jnp.max(work, axis=0, keepdims=True)
        idx = jnp.min(jnp.where(work == m, iota, nrows), axis=0, keepdims=True)
        sel = iota == idx
        work = jnp.where(sel, -jnp.inf, work)
        if want_rank:
            rank = jnp.where(sel, float(r), rank)
        vals.append(m)
        if out_vals_ref is not None:
            out_vals_ref[r:r + 1, :] = m
    return vals, work, rank


def _route_kernel(x_ref, mod_ref, gain_ref, wq_ref, keys_ref,
                  ht_ref, c1_ref, e1_ref, r2_ref, p2_ref, sv_scr, cand_scr):
    h = _rms(x_ref[...]) * gain_ref[...]
    h = h * (1.0 + mod_ref[4:5, :]) + mod_ref[3:4, :]
    hb = h.astype(BF16)
    ht_ref[...] = h.T.astype(BF16)
    q = _dot(hb, wq_ref[...]).astype(BF16)
    for hd in range(PEER_HEADS):
        svals = []
        ranks = []
        scores = []
        for j in range(2):
            col = (hd * 2 + j) * PEER_KEY_DIM
            s = _dot_nt(keys_ref[hd, j], q[:, col: col + PEER_KEY_DIM])
            vals, _, rank = _extract_topk(s, PEER_TOPK, sv_scr.at[j], want_rank=True)
            scores.append(s)
            svals.append(vals)
            ranks.append(rank)
        cand_scr[...] = jnp.full(cand_scr.shape, -jnp.inf, F32)
        off = 0
        for a in range(PEER_TOPK):
            nb = _CAND_COUNTS[a]
            cand_scr[off: off + nb, :] = svals[0][a] + sv_scr[1, 0:nb, :]
            off += nb
        top, left, _ = _extract_topk(cand_scr[...], PEER_TOPK)
        m0 = top[0]
        z = jnp.exp(top[0] - m0)
        for r in range(1, PEER_TOPK):
            z = z + jnp.exp(top[r] - m0)
        taken = (left == -jnp.inf).astype(F32)
        c1 = jnp.zeros(ranks[0].shape, F32)
        off = 0
        for a in range(PEER_TOPK):
            nb = _CAND_COUNTS[a]
            cnt = jnp.sum(taken[off: off + nb, :], axis=0, keepdims=True)
            c1 = jnp.where(ranks[0] == float(a), cnt, c1)
            off += nb
        c1_ref[hd] = c1
        r2_ref[hd] = ranks[1]
        e1_ref[hd] = jnp.where(ranks[0] < float(PEER_TOPK), jnp.exp(scores[0] - svals[0][0]), 0.0)
        p2_ref[hd] = jnp.where(ranks[1] < float(PEER_TOPK), jnp.exp(scores[1] - svals[1][0]), 0.0) / z


def _route(x1, mods, gain, wq, keys, tm):
    bsz, t_len, d = x1.shape
    n_tok = bsz * t_len
    nt = t_len // tm
    tile = lambda b, i: (0, 0, b * nt + i)
    hk = pl.BlockSpec((PEER_HEADS, PEER_N_KEYS, tm), tile)
    hk_shape = jax.ShapeDtypeStruct((PEER_HEADS, PEER_N_KEYS, n_tok), F32)
    return pl.pallas_call(
        _route_kernel,
        grid=(bsz, nt),
        in_specs=[pl.BlockSpec((None, tm, d), lambda b, i: (b, i, 0)),
                  pl.BlockSpec((None, 6, d), lambda b, i: (b, 0, 0)),
                  pl.BlockSpec((1, d), lambda b, i: (0, 0)),
                  pl.BlockSpec(wq.shape, lambda b, i: (0, 0)),
                  pl.BlockSpec(keys.shape, lambda b, i: (0, 0, 0, 0))],
        out_specs=[pl.BlockSpec((d, tm), lambda b, i: (0, b * nt + i)),
                   hk, hk, hk, hk],
        out_shape=[jax.ShapeDtypeStruct((d, n_tok), BF16),
                   hk_shape, hk_shape, hk_shape, hk_shape],
        scratch_shapes=[pltpu.VMEM((2, PEER_TOPK, tm), F32),
                        pltpu.VMEM((_CAND_ROWS, tm), F32)],
        compiler_params=_params(("arbitrary", "arbitrary")),
        name="peer_route",
    )(x1, mods, gain, wq, keys)


def _peer_kernel(ht_ref, u_ref, vt_ref, c1_ref, e1_ref, r2_ref, p2_ref, x1_ref, mod_ref, gain_ref,
                 o_ref, acc, act0, act1, a0, a1, *, nsub):
    e = pl.program_id(2)
    first = (pl.program_id(0) == 0) & (pl.program_id(1) == 0) & (e == 0)
    act_scr = (act0, act1)
    a_scr = (a0, a1)

    @pl.when(first)
    def _():
        for buf in act_scr:
            buf[...] = jnp.zeros(buf.shape, F32)
        for buf in a_scr:
            buf[...] = jnp.zeros(buf.shape, BF16)

    @pl.when(e == 2)
    def _():
        acc[...] = jnp.zeros(acc.shape, F32)

    ncol = act0.shape[1] // LANES
    tr = 32
    nrt = PEER_N_KEYS // tr

    def step(cur, prev):
        act_scr[cur][...] = _dot(u_ref[...], ht_ref[...])
        acc[...] += _dot(vt_ref[...], a_scr[cur][...])
        for ct in range(ncol):
            cols = slice(ct * LANES, (ct + 1) * LANES)
            for rt in range(nrt):
                krows = slice(rt * tr, (rt + 1) * tr)
                ws = [None] * nsub
                for hd in range(PEER_HEADS):
                    r2t = r2_ref[hd, krows, cols]
                    p2t = p2_ref[hd, krows, cols]
                    for ii in range(nsub):
                        contrib = jnp.where(r2t < c1_ref[hd, ii:ii + 1, cols],
                                            p2t * e1_ref[hd, ii:ii + 1, cols], 0.0)
                        ws[ii] = contrib if ws[ii] is None else ws[ii] + contrib
                for ii in range(nsub):
                    rows = slice(ii * PEER_N_KEYS + rt * tr, ii * PEER_N_KEYS + (rt + 1) * tr)
                    a_scr[prev][rows, cols] = (_gelu(act_scr[prev][rows, cols]) * ws[ii]).astype(BF16)

    @pl.when(e % 2 == 0)
    def _():
        step(0, 1)

    @pl.when(e % 2 == 1)
    def _():
        step(1, 0)

    @pl.when(e == pl.num_programs(2) - 1)
    def _():
        out = acc[...].T
        o_ref[...] = x1_ref[...] + mod_ref[5:6, :] * (_rms(out) * gain_ref[...])


def _peer(ht, u_tab, vt_tab, c1, e1, r2, p2, x1, mods, gain, tn, eb):
    bsz, t_len, d = x1.shape
    nt = t_len // tn
    n_exp = u_tab.shape[0]
    nsub = eb // PEER_N_KEYS
    ne = n_exp // eb
    n_tok = bsz * t_len
    c1v = c1.reshape(PEER_HEADS, ne, nsub, n_tok)
    e1v = e1.reshape(PEER_HEADS, ne, nsub, n_tok)
    blk = lambda e, lag: jnp.clip(e - lag, 0, ne - 1)
    sub = pl.BlockSpec((PEER_HEADS, None, nsub, tn), lambda b, i, e: (0, blk(e, 1), 0, b * nt + i))
    hk = pl.BlockSpec((PEER_HEADS, PEER_N_KEYS, tn), lambda b, i, e: (0, 0, b * nt + i))
    return pl.pallas_call(
        functools.partial(_peer_kernel, nsub=nsub),
        grid=(bsz, nt, ne + 2),
        in_specs=[pl.BlockSpec((d, tn), lambda b, i, e: (0, b * nt + i)),
                  pl.BlockSpec((eb, d), lambda b, i, e: (blk(e, 0), 0)),
                  pl.BlockSpec((d, eb), lambda b, i, e: (0, blk(e, 2))),
                  sub, sub, hk, hk,
                  pl.BlockSpec((None, tn, d), lambda b, i, e: (b, i, 0)),
                  pl.BlockSpec((None, 6, d), lambda b, i, e: (b, 0, 0)),
                  pl.BlockSpec((1, d), lambda b, i, e: (0, 0))],
        out_specs=pl.BlockSpec((None, tn, d), lambda b, i, e: (b, i, 0)),
        out_shape=jax.ShapeDtypeStruct(x1.shape, F32),
        scratch_shapes=[pltpu.VMEM((d, tn), F32), pltpu.VMEM((eb, tn), F32), pltpu.VMEM((eb, tn), F32),
                        pltpu.VMEM((eb, tn), BF16), pltpu.VMEM((eb, tn), BF16)],
        compiler_params=_params(("arbitrary", "arbitrary", "arbitrary")),
        name="peer_experts",
    )(ht, u_tab, vt_tab, c1v, e1v, r2, p2, x1, mods, gain)


def _s5_coeffs(lam_re, lam_im, log_step, b_re, b_im, c_re, c_im):
    step = jnp.exp(log_step)[:, None]
    mag = jnp.exp(lam_re * step)
    ab_re, ab_im = mag * jnp.cos(lam_im * step), mag * jnp.sin(lam_im * step)
    den = lam_re * lam_re + lam_im * lam_im
    f_re = ((ab_re - 1.0) * lam_re + ab_im * lam_im) / den
    f_im = (ab_im * lam_re - (ab_re - 1.0) * lam_im) / den
    bb_re = f_re[..., None] * b_re - f_im[..., None] * b_im
    bb_im = f_re[..., None] * b_im + f_im[..., None] * b_re
    lam = jnp.stack([ab_re.reshape(-1), ab_im.reshape(-1)])
    eye8 = jnp.eye(8, dtype=F32)

    def in_blockdiag(bb):
        t = bb.reshape(4, 8, S5_STATE, S5_GROUP).transpose(0, 1, 3, 2)
        return jnp.einsum('jghp,gk->jghkp', t, eye8).reshape(4, LANES, 8 * S5_STATE)

    def out_blockdiag(cc):
        t = cc.reshape(4, 8, S5_GROUP, S5_STATE).transpose(0, 1, 3, 2)
        return jnp.einsum('jgph,gk->jgpkh', t, eye8).reshape(4, 8 * S5_STATE, LANES)

    wb = jnp.concatenate([in_blockdiag(bb_re), in_blockdiag(bb_im)], axis=-1).astype(BF16)
    cmat = jnp.stack([out_blockdiag(c_re), -out_blockdiag(c_im)], axis=1).astype(BF16)
    return wb, lam, cmat


def _gate_rows(a_log, dt_bias):
    neg_a = -jnp.exp(a_log)
    acoef = jnp.zeros((LANES,), F32).at[0:8].set(neg_a[0]).at[16:24].set(neg_a[1])
    dtb = jnp.zeros((LANES,), F32).at[0:8].set(dt_bias[0]).at[16:24].set(dt_bias[1])
    return acoef[None, :], dtb[None, :], acoef[:32, None], dtb[:32, None]


def kernel(x, c, ctx, c_ctx, w_mod, b_mod, norm_pre_mix, norm_post_mix, norm_pre_ffn, norm_post_ffn, w_in, s5_lam_re, s5_lam_im, s5_log_step, s5_b_re, s5_b_im, s5_c_re, s5_c_im, s5_d, s5_glu_w, s5_glu_b, dn_conv_w, dn_a_log, dn_dt_bias, dn_norm_w, w_branch_s5, w_branch_dn, w_out, peer_w_q, peer_sub_keys, peer_u, peer_v):
    bsz, t_len, d = x.shape
    ctx_len = ctx.shape[1]
    l = 0

    pad = (-(bsz + 1)) % 8
    cv = jnp.concatenate([c, c_ctx[None, :], jnp.zeros((pad, d), F32)], axis=0)
    mod_all = _adaln(cv, w_mod[l].astype(BF16), b_mod[l][None, :])
    mods = mod_all[:bsz].reshape(bsz, 6, d)
    mods_ctx = jnp.broadcast_to(mod_all[bsz].reshape(1, 6, d), (bsz, 6, d))

    w = w_in[l]
    w_main = jnp.concatenate([w[:, COL_K:COL_AB], w[:, COL_Q:]], axis=1).astype(BF16)
    w_ctx = w[:, COL_K:COL_AB].astype(BF16)
    w_u = w[:, COL_U:COL_K].astype(BF16)
    w_ab32 = w[:, COL_AB:COL_Q]
    w_ab = jnp.pad(w_ab32, ((0, 0), (0, LANES - 32))).astype(BF16)
    w_abt = w_ab32.T.astype(BF16)
    gain_pre = norm_pre_mix[l][None, :]

    main_c, u_c, ab_c, abt_c = _inproj(ctx, mods_ctx, gain_pre, w_ctx, w_u, w_ab, w_abt, tm=ctx_len, tn=2 * DN_WIDTH)
    main_x, u_x, ab_x, abt_x = _inproj(x, mods, gain_pre, w_main, w_u, w_ab, w_abt, tm=min(1024, t_len), tn=1536)

    wbf, lamf, cf = _s5_coeffs(s5_lam_re[l, 0], s5_lam_im[l, 0], s5_log_step[l, 0], s5_b_re[l, 0], s5_b_im[l, 0],
                               s5_c_re[l, 0], s5_c_im[l, 0])
    wbb, lamb, cb = _s5_coeffs(s5_lam_re[l, 1], s5_lam_im[l, 1], s5_log_step[l, 1], s5_b_re[l, 1], s5_b_im[l, 1],
                               s5_c_re[l, 1], s5_c_im[l, 1])
    zeros_h = jnp.zeros((bsz, 2 * S5_COLS), F32)
    tc = 32
    hf_c, hb_c = _s5(u_c.reshape(bsz, ctx_len * S5_WIDTH), wbf, wbb, lamf, lamb, cf, cb, zeros_h, zeros_h, tc, False)
    yf, yb, _, _ = _s5(u_x.reshape(bsz, t_len * S5_WIDTH), wbf, wbb, lamf, lamb, cf, cb, hf_c, hb_c, tc, True)
    yf = yf.reshape(bsz, t_len, S5_WIDTH)
    yb = yb.reshape(bsz, t_len, S5_WIDTH)

    acoef, dtb, acoeft, dtbt = _gate_rows(dn_a_log[l], dn_dt_bias[l])
    conv_w = dn_conv_w[l]
    zeros_s = jnp.zeros((bsz, DN_HEADS, DN_HEAD_DIM, DN_HEAD_DIM), F32)
    dn_args = (acoef, dtb, acoeft, dtbt)
    (sf_c,) = _deltanet(main_c, 2 * DN_WIDTH, conv_w[:, :2 * DN_WIDTH], ab_c, abt_c, *dn_args, zeros_s,
                        ctx_len, ctx_len, False, False)
    (sb_c,) = _deltanet(main_c, 2 * DN_WIDTH, conv_w[:, :2 * DN_WIDTH], ab_c, abt_c, *dn_args, zeros_s,
                        ctx_len, ctx_len, True, False)
    blk_len = min(256, t_len)
    o_f, _ = _deltanet(main_x, 3 * DN_WIDTH, conv_w, ab_x, abt_x, *dn_args, sf_c, blk_len, GRID_W, False, True)
    o_b, _ = _deltanet(main_x, 3 * DN_WIDTH, conv_w, ab_x, abt_x, *dn_args, sb_c, blk_len, GRID_W, True, True)

    x1 = _post(x, mods, yf, yb, u_x, o_f, o_b, main_x,
               s5_d[l][None, :], s5_glu_w[l].astype(BF16), s5_glu_b[l][None, :], dn_norm_w[l][None, :],
               w_branch_s5[l].astype(BF16), w_branch_dn[l].astype(BF16), w_out[l].astype(BF16),
               norm_post_mix[l][None, :], tm=min(512, t_len))

    ht, c1, e1, r2, p2 = _route(x1, mods, norm_pre_ffn[l][None, :], peer_w_q[l].astype(BF16),
                                peer_sub_keys[l].astype(BF16), tm=min(256, t_len))
    out = _peer(ht, peer_u[l].astype(BF16), peer_v[l].T.astype(BF16), c1, e1, r2, p2, x1, mods,
                norm_post_ffn[l][None, :], tn=min(512, t_len), eb=512)
    return out
```

```python
import functools
import math

import jax
import jax.numpy as jnp
from jax import lax
from jax.experimental import pallas as pl
from jax.experimental.pallas import tpu as pltpu

F32 = jnp.float32
BF16 = jnp.bfloat16

D_MODEL = 1024
NORM_EPS = 1e-6
GRID_W = 64
S5_WIDTH = 512
S5_GROUP = 16
S5_GROUPS = 32
S5_STATE = 64
S5_COLS = S5_GROUPS * S5_STATE
DN_HEADS = 8
DN_HEAD_DIM = 128
DN_WIDTH = 1024
DN_CONV = 5
DN_CHUNK = 64
PEER_HEADS = 8
PEER_N_KEYS = 128
PEER_KEY_DIM = 128
PEER_TOPK = 16
LANES = 128

COL_U = 0
COL_K = 512
COL_V = 1536
COL_AB = 2560
COL_Q = 2592
COL_Z = 3616
COL_G5 = 4640
COL_GD = 5664

VMEM_LIMIT = 56 * 1024 * 1024


def _dot(a, b):
    return jnp.dot(a, b, preferred_element_type=F32)


def _dot_nt(a, b):
    return lax.dot_general(a, b, (((1,), (1,)), ((), ())), preferred_element_type=F32)


def _dot_tn(a, b):
    return lax.dot_general(a, b, (((0,), (0,)), ((), ())), preferred_element_type=F32)


def _sigmoid(x):
    return jax.nn.sigmoid(x)


def _silu(x):
    return x * _sigmoid(x)


def _gelu(x):
    c = math.sqrt(2.0 / math.pi)
    return x * (0.5 * (1.0 + jnp.tanh(c * (x + 0.044715 * (x * x * x)))))


def _softplus(x):
    return jnp.maximum(x, 0.0) + jnp.log1p(jnp.exp(-jnp.abs(x)))


def _rms(x):
    return x * lax.rsqrt(jnp.mean(x * x, axis=-1, keepdims=True) + NORM_EPS)


def _params(sem):
    return pltpu.CompilerParams(dimension_semantics=sem, vmem_limit_bytes=VMEM_LIMIT)


def _adaln_kernel(c_ref, w_ref, b_ref, o_ref):
    c = c_ref[...]
    o_ref[...] = _dot(_silu(c).astype(BF16), w_ref[...]) + b_ref[...]


def _adaln(cv, w_mod, b_mod):
    rows = cv.shape[0]
    ncol = w_mod.shape[1] // D_MODEL
    return pl.pallas_call(
        _adaln_kernel,
        grid=(ncol,),
        in_specs=[pl.BlockSpec((rows, D_MODEL), lambda j: (0, 0)),
                  pl.BlockSpec((D_MODEL, D_MODEL), lambda j: (0, j)),
                  pl.BlockSpec((1, D_MODEL), lambda j: (0, j))],
        out_specs=pl.BlockSpec((rows, D_MODEL), lambda j: (0, j)),
        out_shape=jax.ShapeDtypeStruct((rows, w_mod.shape[1]), F32),
        compiler_params=_params(("arbitrary",)),
        name="adaln",
    )(cv, w_mod, b_mod)


def _inproj_kernel(x_ref, mod_ref, gain_ref, wm_ref, wu_ref, wab_ref, wabt_ref,
                   main_ref, u_ref, ab_ref, abt_ref, hx_scr):
    j = pl.program_id(2)

    @pl.when(j == 0)
    def _():
        h = _rms(x_ref[...]) * gain_ref[...]
        h = h * (1.0 + mod_ref[1:2, :]) + mod_ref[0:1, :]
        hb = h.astype(BF16)
        hx_scr[...] = hb
        u_ref[...] = _dot(hb, wu_ref[...])
        ab_ref[...] = _dot(hb, wab_ref[...])
        abt_ref[...] = _dot_nt(wabt_ref[...], hb)

    main_ref[...] = _dot(hx_scr[...], wm_ref[...])


def _inproj(x, mods, gain, w_main, w_u, w_ab, w_abt, tm, tn):
    bsz, t_len, d = x.shape
    cm = w_main.shape[1]
    grid = (bsz, t_len // tm, cm // tn)
    return pl.pallas_call(
        _inproj_kernel,
        grid=grid,
        in_specs=[pl.BlockSpec((None, tm, d), lambda b, i, j: (b, i, 0)),
                  pl.BlockSpec((None, 6, d), lambda b, i, j: (b, 0, 0)),
                  pl.BlockSpec((1, d), lambda b, i, j: (0, 0)),
                  pl.BlockSpec((d, tn), lambda b, i, j: (0, j)),
                  pl.BlockSpec((d, S5_WIDTH), lambda b, i, j: (0, 0)),
                  pl.BlockSpec((d, LANES), lambda b, i, j: (0, 0)),
                  pl.BlockSpec((32, d), lambda b, i, j: (0, 0))],
        out_specs=[pl.BlockSpec((None, tm, tn), lambda b, i, j: (b, i, j)),
                   pl.BlockSpec((None, tm, S5_WIDTH), lambda b, i, j: (b, i, 0)),
                   pl.BlockSpec((None, tm, LANES), lambda b, i, j: (b, i, 0)),
                   pl.BlockSpec((None, 32, tm), lambda b, i, j: (b, 0, i))],
        out_shape=[jax.ShapeDtypeStruct((bsz, t_len, cm), F32),
                   jax.ShapeDtypeStruct((bsz, t_len, S5_WIDTH), F32),
                   jax.ShapeDtypeStruct((bsz, t_len, LANES), F32),
                   jax.ShapeDtypeStruct((bsz, 32, t_len), F32)],
        scratch_shapes=[pltpu.VMEM((tm, d), BF16)],
        compiler_params=_params(("arbitrary", "arbitrary", "arbitrary")),
        name="inproj",
    )(x, mods, gain, w_main, w_u, w_ab, w_abt)


def _s5_kernel(uf_ref, ub_ref, wbf_ref, wbb_ref, lamf_ref, lamb_ref, cf_ref, cb_ref,
               h0f_ref, h0b_ref, *rest, tc, bsz, need_out):
    if need_out:
        yf_ref, yb_ref, hff_ref, hfb_ref, xsf, xsb, hcf, hcb = rest
    else:
        hff_ref, hfb_ref, xsf, xsb, hcf, hcb = rest
        yf_ref = yb_ref = None
    i = pl.program_id(0)
    half = S5_COLS
    cw = 512
    nblk = half // cw

    @pl.when(i == 0)
    def _():
        hcf[...] = h0f_ref[...]
        hcb[...] = h0b_ref[...]

    dirs = ((uf_ref, wbf_ref, lamf_ref, cf_ref, yf_ref, xsf, hcf, False),
            (ub_ref, wbb_ref, lamb_ref, cb_ref, yb_ref, xsb, hcb, True))
    for u_ref, wb_ref, lam_ref, c_ref, y_ref, xs, hc, rev in dirs:
        for j in range(4):
            lhs = jnp.concatenate(
                [u_ref[:, t * S5_WIDTH + j * LANES: t * S5_WIDTH + (j + 1) * LANES] for t in range(tc)],
                axis=0).astype(BF16)
            r = _dot(lhs, wb_ref[j])
            xs[:, j * cw:(j + 1) * cw] = r[:, :cw]
            xs[:, half + j * cw: half + (j + 1) * cw] = r[:, cw:]
        for cb in range(nblk):
            lo, hi = cb * cw, (cb + 1) * cw
            ar = jnp.broadcast_to(lam_ref[0:1, lo:hi], (bsz, cw))
            ai = jnp.broadcast_to(lam_ref[1:2, lo:hi], (bsz, cw))

            def body(s, carry, lo=lo, hi=hi, ar=ar, ai=ai, xs=xs, rev=rev):
                hr, hi_ = carry
                t = (tc - 1 - s) if rev else s
                row = pl.multiple_of(t * bsz, bsz)
                xr = xs[pl.ds(row, bsz), lo:hi]
                xi = xs[pl.ds(row, bsz), half + lo: half + hi]
                nr = ar * hr - ai * hi_ + xr
                ni = ar * hi_ + ai * hr + xi
                xs[pl.ds(row, bsz), lo:hi] = nr
                xs[pl.ds(row, bsz), half + lo: half + hi] = ni
                return nr, ni

            hr, hi_ = lax.fori_loop(0, tc, body, (hc[:, lo:hi], hc[:, half + lo: half + hi]))
            hc[:, lo:hi] = hr
            hc[:, half + lo: half + hi] = hi_
        if need_out:
            for j in range(4):
                hre = xs[:, j * cw:(j + 1) * cw].astype(BF16)
                him = xs[:, half + j * cw: half + (j + 1) * cw].astype(BF16)
                y = _dot(hre, c_ref[j, 0]) + _dot(him, c_ref[j, 1])
                for t in range(tc):
                    y_ref[:, t * S5_WIDTH + j * LANES: t * S5_WIDTH + (j + 1) * LANES] = y[t * bsz:(t + 1) * bsz]
    hff_ref[...] = hcf[...]
    hfb_ref[...] = hcb[...]


def _s5(u2, wbf, wbb, lamf, lamb, cf, cb, h0f, h0b, tc, need_out):
    bsz = u2.shape[0]
    t_len = u2.shape[1] // S5_WIDTH
    nc = t_len // tc
    blk = tc * S5_WIDTH
    const2 = lambda i: (0, 0)
    const3 = lambda i: (0, 0, 0)
    const4 = lambda i: (0, 0, 0, 0)
    st_spec = pl.BlockSpec((bsz, 2 * S5_COLS), const2)
    st_shape = jax.ShapeDtypeStruct((bsz, 2 * S5_COLS), F32)
    out_specs = [st_spec, st_spec]
    out_shape = [st_shape, st_shape]
    if need_out:
        y_shape = jax.ShapeDtypeStruct(u2.shape, F32)
        out_specs = [pl.BlockSpec((bsz, blk), lambda i: (0, i)),
                     pl.BlockSpec((bsz, blk), lambda i: (0, nc - 1 - i))] + out_specs
        out_shape = [y_shape, y_shape] + out_shape
    return pl.pallas_call(
        functools.partial(_s5_kernel, tc=tc, bsz=bsz, need_out=need_out),
        grid=(nc,),
        in_specs=[pl.BlockSpec((bsz, blk), lambda i: (0, i)),
                  pl.BlockSpec((bsz, blk), lambda i: (0, nc - 1 - i)),
                  pl.BlockSpec(wbf.shape, const3), pl.BlockSpec(wbb.shape, const3),
                  pl.BlockSpec(lamf.shape, const2), pl.BlockSpec(lamb.shape, const2),
                  pl.BlockSpec(cf.shape, const4), pl.BlockSpec(cb.shape, const4),
                  st_spec, st_spec],
        out_specs=out_specs,
        out_shape=out_shape,
        scratch_shapes=[pltpu.VMEM((tc * bsz, 2 * S5_COLS), F32),
                        pltpu.VMEM((tc * bsz, 2 * S5_COLS), F32),
                        pltpu.VMEM((bsz, 2 * S5_COLS), F32),
                        pltpu.VMEM((bsz, 2 * S5_COLS), F32)],
        compiler_params=_params(("arbitrary",)),
        name="s5_scan",
    )(u2, u2, wbf, wbb, lamf, lamb, cf, cb, h0f, h0b)


def _dn_kernel(kvq_ref, cw_ref, ab_ref, abt_ref, acoef_ref, dtb_ref, acoeft_ref, dtbt_ref, s0_ref,
               *rest, blk_len, row_len, rev, need_out):
    if need_out:
        o_ref, sfin_ref, s_scr, act_scr = rest
    else:
        sfin_ref, s_scr, act_scr = rest
        o_ref = None
    ncols = act_scr.shape[1]
    cs = DN_CHUNK
    nch = blk_len // cs
    goff = 16 if rev else 0
    boff = goff + 8

    @pl.when(pl.program_id(1) == 0)
    def _():
        s_scr[...] = s0_ref[...]

    tpos = lax.broadcasted_iota(jnp.int32, (blk_len, LANES), 0) % row_len
    valid = [(tpos + (j - 2) >= 0) & (tpos + (j - 2) < row_len) for j in range(DN_CONV)]
    for cbk in range(ncols // LANES):
        cols = slice(cbk * LANES, (cbk + 1) * LANES)
        xc = kvq_ref[:, cols]
        acc = xc * cw_ref[2:3, cols]
        for j in (0, 1, 3, 4):
            sh = (2 - j) % blk_len
            acc = acc + jnp.where(valid[j], pltpu.roll(xc, sh, axis=0), 0.0) * cw_ref[j:j + 1, cols]
        act_scr[:, cols] = _silu(acc)

    ab = ab_ref[...]
    g_cols = acoef_ref[...] * _softplus(ab + dtb_ref[...])
    beta_cols = _sigmoid(ab)
    g_rows = acoeft_ref[...] * _softplus(abt_ref[...] + dtbt_ref[...])

    ri = lax.broadcasted_iota(jnp.int32, (cs, cs), 0)
    ci = lax.broadcasted_iota(jnp.int32, (cs, cs), 1)
    incl = (ri <= ci) if rev else (ri >= ci)
    strict = incl & (ri != ci)
    tri = incl.astype(F32)
    eye = (ri == ci).astype(F32)
    last = 0 if rev else cs - 1
    scale = DN_HEAD_DIM ** -0.5

    order = list(range(nch - 1, -1, -1)) if rev else list(range(nch))
    units = [(c, h) for c in order for h in range(DN_HEADS)]
    nu = len(units)
    gc_cols = {}
    gc_rows = {}
    for c in order:
        rows = slice(c * cs, (c + 1) * cs)
        gc_cols[c] = jnp.dot(tri, g_cols[rows, :], precision=lax.Precision.HIGHEST, preferred_element_type=F32)
        gc_rows[c] = lax.dot_general(g_rows[:, rows], tri, (((1,), (1,)), ((), ())),
                                     precision=lax.Precision.HIGHEST, preferred_element_type=F32)

    def head_block(base, c, h):
        return act_scr[c * cs:(c + 1) * cs, base + h * DN_HEAD_DIM: base + (h + 1) * DN_HEAD_DIM]

    def l2n(t):
        return t * lax.rsqrt(jnp.sum(t * t, axis=-1, keepdims=True) + 1e-6)

    gch = [gc_cols[c][:, goff + h: goff + h + 1] for c, h in units]
    gcr = [gc_rows[c][goff + h: goff + h + 1, :] for c, h in units]
    dec = [jnp.where(incl, jnp.exp(jnp.minimum(gch[u] - gcr[u], 0.0)), 0.0) for u in range(nu)]
    bh = [beta_cols[c * cs:(c + 1) * cs, boff + h: boff + h + 1] for c, h in units]
    k = [l2n(head_block(0, c, h)) for c, h in units]
    kbf = [t.astype(BF16) for t in k]
    kb = [k[u] * bh[u] for u in range(nu)]
    kkt = [_dot_nt(kb[u].astype(BF16), kbf[u]) for u in range(nu)]
    a = [jnp.where(strict, kkt[u] * dec[u], 0.0) for u in range(nu)]
    abf = [t.astype(BF16) for t in a]
    m = [_dot(t, t) for t in abf]
    p = [eye - t for t in a]
    for lvl in range(5):
        mb = [t.astype(BF16) for t in m]
        if lvl < 4:
            r = [_dot(jnp.concatenate([m[u], p[u]], axis=0).astype(BF16), mb[u]) for u in range(nu)]
            m = [t[:cs] for t in r]
            p = [p[u] + r[u][cs:] for u in range(nu)]
        else:
            r = [_dot(p[u].astype(BF16), mb[u]) for u in range(nu)]
            p = [p[u] + r[u] for u in range(nu)]
    eg = [jnp.exp(t) for t in gch]
    rhs = [jnp.concatenate([head_block(DN_WIDTH, c, h) * bh[u], kb[u] * eg[u]], axis=-1).astype(BF16)
           for u, (c, h) in enumerate(units)]
    uw = [_dot(p[u].astype(BF16), rhs[u]) for u in range(nu)]
    gl = [t[last:last + 1, :] for t in gch]
    kd = [(k[u] * jnp.exp(gl[u] - gch[u])).astype(BF16) for u in range(nu)]
    if need_out:
        q = [l2n(head_block(2 * DN_WIDTH, c, h)) * scale for c, h in units]
        qkt = [_dot_nt(q[u].astype(BF16), kbf[u]) for u in range(nu)]
        at = [jnp.where(incl, qkt[u] * dec[u], 0.0).astype(BF16) for u in range(nu)]
        qd = [(q[u] * eg[u]).astype(BF16) for u in range(nu)]

    for ci, c in enumerate(order):
        us = range(ci * DN_HEADS, (ci + 1) * DN_HEADS)
        s_old = [s_scr[h] for h in range(DN_HEADS)]
        s_bf = [t.astype(BF16) for t in s_old]
        ws = [_dot(uw[u][:, DN_HEAD_DIM:].astype(BF16), s_bf[h]) for h, u in enumerate(us)]
        if need_out:
            qs = [_dot(qd[u], s_bf[h]) for h, u in enumerate(us)]
        vn = [(uw[u][:, :DN_HEAD_DIM] - ws[h]).astype(BF16) for h, u in enumerate(us)]
        if need_out:
            av = [_dot(at[u], vn[h]) for h, u in enumerate(us)]
        kv = [_dot_tn(kd[u], vn[h]) for h, u in enumerate(us)]
        for h, u in enumerate(us):
            if need_out:
                o_ref[c * cs:(c + 1) * cs, h * DN_HEAD_DIM:(h + 1) * DN_HEAD_DIM] = qs[h] + av[h]
            s_scr[h] = s_old[h] * jnp.exp(gl[u]) + kv[h]
    sfin_ref[...] = s_scr[...]


def _deltanet(kvq, ncols, conv_w, ab, abt, acoef, dtb, acoeft, dtbt, s0, blk_len, row_len, rev, need_out):
    bsz, t_len, _ = kvq.shape
    nb = t_len // blk_len
    bi = (lambda b, i: (b, nb - 1 - i, 0)) if rev else (lambda b, i: (b, i, 0))
    bit = (lambda b, i: (b, 0, nb - 1 - i)) if rev else (lambda b, i: (b, 0, i))
    c2 = lambda b, i: (0, 0)
    st_spec = pl.BlockSpec((None, DN_HEADS, DN_HEAD_DIM, DN_HEAD_DIM), lambda b, i: (b, 0, 0, 0))
    st_shape = jax.ShapeDtypeStruct((bsz, DN_HEADS, DN_HEAD_DIM, DN_HEAD_DIM), F32)
    out_specs = [st_spec]
    out_shape = [st_shape]
    if need_out:
        out_specs = [pl.BlockSpec((None, blk_len, DN_WIDTH), bi)] + out_specs
        out_shape = [jax.ShapeDtypeStruct((bsz, t_len, DN_WIDTH), F32)] + out_shape
    return pl.pallas_call(
        functools.partial(_dn_kernel, blk_len=blk_len, row_len=row_len, rev=rev, need_out=need_out),
        grid=(bsz, nb),
        in_specs=[pl.BlockSpec((None, blk_len, ncols), bi),
                  pl.BlockSpec((DN_CONV, ncols), c2),
                  pl.BlockSpec((None, blk_len, LANES), bi),
                  pl.BlockSpec((None, 32, blk_len), bit),
                  pl.BlockSpec((1, LANES), c2), pl.BlockSpec((1, LANES), c2),
                  pl.BlockSpec((32, 1), c2), pl.BlockSpec((32, 1), c2),
                  st_spec],
        out_specs=out_specs,
        out_shape=out_shape,
        scratch_shapes=[pltpu.VMEM((DN_HEADS, DN_HEAD_DIM, DN_HEAD_DIM), F32),
                        pltpu.VMEM((blk_len, ncols), F32)],
        compiler_params=_params(("arbitrary", "arbitrary")),
        name="deltanet_bwd" if rev else "deltanet_fwd",
    )(kvq, conv_w, ab, abt, acoef, dtb, acoeft, dtbt, s0)


def _post_kernel(x_ref, mod_ref, yf_ref, yb_ref, u_ref, of_ref, ob_ref, z_ref, g5_ref, gd_ref,
                 s5d_ref, glw_ref, glb_ref, dnw_ref, wb5_ref, wbd_ref, wo_ref, gain_ref, o_ref):
    y5 = _gelu(yf_ref[...] + yb_ref[...] + s5d_ref[...] * u_ref[...])
    y5 = y5 * _sigmoid(_dot(y5.astype(BF16), glw_ref[...]) + glb_ref[...])
    o = of_ref[...] + ob_ref[...]
    parts = []
    for h in range(DN_HEADS):
        oh = o[:, h * DN_HEAD_DIM:(h + 1) * DN_HEAD_DIM]
        parts.append(_rms(oh) * dnw_ref[...])
    y_dn = jnp.concatenate(parts, axis=-1) * _silu(z_ref[...])
    merged = (_sigmoid(g5_ref[...]) * _dot(y5.astype(BF16), wb5_ref[...])
              + _sigmoid(gd_ref[...]) * _dot(y_dn.astype(BF16), wbd_ref[...]))
    x_mix = _dot(merged.astype(BF16), wo_ref[...])
    o_ref[...] = x_ref[...] + mod_ref[2:3, :] * (_rms(x_mix) * gain_ref[...])


def _post(x, mods, yf, yb, u, o_f, o_b, main, s5d, glw, glb, dnw, wb5, wbd, wo, gain, tm):
    bsz, t_len, d = x.shape
    tok = lambda w: pl.BlockSpec((None, tm, w), lambda b, i: (b, i, 0))
    colblk = lambda k: pl.BlockSpec((None, tm, d), lambda b, i, k=k: (b, i, k))
    full = lambda a: pl.BlockSpec(a.shape, lambda b, i: (0,) * a.ndim)
    return pl.pallas_call(
        _post_kernel,
        grid=(bsz, t_len // tm),
        in_specs=[tok(d), pl.BlockSpec((None, 6, d), lambda b, i: (b, 0, 0)),
                  tok(S5_WIDTH), tok(S5_WIDTH), tok(S5_WIDTH), tok(d), tok(d),
                  colblk(3), colblk(4), colblk(5),
                  full(s5d), full(glw), full(glb), full(dnw), full(wb5), full(wbd), full(wo), full(gain)],
        out_specs=tok(d),
        out_shape=jax.ShapeDtypeStruct(x.shape, F32),
        compiler_params=_params(("arbitrary", "arbitrary")),
        name="mixer_out",
    )(x, mods, yf, yb, u, o_f, o_b, main, main, main, s5d, glw, glb, dnw, wb5, wbd, wo, gain)


_CAND_COUNTS = [PEER_TOPK // (a + 1) for a in range(PEER_TOPK)]
_N_CAND = sum(_CAND_COUNTS)
_CAND_ROWS = -(-_N_CAND // 8) * 8


def _extract_topk(work, k, out_vals_ref=None, want_rank=False):
    nrows = work.shape[0]
    iota = lax.broadcasted_iota(jnp.int32, work.shape, 0)
    rank = jnp.full(work.shape, float(k), F32) if want_rank else None
    vals = []
    for r in range(k):
        m = jnp.max(work, axis=0, keepdims=True)
        idx = jnp.min(jnp.where(work == m, iota, nrows), axis=0, keepdims=True)
        sel = iota == idx
        work = jnp.where(sel, -jnp.inf, work)
        if want_rank:
            rank = jnp.where(sel, float(r), rank)
        vals.append(m)
        if out_vals_ref is not None:
            out_vals_ref[r:r + 1, :] = m
    return vals, work, rank


def _route_kernel(x_ref, mod_ref, gain_ref, wq_ref, keys_ref,
                  ht_ref, c1_ref, e1_ref, r2_ref, p2_ref, sv_scr, cand_scr):
    h = _rms(x_ref[...]) * gain_ref[...]
    h = h * (1.0 + mod_ref[4:5, :]) + mod_ref[3:4, :]
    hb = h.astype(BF16)
    ht_ref[...] = h.T.astype(BF16)
    q = _dot(hb, wq_ref[...]).astype(BF16)
    for hd in range(PEER_HEADS):
        svals = []
        ranks = []
        scores = []
        for j in range(2):
            col = (hd * 2 + j) * PEER_KEY_DIM
            s = _dot_nt(keys_ref[hd, j], q[:, col: col + PEER_KEY_DIM])
            vals, _, rank = _extract_topk(s, PEER_TOPK, sv_scr.at[j], want_rank=True)
            scores.append(s)
            svals.append(vals)
            ranks.append(rank)
        cand_scr[...] = jnp.full(cand_scr.shape, -jnp.inf, F32)
        off = 0
        for a in range(PEER_TOPK):
            nb = _CAND_COUNTS[a]
            cand_scr[off: off + nb, :] = svals[0][a] + sv_scr[1, 0:nb, :]
            off += nb
        top, left, _ = _extract_topk(cand_scr[...], PEER_TOPK)
        m0 = top[0]
        z = jnp.exp(top[0] - m0)
        for r in range(1, PEER_TOPK):
            z = z + jnp.exp(top[r] - m0)
        taken = (left == -jnp.inf).astype(F32)
        c1 = jnp.zeros(ranks[0].shape, F32)
        off = 0
        for a in range(PEER_TOPK):
            nb = _CAND_COUNTS[a]
            cnt = jnp.sum(taken[off: off + nb, :], axis=0, keepdims=True)
            c1 = jnp.where(ranks[0] == float(a), cnt, c1)
            off += nb
        c1_ref[hd] = c1
        r2_ref[hd] = ranks[1]
        e1_ref[hd] = jnp.where(ranks[0] < float(PEER_TOPK), jnp.exp(scores[0] - svals[0][0]), 0.0)
        p2_ref[hd] = jnp.where(ranks[1] < float(PEER_TOPK), jnp.exp(scores[1] - svals[1][0]), 0.0) / z


def _route(x1, mods, gain, wq, keys, tm):
    bsz, t_len, d = x1.shape
    n_tok = bsz * t_len
    nt = t_len // tm
    tile = lambda b, i: (0, 0, b * nt + i)
    hk = pl.BlockSpec((PEER_HEADS, PEER_N_KEYS, tm), tile)
    hk_shape = jax.ShapeDtypeStruct((PEER_HEADS, PEER_N_KEYS, n_tok), F32)
    return pl.pallas_call(
        _route_kernel,
        grid=(bsz, nt),
        in_specs=[pl.BlockSpec((None, tm, d), lambda b, i: (b, i, 0)),
                  pl.BlockSpec((None, 6, d), lambda b, i: (b, 0, 0)),
                  pl.BlockSpec((1, d), lambda b, i: (0, 0)),
                  pl.BlockSpec(wq.shape, lambda b, i: (0, 0)),
                  pl.BlockSpec(keys.shape, lambda b, i: (0, 0, 0, 0))],
        out_specs=[pl.BlockSpec((d, tm), lambda b, i: (0, b * nt + i)),
                   hk, hk, hk, hk],
        out_shape=[jax.ShapeDtypeStruct((d, n_tok), BF16),
                   hk_shape, hk_shape, hk_shape, hk_shape],
        scratch_shapes=[pltpu.VMEM((2, PEER_TOPK, tm), F32),
                        pltpu.VMEM((_CAND_ROWS, tm), F32)],
        compiler_params=_params(("arbitrary", "arbitrary")),
        name="peer_route",
    )(x1, mods, gain, wq, keys)


def _peer_kernel(ht_ref, u_ref, vt_ref, c1_ref, e1_ref, r2_ref, p2_ref, x1_ref, mod_ref, gain_ref,
                 o_ref, acc, act0, act1, a0, a1, *, nsub):
    e = pl.program_id(2)
    first = (pl.program_id(0) == 0) & (pl.program_id(1) == 0) & (e == 0)
    act_scr = (act0, act1)
    a_scr = (a0, a1)

    @pl.when(first)
    def _():
        for buf in act_scr:
            buf[...] = jnp.zeros(buf.shape, F32)
        for buf in a_scr:
            buf[...] = jnp.zeros(buf.shape, BF16)

    @pl.when(e == 2)
    def _():
        acc[...] = jnp.zeros(acc.shape, F32)

    ncol = act0.shape[1] // LANES
    tr = 32
    nrt = PEER_N_KEYS // tr

    def step(cur, prev):
        act_scr[cur][...] = _dot(u_ref[...], ht_ref[...])
        acc[...] += _dot(vt_ref[...], a_scr[cur][...])
        for ct in range(ncol):
            cols = slice(ct * LANES, (ct + 1) * LANES)
            for rt in range(nrt):
                krows = slice(rt * tr, (rt + 1) * tr)
                ws = [None] * nsub
                for hd in range(PEER_HEADS):
                    r2t = r2_ref[hd, krows, cols]
                    p2t = p2_ref[hd, krows, cols]
                    for ii in range(nsub):
                        contrib = jnp.where(r2t < c1_ref[hd, ii:ii + 1, cols],
                                            p2t * e1_ref[hd, ii:ii + 1, cols], 0.0)
                        ws[ii] = contrib if ws[ii] is None else ws[ii] + contrib
                for ii in range(nsub):
                    rows = slice(ii * PEER_N_KEYS + rt * tr, ii * PEER_N_KEYS + (rt + 1) * tr)
                    a_scr[prev][rows, cols] = (_gelu(act_scr[prev][rows, cols]) * ws[ii]).astype(BF16)

    @pl.when(e % 2 == 0)
    def _():
        step(0, 1)

    @pl.when(e % 2 == 1)
    def _():
        step(1, 0)

    @pl.when(e == pl.num_programs(2) - 1)
    def _():
        out = acc[...].T
        o_ref[...] = x1_ref[...] + mod_ref[5:6, :] * (_rms(out) * gain_ref[...])


def _peer(ht, u_tab, v_tab, c1, e1, r2, p2, x1, mods, gain, tn, eb):
    bsz, t_len, d = x1.shape
    nt = t_len // tn
    n_exp = u_tab.shape[0]
    nsub = eb // PEER_N_KEYS
    ne = n_exp // eb
    n_tok = bsz * t_len
    vt_tab = v_tab.reshape(ne, eb, d).transpose(0, 2, 1)
    c1v = c1.reshape(PEER_HEADS, ne, nsub, n_tok)
    e1v = e1.reshape(PEER_HEADS, ne, nsub, n_tok)
    blk = lambda e, lag: jnp.clip(e - lag, 0, ne - 1)
    sub = pl.BlockSpec((PEER_HEADS, None, nsub, tn), lambda b, i, e: (0, blk(e, 1), 0, b * nt + i))
    hk = pl.BlockSpec((PEER_HEADS, PEER_N_KEYS, tn), lambda b, i, e: (0, 0, b * nt + i))
    return pl.pallas_call(
        functools.partial(_peer_kernel, nsub=nsub),
        grid=(bsz, nt, ne + 2),
        in_specs=[pl.BlockSpec((d, tn), lambda b, i, e: (0, b * nt + i)),
                  pl.BlockSpec((eb, d), lambda b, i, e: (blk(e, 0), 0)),
                  pl.BlockSpec((None, d, eb), lambda b, i, e: (blk(e, 2), 0, 0)),
                  sub, sub, hk, hk,
                  pl.BlockSpec((None, tn, d), lambda b, i, e: (b, i, 0)),
                  pl.BlockSpec((None, 6, d), lambda b, i, e: (b, 0, 0)),
                  pl.BlockSpec((1, d), lambda b, i, e: (0, 0))],
        out_specs=pl.BlockSpec((None, tn, d), lambda b, i, e: (b, i, 0)),
        out_shape=jax.ShapeDtypeStruct(x1.shape, F32),
        scratch_shapes=[pltpu.VMEM((d, tn), F32), pltpu.VMEM((eb, tn), F32), pltpu.VMEM((eb, tn), F32),
                        pltpu.VMEM((eb, tn), BF16), pltpu.VMEM((eb, tn), BF16)],
        compiler_params=_params(("arbitrary", "arbitrary", "arbitrary")),
        name="peer_experts",
    )(ht, u_tab, vt_tab, c1v, e1v, r2, p2, x1, mods, gain)


def _s5_coeffs(lam_re, lam_im, log_step, b_re, b_im, c_re, c_im):
    step = jnp.exp(log_step)[:, None]
    mag = jnp.exp(lam_re * step)
    ab_re, ab_im = mag * jnp.cos(lam_im * step), mag * jnp.sin(lam_im * step)
    den = lam_re * lam_re + lam_im * lam_im
    f_re = ((ab_re - 1.0) * lam_re + ab_im * lam_im) / den
    f_im = (ab_im * lam_re - (ab_re - 1.0) * lam_im) / den
    bb_re = f_re[..., None] * b_re - f_im[..., None] * b_im
    bb_im = f_re[..., None] * b_im + f_im[..., None] * b_re
    lam = jnp.stack([ab_re.reshape(-1), ab_im.reshape(-1)])
    eye8 = jnp.eye(8, dtype=F32)

    def in_blockdiag(bb):
        t = bb.reshape(4, 8, S5_STATE, S5_GROUP).transpose(0, 1, 3, 2)
        return jnp.einsum('jghp,gk->jghkp', t, eye8).reshape(4, LANES, 8 * S5_STATE)

    def out_blockdiag(cc):
        t = cc.reshape(4, 8, S5_GROUP, S5_STATE).transpose(0, 1, 3, 2)
        return jnp.einsum('jgph,gk->jgpkh', t, eye8).reshape(4, 8 * S5_STATE, LANES)

    wb = jnp.concatenate([in_blockdiag(bb_re), in_blockdiag(bb_im)], axis=-1).astype(BF16)
    cmat = jnp.stack([out_blockdiag(c_re), -out_blockdiag(c_im)], axis=1).astype(BF16)
    return wb, lam, cmat


def _gate_rows(a_log, dt_bias):
    neg_a = -jnp.exp(a_log)
    acoef = jnp.zeros((LANES,), F32).at[0:8].set(neg_a[0]).at[16:24].set(neg_a[1])
    dtb = jnp.zeros((LANES,), F32).at[0:8].set(dt_bias[0]).at[16:24].set(dt_bias[1])
    return acoef[None, :], dtb[None, :], acoef[:32, None], dtb[:32, None]


def kernel(x, c, ctx, c_ctx, w_mod, b_mod, norm_pre_mix, norm_post_mix, norm_pre_ffn, norm_post_ffn, w_in, s5_lam_re, s5_lam_im, s5_log_step, s5_b_re, s5_b_im, s5_c_re, s5_c_im, s5_d, s5_glu_w, s5_glu_b, dn_conv_w, dn_a_log, dn_dt_bias, dn_norm_w, w_branch_s5, w_branch_dn, w_out, peer_w_q, peer_sub_keys, peer_u, peer_v):
    bsz, t_len, d = x.shape
    ctx_len = ctx.shape[1]
    l = 0

    pad = (-(bsz + 1)) % 8
    cv = jnp.concatenate([c, c_ctx[None, :], jnp.zeros((pad, d), F32)], axis=0)
    mod_all = _adaln(cv, w_mod[l].astype(BF16), b_mod[l][None, :])
    mods = mod_all[:bsz].reshape(bsz, 6, d)
    mods_ctx = jnp.broadcast_to(mod_all[bsz].reshape(1, 6, d), (bsz, 6, d))

    w = w_in[l]
    w_main = jnp.concatenate([w[:, COL_K:COL_AB], w[:, COL_Q:]], axis=1).astype(BF16)
    w_ctx = w[:, COL_K:COL_AB].astype(BF16)
    w_u = w[:, COL_U:COL_K].astype(BF16)
    w_ab32 = w[:, COL_AB:COL_Q]
    w_ab = jnp.pad(w_ab32, ((0, 0), (0, LANES - 32))).astype(BF16)
    w_abt = w_ab32.T.astype(BF16)
    gain_pre = norm_pre_mix[l][None, :]

    main_c, u_c, ab_c, abt_c = _inproj(ctx, mods_ctx, gain_pre, w_ctx, w_u, w_ab, w_abt, tm=ctx_len, tn=2 * DN_WIDTH)
    main_x, u_x, ab_x, abt_x = _inproj(x, mods, gain_pre, w_main, w_u, w_ab, w_abt, tm=min(1024, t_len), tn=1536)

    wbf, lamf, cf = _s5_coeffs(s5_lam_re[l, 0], s5_lam_im[l, 0], s5_log_step[l, 0], s5_b_re[l, 0], s5_b_im[l, 0],
                               s5_c_re[l, 0], s5_c_im[l, 0])
    wbb, lamb, cb = _s5_coeffs(s5_lam_re[l, 1], s5_lam_im[l, 1], s5_log_step[l, 1], s5_b_re[l, 1], s5_b_im[l, 1],
                               s5_c_re[l, 1], s5_c_im[l, 1])
    zeros_h = jnp.zeros((bsz, 2 * S5_COLS), F32)
    tc = 32
    hf_c, hb_c = _s5(u_c.reshape(bsz, ctx_len * S5_WIDTH), wbf, wbb, lamf, lamb, cf, cb, zeros_h, zeros_h, tc, False)
    yf, yb, _, _ = _s5(u_x.reshape(bsz, t_len * S5_WIDTH), wbf, wbb, lamf, lamb, cf, cb, hf_c, hb_c, tc, True)
    yf = yf.reshape(bsz, t_len, S5_WIDTH)
    yb = yb.reshape(bsz, t_len, S5_WIDTH)

    acoef, dtb, acoeft, dtbt = _gate_rows(dn_a_log[l], dn_dt_bias[l])
    conv_w = dn_conv_w[l]
    zeros_s = jnp.zeros((bsz, DN_HEADS, DN_HEAD_DIM, DN_HEAD_DIM), F32)
    dn_args = (acoef, dtb, acoeft, dtbt)
    (sf_c,) = _deltanet(main_c, 2 * DN_WIDTH, conv_w[:, :2 * DN_WIDTH], ab_c, abt_c, *dn_args, zeros_s,
                        ctx_len, ctx_len, False, False)
    (sb_c,) = _deltanet(main_c, 2 * DN_WIDTH, conv_w[:, :2 * DN_WIDTH], ab_c, abt_c, *dn_args, zeros_s,
                        ctx_len, ctx_len, True, False)
    blk_len = min(256, t_len)
    o_f, _ = _deltanet(main_x, 3 * DN_WIDTH, conv_w, ab_x, abt_x, *dn_args, sf_c, blk_len, GRID_W, False, True)
    o_b, _ = _deltanet(main_x, 3 * DN_WIDTH, conv_w, ab_x, abt_x, *dn_args, sb_c, blk_len, GRID_W, True, True)

    x1 = _post(x, mods, yf, yb, u_x, o_f, o_b, main_x,
               s5_d[l][None, :], s5_glu_w[l].astype(BF16), s5_glu_b[l][None, :], dn_norm_w[l][None, :],
               w_branch_s5[l].astype(BF16), w_branch_dn[l].astype(BF16), w_out[l].astype(BF16),
               norm_post_mix[l][None, :], tm=min(512, t_len))

    ht, c1, e1, r2, p2 = _route(x1, mods, norm_pre_ffn[l][None, :], peer_w_q[l].astype(BF16),
                                peer_sub_keys[l].astype(BF16), tm=min(256, t_len))
    out = _peer(ht, peer_u[l].astype(BF16), peer_v[l].astype(BF16), c1, e1, r2, p2, x1, mods,
                norm_post_ffn[l][None, :], tn=min(512, t_len), eb=512)
    return out
```

```python
import functools
import math

import jax
import jax.numpy as jnp
from jax import lax
from jax.experimental import pallas as pl
from jax.experimental.pallas import tpu as pltpu

F32 = jnp.float32
BF16 = jnp.bfloat16

D_MODEL = 1024
NORM_EPS = 1e-6
GRID_W = 64
S5_WIDTH = 512
S5_GROUP = 16
S5_GROUPS = 32
S5_STATE = 64
S5_COLS = S5_GROUPS * S5_STATE
DN_HEADS = 8
DN_HEAD_DIM = 128
DN_WIDTH = 1024
DN_CONV = 5
DN_CHUNK = 64
PEER_HEADS = 8
PEER_N_KEYS = 128
PEER_KEY_DIM = 128
PEER_TOPK = 16
LANES = 128

COL_U = 0
COL_K = 512
COL_V = 1536
COL_AB = 2560
COL_Q = 2592
COL_Z = 3616
COL_G5 = 4640
COL_GD = 5664

VMEM_LIMIT = 56 * 1024 * 1024


def _dot(a, b):
    return jnp.dot(a, b, preferred_element_type=F32)


def _dot_nt(a, b):
    return lax.dot_general(a, b, (((1,), (1,)), ((), ())), preferred_element_type=F32)


def _dot_tn(a, b):
    return lax.dot_general(a, b, (((0,), (0,)), ((), ())), preferred_element_type=F32)


def _sigmoid(x):
    return jax.nn.sigmoid(x)


def _silu(x):
    return x * _sigmoid(x)


def _gelu(x):
    c = math.sqrt(2.0 / math.pi)
    return x * (0.5 * (1.0 + jnp.tanh(c * (x + 0.044715 * (x * x * x)))))


def _softplus(x):
    return jnp.maximum(x, 0.0) + jnp.log1p(jnp.exp(-jnp.abs(x)))


def _rms(x):
    return x * lax.rsqrt(jnp.mean(x * x, axis=-1, keepdims=True) + NORM_EPS)


def _params(sem):
    return pltpu.CompilerParams(dimension_semantics=sem, vmem_limit_bytes=VMEM_LIMIT)


def _adaln_kernel(c_ref, w_ref, b_ref, o_ref):
    c = c_ref[...]
    o_ref[...] = _dot(_silu(c).astype(BF16), w_ref[...]) + b_ref[...]


def _adaln(cv, w_mod, b_mod):
    rows = cv.shape[0]
    ncol = w_mod.shape[1] // D_MODEL
    return pl.pallas_call(
        _adaln_kernel,
        grid=(ncol,),
        in_specs=[pl.BlockSpec((rows, D_MODEL), lambda j: (0, 0)),
                  pl.BlockSpec((D_MODEL, D_MODEL), lambda j: (0, j)),
                  pl.BlockSpec((1, D_MODEL), lambda j: (0, j))],
        out_specs=pl.BlockSpec((rows, D_MODEL), lambda j: (0, j)),
        out_shape=jax.ShapeDtypeStruct((rows, w_mod.shape[1]), F32),
        compiler_params=_params(("arbitrary",)),
        name="adaln",
    )(cv, w_mod, b_mod)


def _inproj_kernel(x_ref, mod_ref, gain_ref, wm_ref, wu_ref, wab_ref, wabt_ref,
                   main_ref, u_ref, ab_ref, abt_ref, hx_scr):
    j = pl.program_id(2)

    @pl.when(j == 0)
    def _():
        h = _rms(x_ref[...]) * gain_ref[...]
        h = h * (1.0 + mod_ref[1:2, :]) + mod_ref[0:1, :]
        hb = h.astype(BF16)
        hx_scr[...] = hb
        u_ref[...] = _dot(hb, wu_ref[...])
        ab_ref[...] = _dot(hb, wab_ref[...])
        abt_ref[...] = _dot_nt(wabt_ref[...], hb)

    main_ref[...] = _dot(hx_scr[...], wm_ref[...])


def _inproj(x, mods, gain, w_main, w_u, w_ab, w_abt, tm, tn):
    bsz, t_len, d = x.shape
    cm = w_main.shape[1]
    grid = (bsz, t_len // tm, cm // tn)
    return pl.pallas_call(
        _inproj_kernel,
        grid=grid,
        in_specs=[pl.BlockSpec((None, tm, d), lambda b, i, j: (b, i, 0)),
                  pl.BlockSpec((None, 6, d), lambda b, i, j: (b, 0, 0)),
                  pl.BlockSpec((1, d), lambda b, i, j: (0, 0)),
                  pl.BlockSpec((d, tn), lambda b, i, j: (0, j)),
                  pl.BlockSpec((d, S5_WIDTH), lambda b, i, j: (0, 0)),
                  pl.BlockSpec((d, LANES), lambda b, i, j: (0, 0)),
                  pl.BlockSpec((32, d), lambda b, i, j: (0, 0))],
        out_specs=[pl.BlockSpec((None, tm, tn), lambda b, i, j: (b, i, j)),
                   pl.BlockSpec((None, tm, S5_WIDTH), lambda b, i, j: (b, i, 0)),
                   pl.BlockSpec((None, tm, LANES), lambda b, i, j: (b, i, 0)),
                   pl.BlockSpec((None, 32, tm), lambda b, i, j: (b, 0, i))],
        out_shape=[jax.ShapeDtypeStruct((bsz, t_len, cm), F32),
                   jax.ShapeDtypeStruct((bsz, t_len, S5_WIDTH), F32),
                   jax.ShapeDtypeStruct((bsz, t_len, LANES), F32),
                   jax.ShapeDtypeStruct((bsz, 32, t_len), F32)],
        scratch_shapes=[pltpu.VMEM((tm, d), BF16)],
        compiler_params=_params(("arbitrary", "arbitrary", "arbitrary")),
        name="inproj",
    )(x, mods, gain, w_main, w_u, w_ab, w_abt)


def _s5_kernel(uf_ref, ub_ref, wbf_ref, wbb_ref, lamf_ref, lamb_ref, cf_ref, cb_ref,
               h0f_ref, h0b_ref, *rest, tc, bsz, need_out):
    if need_out:
        yf_ref, yb_ref, hff_ref, hfb_ref, xsf, xsb, hcf, hcb = rest
    else:
        hff_ref, hfb_ref, xsf, xsb, hcf, hcb = rest
        yf_ref = yb_ref = None
    i = pl.program_id(0)
    half = S5_COLS
    cw = 512
    nblk = half // cw

    @pl.when(i == 0)
    def _():
        hcf[...] = h0f_ref[...]
        hcb[...] = h0b_ref[...]

    dirs = ((uf_ref, wbf_ref, lamf_ref, cf_ref, yf_ref, xsf, hcf, False),
            (ub_ref, wbb_ref, lamb_ref, cb_ref, yb_ref, xsb, hcb, True))
    for u_ref, wb_ref, lam_ref, c_ref, y_ref, xs, hc, rev in dirs:
        for j in range(4):
            lhs = jnp.concatenate(
                [u_ref[:, t * S5_WIDTH + j * LANES: t * S5_WIDTH + (j + 1) * LANES] for t in range(tc)],
                axis=0).astype(BF16)
            r = _dot(lhs, wb_ref[j])
            xs[:, j * cw:(j + 1) * cw] = r[:, :cw]
            xs[:, half + j * cw: half + (j + 1) * cw] = r[:, cw:]
        for cb in range(nblk):
            lo, hi = cb * cw, (cb + 1) * cw
            ar = jnp.broadcast_to(lam_ref[0:1, lo:hi], (bsz, cw))
            ai = jnp.broadcast_to(lam_ref[1:2, lo:hi], (bsz, cw))

            def body(s, carry, lo=lo, hi=hi, ar=ar, ai=ai, xs=xs, rev=rev):
                hr, hi_ = carry
                t = (tc - 1 - s) if rev else s
                row = pl.multiple_of(t * bsz, bsz)
                xr = xs[pl.ds(row, bsz), lo:hi]
                xi = xs[pl.ds(row, bsz), half + lo: half + hi]
                nr = ar * hr - ai * hi_ + xr
                ni = ar * hi_ + ai * hr + xi
                xs[pl.ds(row, bsz), lo:hi] = nr
                xs[pl.ds(row, bsz), half + lo: half + hi] = ni
                return nr, ni

            hr, hi_ = lax.fori_loop(0, tc, body, (hc[:, lo:hi], hc[:, half + lo: half + hi]))
            hc[:, lo:hi] = hr
            hc[:, half + lo: half + hi] = hi_
        if need_out:
            for j in range(4):
                hre = xs[:, j * cw:(j + 1) * cw].astype(BF16)
                him = xs[:, half + j * cw: half + (j + 1) * cw].astype(BF16)
                y = _dot(hre, c_ref[j, 0]) + _dot(him, c_ref[j, 1])
                for t in range(tc):
                    y_ref[:, t * S5_WIDTH + j * LANES: t * S5_WIDTH + (j + 1) * LANES] = y[t * bsz:(t + 1) * bsz]
    hff_ref[...] = hcf[...]
    hfb_ref[...] = hcb[...]


def _s5(u2, wbf, wbb, lamf, lamb, cf, cb, h0f, h0b, tc, need_out):
    bsz = u2.shape[0]
    t_len = u2.shape[1] // S5_WIDTH
    nc = t_len // tc
    blk = tc * S5_WIDTH
    const2 = lambda i: (0, 0)
    const3 = lambda i: (0, 0, 0)
    const4 = lambda i: (0, 0, 0, 0)
    st_spec = pl.BlockSpec((bsz, 2 * S5_COLS), const2)
    st_shape = jax.ShapeDtypeStruct((bsz, 2 * S5_COLS), F32)
    out_specs = [st_spec, st_spec]
    out_shape = [st_shape, st_shape]
    if need_out:
        y_shape = jax.ShapeDtypeStruct(u2.shape, F32)
        out_specs = [pl.BlockSpec((bsz, blk), lambda i: (0, i)),
                     pl.BlockSpec((bsz, blk), lambda i: (0, nc - 1 - i))] + out_specs
        out_shape = [y_shape, y_shape] + out_shape
    return pl.pallas_call(
        functools.partial(_s5_kernel, tc=tc, bsz=bsz, need_out=need_out),
        grid=(nc,),
        in_specs=[pl.BlockSpec((bsz, blk), lambda i: (0, i)),
                  pl.BlockSpec((bsz, blk), lambda i: (0, nc - 1 - i)),
                  pl.BlockSpec(wbf.shape, const3), pl.BlockSpec(wbb.shape, const3),
                  pl.BlockSpec(lamf.shape, const2), pl.BlockSpec(lamb.shape, const2),
                  pl.BlockSpec(cf.shape, const4), pl.BlockSpec(cb.shape, const4),
                  st_spec, st_spec],
        out_specs=out_specs,
        out_shape=out_shape,
        scratch_shapes=[pltpu.VMEM((tc * bsz, 2 * S5_COLS), F32),
                        pltpu.VMEM((tc * bsz, 2 * S5_COLS), F32),
                        pltpu.VMEM((bsz, 2 * S5_COLS), F32),
                        pltpu.VMEM((bsz, 2 * S5_COLS), F32)],
        compiler_params=_params(("arbitrary",)),
        name="s5_scan",
    )(u2, u2, wbf, wbb, lamf, lamb, cf, cb, h0f, h0b)


def _dn_kernel(kvq_ref, cw_ref, ab_ref, abt_ref, acoef_ref, dtb_ref, acoeft_ref, dtbt_ref, s0_ref,
               *rest, blk_len, row_len, rev, need_out):
    if need_out:
        o_ref, sfin_ref, s_scr, act_scr = rest
    else:
        sfin_ref, s_scr, act_scr = rest
        o_ref = None
    ncols = act_scr.shape[1]
    cs = DN_CHUNK
    nch = blk_len // cs
    goff = 16 if rev else 0
    boff = goff + 8

    @pl.when(pl.program_id(1) == 0)
    def _():
        s_scr[...] = s0_ref[...]

    tpos = lax.broadcasted_iota(jnp.int32, (blk_len, LANES), 0) % row_len
    valid = [(tpos + (j - 2) >= 0) & (tpos + (j - 2) < row_len) for j in range(DN_CONV)]
    for cbk in range(ncols // LANES):
        cols = slice(cbk * LANES, (cbk + 1) * LANES)
        xc = kvq_ref[:, cols]
        acc = xc * cw_ref[2:3, cols]
        for j in (0, 1, 3, 4):
            sh = (2 - j) % blk_len
            acc = acc + jnp.where(valid[j], pltpu.roll(xc, sh, axis=0), 0.0) * cw_ref[j:j + 1, cols]
        act_scr[:, cols] = _silu(acc)

    ab = ab_ref[...]
    g_cols = acoef_ref[...] * _softplus(ab + dtb_ref[...])
    beta_cols = _sigmoid(ab)
    g_rows = acoeft_ref[...] * _softplus(abt_ref[...] + dtbt_ref[...])

    ri = lax.broadcasted_iota(jnp.int32, (cs, cs), 0)
    ci = lax.broadcasted_iota(jnp.int32, (cs, cs), 1)
    incl = (ri <= ci) if rev else (ri >= ci)
    strict = incl & (ri != ci)
    tri = incl.astype(F32)
    eye = (ri == ci).astype(F32)
    last = 0 if rev else cs - 1
    scale = DN_HEAD_DIM ** -0.5

    order = list(range(nch - 1, -1, -1)) if rev else list(range(nch))
    units = [(c, h) for c in order for h in range(DN_HEADS)]
    nu = len(units)
    gc_cols = {}
    gc_rows = {}
    for c in order:
        rows = slice(c * cs, (c + 1) * cs)
        gc_cols[c] = jnp.dot(tri, g_cols[rows, :], precision=lax.Precision.HIGHEST, preferred_element_type=F32)
        gc_rows[c] = lax.dot_general(g_rows[:, rows], tri, (((1,), (1,)), ((), ())),
                                     precision=lax.Precision.HIGHEST, preferred_element_type=F32)

    def head_block(base, c, h):
        return act_scr[c * cs:(c + 1) * cs, base + h * DN_HEAD_DIM: base + (h + 1) * DN_HEAD_DIM]

    def l2n(t):
        return t * lax.rsqrt(jnp.sum(t * t, axis=-1, keepdims=True) + 1e-6)

    gch = [gc_cols[c][:, goff + h: goff + h + 1] for c, h in units]
    gcr = [gc_rows[c][goff + h: goff + h + 1, :] for c, h in units]
    dec = [jnp.where(incl, jnp.exp(jnp.minimum(gch[u] - gcr[u], 0.0)), 0.0) for u in range(nu)]
    bh = [beta_cols[c * cs:(c + 1) * cs, boff + h: boff + h + 1] for c, h in units]
    k = [l2n(head_block(0, c, h)) for c, h in units]
    kbf = [t.astype(BF16) for t in k]
    kb = [k[u] * bh[u] for u in range(nu)]
    kkt = [_dot_nt(kb[u].astype(BF16), kbf[u]) for u in range(nu)]
    a = [jnp.where(strict, kkt[u] * dec[u], 0.0) for u in range(nu)]
    abf = [t.astype(BF16) for t in a]
    m = [_dot(t, t) for t in abf]
    p = [eye - t for t in a]
    for lvl in range(5):
        mb = [t.astype(BF16) for t in m]
        if lvl < 4:
            r = [_dot(jnp.concatenate([m[u], p[u]], axis=0).astype(BF16), mb[u]) for u in range(nu)]
            m = [t[:cs] for t in r]
            p = [p[u] + r[u][cs:] for u in range(nu)]
        else:
            r = [_dot(p[u].astype(BF16), mb[u]) for u in range(nu)]
            p = [p[u] + r[u] for u in range(nu)]
    eg = [jnp.exp(t) for t in gch]
    rhs = [jnp.concatenate([head_block(DN_WIDTH, c, h) * bh[u], kb[u] * eg[u]], axis=-1).astype(BF16)
           for u, (c, h) in enumerate(units)]
    uw = [_dot(p[u].astype(BF16), rhs[u]) for u in range(nu)]
    gl = [t[last:last + 1, :] for t in gch]
    kd = [(k[u] * jnp.exp(gl[u] - gch[u])).astype(BF16) for u in range(nu)]
    if need_out:
        q = [l2n(head_block(2 * DN_WIDTH, c, h)) * scale for c, h in units]
        qkt = [_dot_nt(q[u].astype(BF16), kbf[u]) for u in range(nu)]
        at = [jnp.where(incl, qkt[u] * dec[u], 0.0).astype(BF16) for u in range(nu)]
        qd = [(q[u] * eg[u]).astype(BF16) for u in range(nu)]

    for ci, c in enumerate(order):
        us = range(ci * DN_HEADS, (ci + 1) * DN_HEADS)
        s_old = [s_scr[h] for h in range(DN_HEADS)]
        s_bf = [t.astype(BF16) for t in s_old]
        ws = [_dot(uw[u][:, DN_HEAD_DIM:].astype(BF16), s_bf[h]) for h, u in enumerate(us)]
        if need_out:
            qs = [_dot(qd[u], s_bf[h]) for h, u in enumerate(us)]
        vn = [(uw[u][:, :DN_HEAD_DIM] - ws[h]).astype(BF16) for h, u in enumerate(us)]
        if need_out:
            av = [_dot(at[u], vn[h]) for h, u in enumerate(us)]
        kv = [_dot_tn(kd[u], vn[h]) for h, u in enumerate(us)]
        for h, u in enumerate(us):
            if need_out:
                o_ref[c * cs:(c + 1) * cs, h * DN_HEAD_DIM:(h + 1) * DN_HEAD_DIM] = qs[h] + av[h]
            s_scr[h] = s_old[h] * jnp.exp(gl[u]) + kv[h]
    sfin_ref[...] = s_scr[...]


def _deltanet(kvq, ncols, conv_w, ab, abt, acoef, dtb, acoeft, dtbt, s0, blk_len, row_len, rev, need_out):
    bsz, t_len, _ = kvq.shape
    nb = t_len // blk_len
    bi = (lambda b, i: (b, nb - 1 - i, 0)) if rev else (lambda b, i: (b, i, 0))
    bit = (lambda b, i: (b, 0, nb - 1 - i)) if rev else (lambda b, i: (b, 0, i))
    c2 = lambda b, i: (0, 0)
    st_spec = pl.BlockSpec((None, DN_HEADS, DN_HEAD_DIM, DN_HEAD_DIM), lambda b, i: (b, 0, 0, 0))
    st_shape = jax.ShapeDtypeStruct((bsz, DN_HEADS, DN_HEAD_DIM, DN_HEAD_DIM), F32)
    out_specs = [st_spec]
    out_shape = [st_shape]
    if need_out:
        out_specs = [pl.BlockSpec((None, blk_len, DN_WIDTH), bi)] + out_specs
        out_shape = [jax.ShapeDtypeStruct((bsz, t_len, DN_WIDTH), F32)] + out_shape
    return pl.pallas_call(
        functools.partial(_dn_kernel, blk_len=blk_len, row_len=row_len, rev=rev, need_out=need_out),
        grid=(bsz, nb),
        in_specs=[pl.BlockSpec((None, blk_len, ncols), bi),
                  pl.BlockSpec((DN_CONV, ncols), c2),
                  pl.BlockSpec((None, blk_len, LANES), bi),
                  pl.BlockSpec((None, 32, blk_len), bit),
                  pl.BlockSpec((1, LANES), c2), pl.BlockSpec((1, LANES), c2),
                  pl.BlockSpec((32, 1), c2), pl.BlockSpec((32, 1), c2),
                  st_spec],
        out_specs=out_specs,
        out_shape=out_shape,
        scratch_shapes=[pltpu.VMEM((DN_HEADS, DN_HEAD_DIM, DN_HEAD_DIM), F32),
                        pltpu.VMEM((blk_len, ncols), F32)],
        compiler_params=_params(("arbitrary", "arbitrary")),
        name="deltanet_bwd" if rev else "deltanet_fwd",
    )(kvq, conv_w, ab, abt, acoef, dtb, acoeft, dtbt, s0)


def _post_kernel(x_ref, mod_ref, yf_ref, yb_ref, u_ref, of_ref, ob_ref, z_ref, g5_ref, gd_ref,
                 s5d_ref, glw_ref, glb_ref, dnw_ref, wb5_ref, wbd_ref, wo_ref, gain_ref, o_ref):
    y5 = _gelu(yf_ref[...] + yb_ref[...] + s5d_ref[...] * u_ref[...])
    y5 = y5 * _sigmoid(_dot(y5.astype(BF16), glw_ref[...]) + glb_ref[...])
    o = of_ref[...] + ob_ref[...]
    parts = []
    for h in range(DN_HEADS):
        oh = o[:, h * DN_HEAD_DIM:(h + 1) * DN_HEAD_DIM]
        parts.append(_rms(oh) * dnw_ref[...])
    y_dn = jnp.concatenate(parts, axis=-1) * _silu(z_ref[...])
    merged = (_sigmoid(g5_ref[...]) * _dot(y5.astype(BF16), wb5_ref[...])
              + _sigmoid(gd_ref[...]) * _dot(y_dn.astype(BF16), wbd_ref[...]))
    x_mix = _dot(merged.astype(BF16), wo_ref[...])
    o_ref[...] = x_ref[...] + mod_ref[2:3, :] * (_rms(x_mix) * gain_ref[...])


def _post(x, mods, yf, yb, u, o_f, o_b, main, s5d, glw, glb, dnw, wb5, wbd, wo, gain, tm):
    bsz, t_len, d = x.shape
    tok = lambda w: pl.BlockSpec((None, tm, w), lambda b, i: (b, i, 0))
    colblk = lambda k: pl.BlockSpec((None, tm, d), lambda b, i, k=k: (b, i, k))
    full = lambda a: pl.BlockSpec(a.shape, lambda b, i: (0,) * a.ndim)
    return pl.pallas_call(
        _post_kernel,
        grid=(bsz, t_len // tm),
        in_specs=[tok(d), pl.BlockSpec((None, 6, d), lambda b, i: (b, 0, 0)),
                  tok(S5_WIDTH), tok(S5_WIDTH), tok(S5_WIDTH), tok(d), tok(d),
                  colblk(3), colblk(4), colblk(5),
                  full(s5d), full(glw), full(glb), full(dnw), full(wb5), full(wbd), full(wo), full(gain)],
        out_specs=tok(d),
        out_shape=jax.ShapeDtypeStruct(x.shape, F32),
        compiler_params=_params(("arbitrary", "arbitrary")),
        name="mixer_out",
    )(x, mods, yf, yb, u, o_f, o_b, main, main, main, s5d, glw, glb, dnw, wb5, wbd, wo, gain)


_CAND_COUNTS = [PEER_TOPK // (a + 1) for a in range(PEER_TOPK)]
_N_CAND = sum(_CAND_COUNTS)
_CAND_ROWS = -(-_N_CAND // 8) * 8


def _extract_topk(work, k, out_vals_ref=None, want_rank=False):
    nrows = work.shape[0]
    iota = lax.broadcasted_iota(jnp.int32, work.shape, 0)
    rank = jnp.full(work.shape, float(k), F32) if want_rank else None
    vals = []
    for r in range(k):
        m = jnp.max(work, axis=0, keepdims=True)
        idx = jnp.min(jnp.where(work == m, iota, nrows), axis=0, keepdims=True)
        sel = iota == idx
        work = jnp.where(sel, -jnp.inf, work)
        if want_rank:
            rank = jnp.where(sel, float(r), rank)
        vals.append(m)
        if out_vals_ref is not None:
            out_vals_ref[r:r + 1, :] = m
    return vals, work, rank


def _route_kernel(x_ref, mod_ref, gain_ref, wq_ref, keys_ref,
                  ht_ref, c1_ref, e1_ref, r2_ref, p2_ref, sv_scr, cand_scr):
    h = _rms(x_ref[...]) * gain_ref[...]
    h = h * (1.0 + mod_ref[4:5, :]) + mod_ref[3:4, :]
    hb = h.astype(BF16)
    ht_ref[...] = h.T.astype(BF16)
    q = _dot(hb, wq_ref[...]).astype(BF16)
    for hd in range(PEER_HEADS):
        svals = []
        ranks = []
        scores = []
        for j in range(2):
            col = (hd * 2 + j) * PEER_KEY_DIM
            s = _dot_nt(keys_ref[hd, j], q[:, col: col + PEER_KEY_DIM])
            vals, _, rank = _extract_topk(s, PEER_TOPK, sv_scr.at[j], want_rank=True)
            scores.append(s)
            svals.append(vals)
            ranks.append(rank)
        cand_scr[...] = jnp.full(cand_scr.shape, -jnp.inf, F32)
        off = 0
        for a in range(PEER_TOPK):
            nb = _CAND_COUNTS[a]
            cand_scr[off: off + nb, :] = svals[0][a] + sv_scr[1, 0:nb, :]
            off += nb
        top, left, _ = _extract_topk(cand_scr[...], PEER_TOPK)
        m0 = top[0]
        z = jnp.exp(top[0] - m0)
        for r in range(1, PEER_TOPK):
            z = z + jnp.exp(top[r] - m0)
        taken = (left == -jnp.inf).astype(F32)
        c1 = jnp.zeros(ranks[0].shape, F32)
        off = 0
        for a in range(PEER_TOPK):
            nb = _CAND_COUNTS[a]
            cnt = jnp.sum(taken[off: off + nb, :], axis=0, keepdims=True)
            c1 = jnp.where(ranks[0] == float(a), cnt, c1)
            off += nb
        c1_ref[hd] = c1
        r2_ref[hd] = ranks[1]
        e1_ref[hd] = jnp.where(ranks[0] < float(PEER_TOPK), jnp.exp(scores[0] - svals[0][0]), 0.0)
        p2_ref[hd] = jnp.where(ranks[1] < float(PEER_TOPK), jnp.exp(scores[1] - svals[1][0]), 0.0) / z


def _route(x1, mods, gain, wq, keys, tm):
    bsz, t_len, d = x1.shape
    n_tok = bsz * t_len
    nt = t_len // tm
    tile = lambda b, i: (0, 0, b * nt + i)
    hk = pl.BlockSpec((PEER_HEADS, PEER_N_KEYS, tm), tile)
    hk_shape = jax.ShapeDtypeStruct((PEER_HEADS, PEER_N_KEYS, n_tok), F32)
    return pl.pallas_call(
        _route_kernel,
        grid=(bsz, nt),
        in_specs=[pl.BlockSpec((None, tm, d), lambda b, i: (b, i, 0)),
                  pl.BlockSpec((None, 6, d), lambda b, i: (b, 0, 0)),
                  pl.BlockSpec((1, d), lambda b, i: (0, 0)),
                  pl.BlockSpec(wq.shape, lambda b, i: (0, 0)),
                  pl.BlockSpec(keys.shape, lambda b, i: (0, 0, 0, 0))],
        out_specs=[pl.BlockSpec((d, tm), lambda b, i: (0, b * nt + i)),
                   hk, hk, hk, hk],
        out_shape=[jax.ShapeDtypeStruct((d, n_tok), BF16),
                   hk_shape, hk_shape, hk_shape, hk_shape],
        scratch_shapes=[pltpu.VMEM((2, PEER_TOPK, tm), F32),
                        pltpu.VMEM((_CAND_ROWS, tm), F32)],
        compiler_params=_params(("arbitrary", "arbitrary")),
        name="peer_route",
    )(x1, mods, gain, wq, keys)


def _peer_kernel(ht_ref, u_ref, vt_ref, c1_ref, e1_ref, r2_ref, p2_ref, x1_ref, mod_ref, gain_ref,
                 o_ref, acc, act_scr, a_scr, *, nsub):
    e = pl.program_id(2)

    @pl.when(e == 0)
    def _():
        acc[...] = jnp.zeros(acc.shape, F32)

    ncol = act_scr.shape[1] // LANES
    tr = 32
    nrt = PEER_N_KEYS // tr

    act_scr[...] = _dot(u_ref[...], ht_ref[...])
    for ct in range(ncol):
        cols = slice(ct * LANES, (ct + 1) * LANES)
        for rt in range(nrt):
            krows = slice(rt * tr, (rt + 1) * tr)
            ws = [None] * nsub
            for hd in range(PEER_HEADS):
                r2t = r2_ref[hd, krows, cols]
                p2t = p2_ref[hd, krows, cols]
                for ii in range(nsub):
                    contrib = jnp.where(r2t < c1_ref[hd, ii:ii + 1, cols], p2t * e1_ref[hd, ii:ii + 1, cols], 0.0)
                    ws[ii] = contrib if ws[ii] is None else ws[ii] + contrib
            for ii in range(nsub):
                rows = slice(ii * PEER_N_KEYS + rt * tr, ii * PEER_N_KEYS + (rt + 1) * tr)
                a_scr[rows, cols] = (_gelu(act_scr[rows, cols]) * ws[ii]).astype(BF16)
    acc[...] += _dot(vt_ref[...], a_scr[...])

    @pl.when(e == pl.num_programs(2) - 1)
    def _():
        out = acc[...].T
        o_ref[...] = x1_ref[...] + mod_ref[5:6, :] * (_rms(out) * gain_ref[...])


def _peer(ht, u_tab, v_tab, c1, e1, r2, p2, x1, mods, gain, tn, eb):
    bsz, t_len, d = x1.shape
    nt = t_len // tn
    n_exp = u_tab.shape[0]
    nsub = eb // PEER_N_KEYS
    ne = n_exp // eb
    n_tok = bsz * t_len
    vt_tab = v_tab.reshape(ne, eb, d).transpose(0, 2, 1)
    c1v = c1.reshape(PEER_HEADS, ne, nsub, n_tok)
    e1v = e1.reshape(PEER_HEADS, ne, nsub, n_tok)
    sub = pl.BlockSpec((PEER_HEADS, None, nsub, tn), lambda b, i, e: (0, e, 0, b * nt + i))
    hk = pl.BlockSpec((PEER_HEADS, PEER_N_KEYS, tn), lambda b, i, e: (0, 0, b * nt + i))
    return pl.pallas_call(
        functools.partial(_peer_kernel, nsub=nsub),
        grid=(bsz, nt, ne),
        in_specs=[pl.BlockSpec((d, tn), lambda b, i, e: (0, b * nt + i)),
                  pl.BlockSpec((eb, d), lambda b, i, e: (e, 0)),
                  pl.BlockSpec((None, d, eb), lambda b, i, e: (e, 0, 0)),
                  sub, sub, hk, hk,
                  pl.BlockSpec((None, tn, d), lambda b, i, e: (b, i, 0)),
                  pl.BlockSpec((None, 6, d), lambda b, i, e: (b, 0, 0)),
                  pl.BlockSpec((1, d), lambda b, i, e: (0, 0))],
        out_specs=pl.BlockSpec((None, tn, d), lambda b, i, e: (b, i, 0)),
        out_shape=jax.ShapeDtypeStruct(x1.shape, F32),
        scratch_shapes=[pltpu.VMEM((d, tn), F32), pltpu.VMEM((eb, tn), F32), pltpu.VMEM((eb, tn), BF16)],
        compiler_params=_params(("arbitrary", "arbitrary", "arbitrary")),
        name="peer_experts",
    )(ht, u_tab, vt_tab, c1v, e1v, r2, p2, x1, mods, gain)


def _s5_coeffs(lam_re, lam_im, log_step, b_re, b_im, c_re, c_im):
    step = jnp.exp(log_step)[:, None]
    mag = jnp.exp(lam_re * step)
    ab_re, ab_im = mag * jnp.cos(lam_im * step), mag * jnp.sin(lam_im * step)
    den = lam_re * lam_re + lam_im * lam_im
    f_re = ((ab_re - 1.0) * lam_re + ab_im * lam_im) / den
    f_im = (ab_im * lam_re - (ab_re - 1.0) * lam_im) / den
    bb_re = f_re[..., None] * b_re - f_im[..., None] * b_im
    bb_im = f_re[..., None] * b_im + f_im[..., None] * b_re
    lam = jnp.stack([ab_re.reshape(-1), ab_im.reshape(-1)])
    eye8 = jnp.eye(8, dtype=F32)

    def in_blockdiag(bb):
        t = bb.reshape(4, 8, S5_STATE, S5_GROUP).transpose(0, 1, 3, 2)
        return jnp.einsum('jghp,gk->jghkp', t, eye8).reshape(4, LANES, 8 * S5_STATE)

    def out_blockdiag(cc):
        t = cc.reshape(4, 8, S5_GROUP, S5_STATE).transpose(0, 1, 3, 2)
        return jnp.einsum('jgph,gk->jgpkh', t, eye8).reshape(4, 8 * S5_STATE, LANES)

    wb = jnp.concatenate([in_blockdiag(bb_re), in_blockdiag(bb_im)], axis=-1).astype(BF16)
    cmat = jnp.stack([out_blockdiag(c_re), -out_blockdiag(c_im)], axis=1).astype(BF16)
    return wb, lam, cmat


def _gate_rows(a_log, dt_bias):
    neg_a = -jnp.exp(a_log)
    acoef = jnp.zeros((LANES,), F32).at[0:8].set(neg_a[0]).at[16:24].set(neg_a[1])
    dtb = jnp.zeros((LANES,), F32).at[0:8].set(dt_bias[0]).at[16:24].set(dt_bias[1])
    return acoef[None, :], dtb[None, :], acoef[:32, None], dtb[:32, None]


def kernel(x, c, ctx, c_ctx, w_mod, b_mod, norm_pre_mix, norm_post_mix, norm_pre_ffn, norm_post_ffn, w_in, s5_lam_re, s5_lam_im, s5_log_step, s5_b_re, s5_b_im, s5_c_re, s5_c_im, s5_d, s5_glu_w, s5_glu_b, dn_conv_w, dn_a_log, dn_dt_bias, dn_norm_w, w_branch_s5, w_branch_dn, w_out, peer_w_q, peer_sub_keys, peer_u, peer_v):
    bsz, t_len, d = x.shape
    ctx_len = ctx.shape[1]
    l = 0

    pad = (-(bsz + 1)) % 8
    cv = jnp.concatenate([c, c_ctx[None, :], jnp.zeros((pad, d), F32)], axis=0)
    mod_all = _adaln(cv, w_mod[l].astype(BF16), b_mod[l][None, :])
    mods = mod_all[:bsz].reshape(bsz, 6, d)
    mods_ctx = jnp.broadcast_to(mod_all[bsz].reshape(1, 6, d), (bsz, 6, d))

    w = w_in[l]
    w_main = jnp.concatenate([w[:, COL_K:COL_AB], w[:, COL_Q:]], axis=1).astype(BF16)
    w_ctx = w[:, COL_K:COL_AB].astype(BF16)
    w_u = w[:, COL_U:COL_K].astype(BF16)
    w_ab32 = w[:, COL_AB:COL_Q]
    w_ab = jnp.pad(w_ab32, ((0, 0), (0, LANES - 32))).astype(BF16)
    w_abt = w_ab32.T.astype(BF16)
    gain_pre = norm_pre_mix[l][None, :]

    main_c, u_c, ab_c, abt_c = _inproj(ctx, mods_ctx, gain_pre, w_ctx, w_u, w_ab, w_abt, tm=ctx_len, tn=2 * DN_WIDTH)
    main_x, u_x, ab_x, abt_x = _inproj(x, mods, gain_pre, w_main, w_u, w_ab, w_abt, tm=min(1024, t_len), tn=1536)

    wbf, lamf, cf = _s5_coeffs(s5_lam_re[l, 0], s5_lam_im[l, 0], s5_log_step[l, 0], s5_b_re[l, 0], s5_b_im[l, 0],
                               s5_c_re[l, 0], s5_c_im[l, 0])
    wbb, lamb, cb = _s5_coeffs(s5_lam_re[l, 1], s5_lam_im[l, 1], s5_log_step[l, 1], s5_b_re[l, 1], s5_b_im[l, 1],
                               s5_c_re[l, 1], s5_c_im[l, 1])
    zeros_h = jnp.zeros((bsz, 2 * S5_COLS), F32)
    tc = 32
    hf_c, hb_c = _s5(u_c.reshape(bsz, ctx_len * S5_WIDTH), wbf, wbb, lamf, lamb, cf, cb, zeros_h, zeros_h, tc, False)
    yf, yb, _, _ = _s5(u_x.reshape(bsz, t_len * S5_WIDTH), wbf, wbb, lamf, lamb, cf, cb, hf_c, hb_c, tc, True)
    yf = yf.reshape(bsz, t_len, S5_WIDTH)
    yb = yb.reshape(bsz, t_len, S5_WIDTH)

    acoef, dtb, acoeft, dtbt = _gate_rows(dn_a_log[l], dn_dt_bias[l])
    conv_w = dn_conv_w[l]
    zeros_s = jnp.zeros((bsz, DN_HEADS, DN_HEAD_DIM, DN_HEAD_DIM), F32)
    dn_args = (acoef, dtb, acoeft, dtbt)
    (sf_c,) = _deltanet(main_c, 2 * DN_WIDTH, conv_w[:, :2 * DN_WIDTH], ab_c, abt_c, *dn_args, zeros_s,
                        ctx_len, ctx_len, False, False)
    (sb_c,) = _deltanet(main_c, 2 * DN_WIDTH, conv_w[:, :2 * DN_WIDTH], ab_c, abt_c, *dn_args, zeros_s,
                        ctx_len, ctx_len, True, False)
    blk_len = min(256, t_len)
    o_f, _ = _deltanet(main_x, 3 * DN_WIDTH, conv_w, ab_x, abt_x, *dn_args, sf_c, blk_len, GRID_W, False, True)
    o_b, _ = _deltanet(main_x, 3 * DN_WIDTH, conv_w, ab_x, abt_x, *dn_args, sb_c, blk_len, GRID_W, True, True)

    x1 = _post(x, mods, yf, yb, u_x, o_f, o_b, main_x,
               s5_d[l][None, :], s5_glu_w[l].astype(BF16), s5_glu_b[l][None, :], dn_norm_w[l][None, :],
               w_branch_s5[l].astype(BF16), w_branch_dn[l].astype(BF16), w_out[l].astype(BF16),
               norm_post_mix[l][None, :], tm=min(512, t_len))

    ht, c1, e1, r2, p2 = _route(x1, mods, norm_pre_ffn[l][None, :], peer_w_q[l].astype(BF16),
                                peer_sub_keys[l].astype(BF16), tm=min(256, t_len))
    out = _peer(ht, peer_u[l].astype(BF16), peer_v[l].astype(BF16), c1, e1, r2, p2, x1, mods,
                norm_post_ffn[l][None, :], tn=min(1024, t_len), eb=512)
    return out
```

```python
import functools
import math

import jax
import jax.numpy as jnp
from jax import lax
from jax.experimental import pallas as pl
from jax.experimental.pallas import tpu as pltpu

F32 = jnp.float32
BF16 = jnp.bfloat16

D_MODEL = 1024
NORM_EPS = 1e-6
GRID_W = 64
S5_WIDTH = 512
S5_GROUP = 16
S5_GROUPS = 32
S5_STATE = 64
S5_COLS = S5_GROUPS * S5_STATE
DN_HEADS = 8
DN_HEAD_DIM = 128
DN_WIDTH = 1024
DN_CONV = 5
DN_CHUNK = 64
PEER_HEADS = 8
PEER_N_KEYS = 128
PEER_KEY_DIM = 128
PEER_TOPK = 16
LANES = 128

COL_U = 0
COL_K = 512
COL_V = 1536
COL_AB = 2560
COL_Q = 2592
COL_Z = 3616
COL_G5 = 4640
COL_GD = 5664

VMEM_LIMIT = 56 * 1024 * 1024


def _dot(a, b):
    return jnp.dot(a, b, preferred_element_type=F32)


def _dot_nt(a, b):
    return lax.dot_general(a, b, (((1,), (1,)), ((), ())), preferred_element_type=F32)


def _dot_tn(a, b):
    return lax.dot_general(a, b, (((0,), (0,)), ((), ())), preferred_element_type=F32)


def _sigmoid(x):
    return jax.nn.sigmoid(x)


def _silu(x):
    return x * _sigmoid(x)


def _gelu(x):
    c = math.sqrt(2.0 / math.pi)
    return x * (0.5 * (1.0 + jnp.tanh(c * (x + 0.044715 * (x * x * x)))))


def _softplus(x):
    return jnp.maximum(x, 0.0) + jnp.log1p(jnp.exp(-jnp.abs(x)))


def _rms(x):
    return x * lax.rsqrt(jnp.mean(x * x, axis=-1, keepdims=True) + NORM_EPS)


def _params(sem):
    return pltpu.CompilerParams(dimension_semantics=sem, vmem_limit_bytes=VMEM_LIMIT)


def _adaln_kernel(c_ref, w_ref, b_ref, o_ref):
    c = c_ref[...]
    o_ref[...] = _dot(_silu(c).astype(BF16), w_ref[...]) + b_ref[...]


def _adaln(cv, w_mod, b_mod):
    rows = cv.shape[0]
    ncol = w_mod.shape[1] // D_MODEL
    return pl.pallas_call(
        _adaln_kernel,
        grid=(ncol,),
        in_specs=[pl.BlockSpec((rows, D_MODEL), lambda j: (0, 0)),
                  pl.BlockSpec((D_MODEL, D_MODEL), lambda j: (0, j)),
                  pl.BlockSpec((1, D_MODEL), lambda j: (0, j))],
        out_specs=pl.BlockSpec((rows, D_MODEL), lambda j: (0, j)),
        out_shape=jax.ShapeDtypeStruct((rows, w_mod.shape[1]), F32),
        compiler_params=_params(("arbitrary",)),
        name="adaln",
    )(cv, w_mod, b_mod)


def _inproj_kernel(x_ref, mod_ref, gain_ref, wc_ref, cw_ref, *rest, ncs, row_len, has_gates):
    if has_gates:
        wg_ref, wu_ref, wab_ref, wabt_ref, kvq_ref, gates_ref, u_ref, ab_ref, abt_ref, hx_scr = rest
    else:
        wu_ref, wab_ref, wabt_ref, kvq_ref, u_ref, ab_ref, abt_ref, hx_scr = rest
    j = pl.program_id(2)

    @pl.when(j == 0)
    def _():
        h = _rms(x_ref[...]) * gain_ref[...]
        h = h * (1.0 + mod_ref[1:2, :]) + mod_ref[0:1, :]
        hb = h.astype(BF16)
        hx_scr[...] = hb
        u_ref[...] = _dot(hb, wu_ref[...])
        ab_ref[...] = _dot(hb, wab_ref[...])
        abt_ref[...] = _dot_nt(wabt_ref[...], hb)

    @pl.when(j < ncs)
    def _():
        kvq_ref[...] = _dot(hx_scr[...], wc_ref[...])
        tm, tn = kvq_ref.shape
        tpos = lax.broadcasted_iota(jnp.int32, (tm, LANES), 0) % row_len
        valid = [(tpos + (t - 2) >= 0) & (tpos + (t - 2) < row_len) for t in range(DN_CONV)]
        for cbk in range(tn // LANES):
            cols = slice(cbk * LANES, (cbk + 1) * LANES)
            xc = kvq_ref[:, cols]
            acc = xc * cw_ref[2:3, cols]
            for t in (0, 1, 3, 4):
                acc = acc + jnp.where(valid[t], pltpu.roll(xc, (2 - t) % tm, axis=0), 0.0) * cw_ref[t:t + 1, cols]
            kvq_ref[:, cols] = _silu(acc)

    if has_gates:
        @pl.when(j >= ncs)
        def _():
            gates_ref[...] = _dot(hx_scr[...], wg_ref[...])


def _inproj(x, mods, gain, w_conv, conv_w, w_gate, w_u, w_ab, w_abt, tm, tn, row_len):
    bsz, t_len, d = x.shape
    cc = w_conv.shape[1]
    ncs = cc // tn
    has_gates = w_gate is not None
    ngs = (w_gate.shape[1] // tn) if has_gates else 0
    cstep = lambda j: jnp.minimum(j, ncs - 1)
    gstep = lambda j: jnp.maximum(j - ncs, 0)
    in_specs = [pl.BlockSpec((None, tm, d), lambda b, i, j: (b, i, 0)),
                pl.BlockSpec((None, 6, d), lambda b, i, j: (b, 0, 0)),
                pl.BlockSpec((1, d), lambda b, i, j: (0, 0)),
                pl.BlockSpec((d, tn), lambda b, i, j: (0, cstep(j))),
                pl.BlockSpec((DN_CONV, tn), lambda b, i, j: (0, cstep(j)))]
    args = [x, mods, gain, w_conv, conv_w]
    out_specs = [pl.BlockSpec((None, tm, tn), lambda b, i, j: (b, i, cstep(j)))]
    out_shape = [jax.ShapeDtypeStruct((bsz, t_len, cc), F32)]
    if has_gates:
        in_specs.append(pl.BlockSpec((d, tn), lambda b, i, j: (0, gstep(j))))
        args.append(w_gate)
        out_specs.append(pl.BlockSpec((None, tm, tn), lambda b, i, j: (b, i, gstep(j))))
        out_shape.append(jax.ShapeDtypeStruct((bsz, t_len, w_gate.shape[1]), F32))
    in_specs += [pl.BlockSpec((d, S5_WIDTH), lambda b, i, j: (0, 0)),
                 pl.BlockSpec((d, LANES), lambda b, i, j: (0, 0)),
                 pl.BlockSpec((32, d), lambda b, i, j: (0, 0))]
    args += [w_u, w_ab, w_abt]
    out_specs += [pl.BlockSpec((None, tm, S5_WIDTH), lambda b, i, j: (b, i, 0)),
                  pl.BlockSpec((None, tm, LANES), lambda b, i, j: (b, i, 0)),
                  pl.BlockSpec((None, 32, tm), lambda b, i, j: (b, 0, i))]
    out_shape += [jax.ShapeDtypeStruct((bsz, t_len, S5_WIDTH), F32),
                  jax.ShapeDtypeStruct((bsz, t_len, LANES), F32),
                  jax.ShapeDtypeStruct((bsz, 32, t_len), F32)]
    outs = pl.pallas_call(
        functools.partial(_inproj_kernel, ncs=ncs, row_len=row_len, has_gates=has_gates),
        grid=(bsz, t_len // tm, ncs + ngs),
        in_specs=in_specs,
        out_specs=out_specs,
        out_shape=out_shape,
        scratch_shapes=[pltpu.VMEM((tm, d), BF16)],
        compiler_params=_params(("arbitrary", "arbitrary", "arbitrary")),
        name="inproj",
    )(*args)
    if has_gates:
        return outs
    return (outs[0], None) + tuple(outs[1:])


def _s5_kernel(uf_ref, ub_ref, wbf_ref, wbb_ref, lamf_ref, lamb_ref, cf_ref, cb_ref,
               h0f_ref, h0b_ref, *rest, tc, bsz, need_out):
    if need_out:
        yf_ref, yb_ref, hff_ref, hfb_ref, xsf, xsb, hcf, hcb = rest
    else:
        hff_ref, hfb_ref, xsf, xsb, hcf, hcb = rest
        yf_ref = yb_ref = None
    i = pl.program_id(0)
    half = S5_COLS
    cw = 512
    nblk = half // cw

    @pl.when(i == 0)
    def _():
        hcf[...] = h0f_ref[...]
        hcb[...] = h0b_ref[...]

    dirs = ((uf_ref, wbf_ref, lamf_ref, cf_ref, yf_ref, xsf, hcf, False),
            (ub_ref, wbb_ref, lamb_ref, cb_ref, yb_ref, xsb, hcb, True))
    for u_ref, wb_ref, lam_ref, c_ref, y_ref, xs, hc, rev in dirs:
        for j in range(4):
            lhs = jnp.concatenate(
                [u_ref[:, t * S5_WIDTH + j * LANES: t * S5_WIDTH + (j + 1) * LANES] for t in range(tc)],
                axis=0).astype(BF16)
            r = _dot(lhs, wb_ref[j])
            xs[:, j * cw:(j + 1) * cw] = r[:, :cw]
            xs[:, half + j * cw: half + (j + 1) * cw] = r[:, cw:]
        for cb in range(nblk):
            lo, hi = cb * cw, (cb + 1) * cw
            ar = jnp.broadcast_to(lam_ref[0:1, lo:hi], (bsz, cw))
            ai = jnp.broadcast_to(lam_ref[1:2, lo:hi], (bsz, cw))

            def body(s, carry, lo=lo, hi=hi, ar=ar, ai=ai, xs=xs, rev=rev):
                hr, hi_ = carry
                t = (tc - 1 - s) if rev else s
                row = pl.multiple_of(t * bsz, bsz)
                xr = xs[pl.ds(row, bsz), lo:hi]
                xi = xs[pl.ds(row, bsz), half + lo: half + hi]
                nr = ar * hr - ai * hi_ + xr
                ni = ar * hi_ + ai * hr + xi
                xs[pl.ds(row, bsz), lo:hi] = nr
                xs[pl.ds(row, bsz), half + lo: half + hi] = ni
                return nr, ni

            hr, hi_ = lax.fori_loop(0, tc, body, (hc[:, lo:hi], hc[:, half + lo: half + hi]))
            hc[:, lo:hi] = hr
            hc[:, half + lo: half + hi] = hi_
        if need_out:
            for j in range(4):
                hre = xs[:, j * cw:(j + 1) * cw].astype(BF16)
                him = xs[:, half + j * cw: half + (j + 1) * cw].astype(BF16)
                y = _dot(hre, c_ref[j, 0]) + _dot(him, c_ref[j, 1])
                for t in range(tc):
                    y_ref[:, t * S5_WIDTH + j * LANES: t * S5_WIDTH + (j + 1) * LANES] = y[t * bsz:(t + 1) * bsz]
    hff_ref[...] = hcf[...]
    hfb_ref[...] = hcb[...]


def _s5(u2, wbf, wbb, lamf, lamb, cf, cb, h0f, h0b, tc, need_out):
    bsz = u2.shape[0]
    t_len = u2.shape[1] // S5_WIDTH
    nc = t_len // tc
    blk = tc * S5_WIDTH
    const2 = lambda i: (0, 0)
    const3 = lambda i: (0, 0, 0)
    const4 = lambda i: (0, 0, 0, 0)
    st_spec = pl.BlockSpec((bsz, 2 * S5_COLS), const2)
    st_shape = jax.ShapeDtypeStruct((bsz, 2 * S5_COLS), F32)
    out_specs = [st_spec, st_spec]
    out_shape = [st_shape, st_shape]
    if need_out:
        y_shape = jax.ShapeDtypeStruct(u2.shape, F32)
        out_specs = [pl.BlockSpec((bsz, blk), lambda i: (0, i)),
                     pl.BlockSpec((bsz, blk), lambda i: (0, nc - 1 - i))] + out_specs
        out_shape = [y_shape, y_shape] + out_shape
    return pl.pallas_call(
        functools.partial(_s5_kernel, tc=tc, bsz=bsz, need_out=need_out),
        grid=(nc,),
        in_specs=[pl.BlockSpec((bsz, blk), lambda i: (0, i)),
                  pl.BlockSpec((bsz, blk), lambda i: (0, nc - 1 - i)),
                  pl.BlockSpec(wbf.shape, const3), pl.BlockSpec(wbb.shape, const3),
                  pl.BlockSpec(lamf.shape, const2), pl.BlockSpec(lamb.shape, const2),
                  pl.BlockSpec(cf.shape, const4), pl.BlockSpec(cb.shape, const4),
                  st_spec, st_spec],
        out_specs=out_specs,
        out_shape=out_shape,
        scratch_shapes=[pltpu.VMEM((tc * bsz, 2 * S5_COLS), F32),
                        pltpu.VMEM((tc * bsz, 2 * S5_COLS), F32),
                        pltpu.VMEM((bsz, 2 * S5_COLS), F32),
                        pltpu.VMEM((bsz, 2 * S5_COLS), F32)],
        compiler_params=_params(("arbitrary",)),
        name="s5_scan",
    )(u2, u2, wbf, wbb, lamf, lamb, cf, cb, h0f, h0b)


def _dn_kernel(act_scr, ab_ref, abt_ref, acoef_ref, dtb_ref, acoeft_ref, dtbt_ref, s0_ref,
               *rest, blk_len, rev, need_out):
    if need_out:
        o_ref, sfin_ref, s_scr = rest
    else:
        sfin_ref, s_scr = rest
        o_ref = None
    cs = DN_CHUNK
    nch = blk_len // cs
    goff = 16 if rev else 0
    boff = goff + 8

    @pl.when(pl.program_id(1) == 0)
    def _():
        s_scr[...] = s0_ref[...]

    ab = ab_ref[...]
    g_cols = acoef_ref[...] * _softplus(ab + dtb_ref[...])
    beta_cols = _sigmoid(ab)
    g_rows = acoeft_ref[...] * _softplus(abt_ref[...] + dtbt_ref[...])

    ri = lax.broadcasted_iota(jnp.int32, (cs, cs), 0)
    ci = lax.broadcasted_iota(jnp.int32, (cs, cs), 1)
    incl = (ri <= ci) if rev else (ri >= ci)
    strict = incl & (ri != ci)
    tri = incl.astype(F32)
    eye = (ri == ci).astype(F32)
    last = 0 if rev else cs - 1
    scale = DN_HEAD_DIM ** -0.5

    order = list(range(nch - 1, -1, -1)) if rev else list(range(nch))
    units = [(c, h) for c in order for h in range(DN_HEADS)]
    nu = len(units)
    gc_cols = {}
    gc_rows = {}
    for c in order:
        rows = slice(c * cs, (c + 1) * cs)
        gc_cols[c] = jnp.dot(tri, g_cols[rows, :], precision=lax.Precision.HIGHEST, preferred_element_type=F32)
        gc_rows[c] = lax.dot_general(g_rows[:, rows], tri, (((1,), (1,)), ((), ())),
                                     precision=lax.Precision.HIGHEST, preferred_element_type=F32)

    def head_block(base, c, h):
        return act_scr[c * cs:(c + 1) * cs, base + h * DN_HEAD_DIM: base + (h + 1) * DN_HEAD_DIM]

    def l2n(t):
        return t * lax.rsqrt(jnp.sum(t * t, axis=-1, keepdims=True) + 1e-6)

    gch = [gc_cols[c][:, goff + h: goff + h + 1] for c, h in units]
    gcr = [gc_rows[c][goff + h: goff + h + 1, :] for c, h in units]
    dec = [jnp.where(incl, jnp.exp(jnp.minimum(gch[u] - gcr[u], 0.0)), 0.0) for u in range(nu)]
    bh = [beta_cols[c * cs:(c + 1) * cs, boff + h: boff + h + 1] for c, h in units]
    k = [l2n(head_block(0, c, h)) for c, h in units]
    kbf = [t.astype(BF16) for t in k]
    kb = [k[u] * bh[u] for u in range(nu)]
    kkt = [_dot_nt(kb[u].astype(BF16), kbf[u]) for u in range(nu)]
    a = [jnp.where(strict, kkt[u] * dec[u], 0.0) for u in range(nu)]
    abf = [t.astype(BF16) for t in a]
    m = [_dot(t, t) for t in abf]
    p = [eye - t for t in a]
    for lvl in range(5):
        mb = [t.astype(BF16) for t in m]
        if lvl < 4:
            r = [_dot(jnp.concatenate([m[u], p[u]], axis=0).astype(BF16), mb[u]) for u in range(nu)]
            m = [t[:cs] for t in r]
            p = [p[u] + r[u][cs:] for u in range(nu)]
        else:
            r = [_dot(p[u].astype(BF16), mb[u]) for u in range(nu)]
            p = [p[u] + r[u] for u in range(nu)]
    eg = [jnp.exp(t) for t in gch]
    rhs = [jnp.concatenate([head_block(DN_WIDTH, c, h) * bh[u], kb[u] * eg[u]], axis=-1).astype(BF16)
           for u, (c, h) in enumerate(units)]
    uw = [_dot(p[u].astype(BF16), rhs[u]) for u in range(nu)]
    gl = [t[last:last + 1, :] for t in gch]
    kd = [(k[u] * jnp.exp(gl[u] - gch[u])).astype(BF16) for u in range(nu)]
    if need_out:
        q = [l2n(head_block(2 * DN_WIDTH, c, h)) * scale for c, h in units]
        qkt = [_dot_nt(q[u].astype(BF16), kbf[u]) for u in range(nu)]
        at = [jnp.where(incl, qkt[u] * dec[u], 0.0).astype(BF16) for u in range(nu)]
        qd = [(q[u] * eg[u]).astype(BF16) for u in range(nu)]

    for ci, c in enumerate(order):
        us = range(ci * DN_HEADS, (ci + 1) * DN_HEADS)
        s_old = [s_scr[h] for h in range(DN_HEADS)]
        s_bf = [t.astype(BF16) for t in s_old]
        ws = [_dot(uw[u][:, DN_HEAD_DIM:].astype(BF16), s_bf[h]) for h, u in enumerate(us)]
        if need_out:
            qs = [_dot(qd[u], s_bf[h]) for h, u in enumerate(us)]
        vn = [(uw[u][:, :DN_HEAD_DIM] - ws[h]).astype(BF16) for h, u in enumerate(us)]
        if need_out:
            av = [_dot(at[u], vn[h]) for h, u in enumerate(us)]
        kv = [_dot_tn(kd[u], vn[h]) for h, u in enumerate(us)]
        for h, u in enumerate(us):
            if need_out:
                o_ref[c * cs:(c + 1) * cs, h * DN_HEAD_DIM:(h + 1) * DN_HEAD_DIM] = qs[h] + av[h]
            s_scr[h] = s_old[h] * jnp.exp(gl[u]) + kv[h]
    sfin_ref[...] = s_scr[...]


def _deltanet(kvq, ab, abt, acoef, dtb, acoeft, dtbt, s0, blk_len, rev, need_out):
    bsz, t_len, ncols = kvq.shape
    nb = t_len // blk_len
    bi = (lambda b, i: (b, nb - 1 - i, 0)) if rev else (lambda b, i: (b, i, 0))
    bit = (lambda b, i: (b, 0, nb - 1 - i)) if rev else (lambda b, i: (b, 0, i))
    c2 = lambda b, i: (0, 0)
    st_spec = pl.BlockSpec((None, DN_HEADS, DN_HEAD_DIM, DN_HEAD_DIM), lambda b, i: (b, 0, 0, 0))
    st_shape = jax.ShapeDtypeStruct((bsz, DN_HEADS, DN_HEAD_DIM, DN_HEAD_DIM), F32)
    out_specs = [st_spec]
    out_shape = [st_shape]
    if need_out:
        out_specs = [pl.BlockSpec((None, blk_len, DN_WIDTH), bi)] + out_specs
        out_shape = [jax.ShapeDtypeStruct((bsz, t_len, DN_WIDTH), F32)] + out_shape
    return pl.pallas_call(
        functools.partial(_dn_kernel, blk_len=blk_len, rev=rev, need_out=need_out),
        grid=(bsz, nb),
        in_specs=[pl.BlockSpec((None, blk_len, ncols), bi),
                  pl.BlockSpec((None, blk_len, LANES), bi),
                  pl.BlockSpec((None, 32, blk_len), bit),
                  pl.BlockSpec((1, LANES), c2), pl.BlockSpec((1, LANES), c2),
                  pl.BlockSpec((32, 1), c2), pl.BlockSpec((32, 1), c2),
                  st_spec],
        out_specs=out_specs,
        out_shape=out_shape,
        scratch_shapes=[pltpu.VMEM((DN_HEADS, DN_HEAD_DIM, DN_HEAD_DIM), F32)],
        compiler_params=_params(("arbitrary", "arbitrary")),
        name="deltanet_bwd" if rev else "deltanet_fwd",
    )(kvq, ab, abt, acoef, dtb, acoeft, dtbt, s0)


def _post_kernel(x_ref, mod_ref, yf_ref, yb_ref, u_ref, of_ref, ob_ref, z_ref, g5_ref, gd_ref,
                 s5d_ref, glw_ref, glb_ref, dnw_ref, wb5_ref, wbd_ref, wo_ref, gain_ref, o_ref):
    y5 = _gelu(yf_ref[...] + yb_ref[...] + s5d_ref[...] * u_ref[...])
    y5 = y5 * _sigmoid(_dot(y5.astype(BF16), glw_ref[...]) + glb_ref[...])
    o = of_ref[...] + ob_ref[...]
    parts = []
    for h in range(DN_HEADS):
        oh = o[:, h * DN_HEAD_DIM:(h + 1) * DN_HEAD_DIM]
        parts.append(_rms(oh) * dnw_ref[...])
    y_dn = jnp.concatenate(parts, axis=-1) * _silu(z_ref[...])
    merged = (_sigmoid(g5_ref[...]) * _dot(y5.astype(BF16), wb5_ref[...])
              + _sigmoid(gd_ref[...]) * _dot(y_dn.astype(BF16), wbd_ref[...]))
    x_mix = _dot(merged.astype(BF16), wo_ref[...])
    o_ref[...] = x_ref[...] + mod_ref[2:3, :] * (_rms(x_mix) * gain_ref[...])


def _post(x, mods, yf, yb, u, o_f, o_b, gates, s5d, glw, glb, dnw, wb5, wbd, wo, gain, tm):
    bsz, t_len, d = x.shape
    tok = lambda w: pl.BlockSpec((None, tm, w), lambda b, i: (b, i, 0))
    colblk = lambda k: pl.BlockSpec((None, tm, d), lambda b, i, k=k: (b, i, k))
    full = lambda a: pl.BlockSpec(a.shape, lambda b, i: (0,) * a.ndim)
    return pl.pallas_call(
        _post_kernel,
        grid=(bsz, t_len // tm),
        in_specs=[tok(d), pl.BlockSpec((None, 6, d), lambda b, i: (b, 0, 0)),
                  tok(S5_WIDTH), tok(S5_WIDTH), tok(S5_WIDTH), tok(d), tok(d),
                  colblk(0), colblk(1), colblk(2),
                  full(s5d), full(glw), full(glb), full(dnw), full(wb5), full(wbd), full(wo), full(gain)],
        out_specs=tok(d),
        out_shape=jax.ShapeDtypeStruct(x.shape, F32),
        compiler_params=_params(("arbitrary", "arbitrary")),
        name="mixer_out",
    )(x, mods, yf, yb, u, o_f, o_b, gates, gates, gates, s5d, glw, glb, dnw, wb5, wbd, wo, gain)


_CAND_COUNTS = [PEER_TOPK // (a + 1) for a in range(PEER_TOPK)]
_N_CAND = sum(_CAND_COUNTS)
_CAND_ROWS = -(-_N_CAND // 8) * 8


def _extract_topk(work, k, out_vals_ref=None, want_rank=False):
    nrows = work.shape[0]
    iota = lax.broadcasted_iota(jnp.int32, work.shape, 0)
    rank = jnp.full(work.shape, float(k), F32) if want_rank else None
    vals = []
    for r in range(k):
        m = jnp.max(work, axis=0, keepdims=True)
        idx = jnp.min(jnp.where(work == m, iota, nrows), axis=0, keepdims=True)
        sel = iota == idx
        work = jnp.where(sel, -jnp.inf, work)
        if want_rank:
            rank = jnp.where(sel, float(r), rank)
        vals.append(m)
        if out_vals_ref is not None:
            out_vals_ref[r:r + 1, :] = m
    return vals, work, rank


def _route_kernel(x_ref, mod_ref, gain_ref, wq_ref, keys_ref,
                  ht_ref, c1_ref, e1_ref, r2_ref, p2_ref, sv_scr, cand_scr):
    h = _rms(x_ref[...]) * gain_ref[...]
    h = h * (1.0 + mod_ref[4:5, :]) + mod_ref[3:4, :]
    hb = h.astype(BF16)
    ht_ref[...] = h.T.astype(BF16)
    q = _dot(hb, wq_ref[...]).astype(BF16)
    for hd in range(PEER_HEADS):
        svals = []
        ranks = []
        scores = []
        for j in range(2):
            col = (hd * 2 + j) * PEER_KEY_DIM
            s = _dot_nt(keys_ref[hd, j], q[:, col: col + PEER_KEY_DIM])
            vals, _, rank = _extract_topk(s, PEER_TOPK, sv_scr.at[j], want_rank=True)
            scores.append(s)
            svals.append(vals)
            ranks.append(rank)
        cand_scr[...] = jnp.full(cand_scr.shape, -jnp.inf, F32)
        off = 0
        for a in range(PEER_TOPK):
            nb = _CAND_COUNTS[a]
            cand_scr[off: off + nb, :] = svals[0][a] + sv_scr[1, 0:nb, :]
            off += nb
        top, left, _ = _extract_topk(cand_scr[...], PEER_TOPK)
        m0 = top[0]
        z = jnp.exp(top[0] - m0)
        for r in range(1, PEER_TOPK):
            z = z + jnp.exp(top[r] - m0)
        taken = (left == -jnp.inf).astype(F32)
        c1 = jnp.zeros(ranks[0].shape, F32)
        off = 0
        for a in range(PEER_TOPK):
            nb = _CAND_COUNTS[a]
            cnt = jnp.sum(taken[off: off + nb, :], axis=0, keepdims=True)
            c1 = jnp.where(ranks[0] == float(a), cnt, c1)
            off += nb
        c1_ref[hd] = c1
        r2_ref[hd] = ranks[1]
        e1_ref[hd] = jnp.where(ranks[0] < float(PEER_TOPK), jnp.exp(scores[0] - svals[0][0]), 0.0)
        p2_ref[hd] = jnp.where(ranks[1] < float(PEER_TOPK), jnp.exp(scores[1] - svals[1][0]), 0.0) / z


def _route(x1, mods, gain, wq, keys, tm):
    bsz, t_len, d = x1.shape
    n_tok = bsz * t_len
    nt = t_len // tm
    tile = lambda b, i: (0, 0, b * nt + i)
    hk = pl.BlockSpec((PEER_HEADS, PEER_N_KEYS, tm), tile)
    hk_shape = jax.ShapeDtypeStruct((PEER_HEADS, PEER_N_KEYS, n_tok), F32)
    return pl.pallas_call(
        _route_kernel,
        grid=(bsz, nt),
        in_specs=[pl.BlockSpec((None, tm, d), lambda b, i: (b, i, 0)),
                  pl.BlockSpec((None, 6, d), lambda b, i: (b, 0, 0)),
                  pl.BlockSpec((1, d), lambda b, i: (0, 0)),
                  pl.BlockSpec(wq.shape, lambda b, i: (0, 0)),
                  pl.BlockSpec(keys.shape, lambda b, i: (0, 0, 0, 0))],
        out_specs=[pl.BlockSpec((d, tm), lambda b, i: (0, b * nt + i)),
                   hk, hk, hk, hk],
        out_shape=[jax.ShapeDtypeStruct((d, n_tok), BF16),
                   hk_shape, hk_shape, hk_shape, hk_shape],
        scratch_shapes=[pltpu.VMEM((2, PEER_TOPK, tm), F32),
                        pltpu.VMEM((_CAND_ROWS, tm), F32)],
        compiler_params=_params(("arbitrary", "arbitrary")),
        name="peer_route",
    )(x1, mods, gain, wq, keys)


def _peer_kernel(ht_ref, u_ref, vt_ref, c1_ref, e1_ref, r2_ref, p2_ref, x1_ref, mod_ref, gain_ref,
                 o_ref, acc, act_scr, a_scr, *, nsub):
    e = pl.program_id(2)

    @pl.when(e == 0)
    def _():
        acc[...] = jnp.zeros(acc.shape, F32)

    ncol = act_scr.shape[1] // LANES
    tr = 32
    nrt = PEER_N_KEYS // tr

    act_scr[...] = _dot(u_ref[...], ht_ref[...])
    for ct in range(ncol):
        cols = slice(ct * LANES, (ct + 1) * LANES)
        for rt in range(nrt):
            krows = slice(rt * tr, (rt + 1) * tr)
            ws = [None] * nsub
            for hd in range(PEER_HEADS):
                r2t = r2_ref[hd, krows, cols]
                p2t = p2_ref[hd, krows, cols]
                for ii in range(nsub):
                    contrib = jnp.where(r2t < c1_ref[hd, ii:ii + 1, cols], p2t * e1_ref[hd, ii:ii + 1, cols], 0.0)
                    ws[ii] = contrib if ws[ii] is None else ws[ii] + contrib
            for ii in range(nsub):
                rows = slice(ii * PEER_N_KEYS + rt * tr, ii * PEER_N_KEYS + (rt + 1) * tr)
                a_scr[rows, cols] = (_gelu(act_scr[rows, cols]) * ws[ii]).astype(BF16)
    acc[...] += _dot(vt_ref[...], a_scr[...])

    @pl.when(e == pl.num_programs(2) - 1)
    def _():
        out = acc[...].T
        o_ref[...] = x1_ref[...] + mod_ref[5:6, :] * (_rms(out) * gain_ref[...])


def _peer(ht, u_tab, v_tab, c1, e1, r2, p2, x1, mods, gain, tn, eb):
    bsz, t_len, d = x1.shape
    nt = t_len // tn
    n_exp = u_tab.shape[0]
    nsub = eb // PEER_N_KEYS
    ne = n_exp // eb
    n_tok = bsz * t_len
    vt_tab = v_tab.reshape(ne, eb, d).transpose(0, 2, 1)
    c1v = c1.reshape(PEER_HEADS, ne, nsub, n_tok)
    e1v = e1.reshape(PEER_HEADS, ne, nsub, n_tok)
    sub = pl.BlockSpec((PEER_HEADS, None, nsub, tn), lambda b, i, e: (0, e, 0, b * nt + i))
    hk = pl.BlockSpec((PEER_HEADS, PEER_N_KEYS, tn), lambda b, i, e: (0, 0, b * nt + i))
    return pl.pallas_call(
        functools.partial(_peer_kernel, nsub=nsub),
        grid=(bsz, nt, ne),
        in_specs=[pl.BlockSpec((d, tn), lambda b, i, e: (0, b * nt + i)),
                  pl.BlockSpec((eb, d), lambda b, i, e: (e, 0)),
                  pl.BlockSpec((None, d, eb), lambda b, i, e: (e, 0, 0)),
                  sub, sub, hk, hk,
                  pl.BlockSpec((None, tn, d), lambda b, i, e: (b, i, 0)),
                  pl.BlockSpec((None, 6, d), lambda b, i, e: (b, 0, 0)),
                  pl.BlockSpec((1, d), lambda b, i, e: (0, 0))],
        out_specs=pl.BlockSpec((None, tn, d), lambda b, i, e: (b, i, 0)),
        out_shape=jax.ShapeDtypeStruct(x1.shape, F32),
        scratch_shapes=[pltpu.VMEM((d, tn), F32), pltpu.VMEM((eb, tn), F32), pltpu.VMEM((eb, tn), BF16)],
        compiler_params=_params(("arbitrary", "arbitrary", "arbitrary")),
        name="peer_experts",
    )(ht, u_tab, vt_tab, c1v, e1v, r2, p2, x1, mods, gain)


def _s5_coeffs(lam_re, lam_im, log_step, b_re, b_im, c_re, c_im):
    step = jnp.exp(log_step)[:, None]
    mag = jnp.exp(lam_re * step)
    ab_re, ab_im = mag * jnp.cos(lam_im * step), mag * jnp.sin(lam_im * step)
    den = lam_re * lam_re + lam_im * lam_im
    f_re = ((ab_re - 1.0) * lam_re + ab_im * lam_im) / den
    f_im = (ab_im * lam_re - (ab_re - 1.0) * lam_im) / den
    bb_re = f_re[..., None] * b_re - f_im[..., None] * b_im
    bb_im = f_re[..., None] * b_im + f_im[..., None] * b_re
    lam = jnp.stack([ab_re.reshape(-1), ab_im.reshape(-1)])
    eye8 = jnp.eye(8, dtype=F32)

    def in_blockdiag(bb):
        t = bb.reshape(4, 8, S5_STATE, S5_GROUP).transpose(0, 1, 3, 2)
        return jnp.einsum('jghp,gk->jghkp', t, eye8).reshape(4, LANES, 8 * S5_STATE)

    def out_blockdiag(cc):
        t = cc.reshape(4, 8, S5_GROUP, S5_STATE).transpose(0, 1, 3, 2)
        return jnp.einsum('jgph,gk->jgpkh', t, eye8).reshape(4, 8 * S5_STATE, LANES)

    wb = jnp.concatenate([in_blockdiag(bb_re), in_blockdiag(bb_im)], axis=-1).astype(BF16)
    cmat = jnp.stack([out_blockdiag(c_re), -out_blockdiag(c_im)], axis=1).astype(BF16)
    return wb, lam, cmat


def _gate_rows(a_log, dt_bias):
    neg_a = -jnp.exp(a_log)
    acoef = jnp.zeros((LANES,), F32).at[0:8].set(neg_a[0]).at[16:24].set(neg_a[1])
    dtb = jnp.zeros((LANES,), F32).at[0:8].set(dt_bias[0]).at[16:24].set(dt_bias[1])
    return acoef[None, :], dtb[None, :], acoef[:32, None], dtb[:32, None]


def kernel(x, c, ctx, c_ctx, w_mod, b_mod, norm_pre_mix, norm_post_mix, norm_pre_ffn, norm_post_ffn, w_in, s5_lam_re, s5_lam_im, s5_log_step, s5_b_re, s5_b_im, s5_c_re, s5_c_im, s5_d, s5_glu_w, s5_glu_b, dn_conv_w, dn_a_log, dn_dt_bias, dn_norm_w, w_branch_s5, w_branch_dn, w_out, peer_w_q, peer_sub_keys, peer_u, peer_v):
    bsz, t_len, d = x.shape
    ctx_len = ctx.shape[1]
    l = 0

    pad = (-(bsz + 1)) % 8
    cv = jnp.concatenate([c, c_ctx[None, :], jnp.zeros((pad, d), F32)], axis=0)
    mod_all = _adaln(cv, w_mod[l].astype(BF16), b_mod[l][None, :])
    mods = mod_all[:bsz].reshape(bsz, 6, d)
    mods_ctx = jnp.broadcast_to(mod_all[bsz].reshape(1, 6, d), (bsz, 6, d))

    w = w_in[l]
    w_kvq = jnp.concatenate([w[:, COL_K:COL_AB], w[:, COL_Q:COL_Z]], axis=1).astype(BF16)
    w_gates = w[:, COL_Z:].astype(BF16)
    w_ctx = w[:, COL_K:COL_AB].astype(BF16)
    w_u = w[:, COL_U:COL_K].astype(BF16)
    w_ab32 = w[:, COL_AB:COL_Q]
    w_ab = jnp.pad(w_ab32, ((0, 0), (0, LANES - 32))).astype(BF16)
    w_abt = w_ab32.T.astype(BF16)
    gain_pre = norm_pre_mix[l][None, :]
    conv_w = dn_conv_w[l]

    kv_c, _, u_c, ab_c, abt_c = _inproj(ctx, mods_ctx, gain_pre, w_ctx, conv_w[:, :2 * DN_WIDTH], None,
                                        w_u, w_ab, w_abt, tm=ctx_len, tn=2 * DN_WIDTH, row_len=ctx_len)
    kvq_x, gates_x, u_x, ab_x, abt_x = _inproj(x, mods, gain_pre, w_kvq, conv_w, w_gates, w_u, w_ab, w_abt,
                                               tm=min(1024, t_len), tn=1536, row_len=GRID_W)

    wbf, lamf, cf = _s5_coeffs(s5_lam_re[l, 0], s5_lam_im[l, 0], s5_log_step[l, 0], s5_b_re[l, 0], s5_b_im[l, 0],
                               s5_c_re[l, 0], s5_c_im[l, 0])
    wbb, lamb, cb = _s5_coeffs(s5_lam_re[l, 1], s5_lam_im[l, 1], s5_log_step[l, 1], s5_b_re[l, 1], s5_b_im[l, 1],
                               s5_c_re[l, 1], s5_c_im[l, 1])
    zeros_h = jnp.zeros((bsz, 2 * S5_COLS), F32)
    tc = 32
    hf_c, hb_c = _s5(u_c.reshape(bsz, ctx_len * S5_WIDTH), wbf, wbb, lamf, lamb, cf, cb, zeros_h, zeros_h, tc, False)
    yf, yb, _, _ = _s5(u_x.reshape(bsz, t_len * S5_WIDTH), wbf, wbb, lamf, lamb, cf, cb, hf_c, hb_c, tc, True)
    yf = yf.reshape(bsz, t_len, S5_WIDTH)
    yb = yb.reshape(bsz, t_len, S5_WIDTH)

    acoef, dtb, acoeft, dtbt = _gate_rows(dn_a_log[l], dn_dt_bias[l])
    zeros_s = jnp.zeros((bsz, DN_HEADS, DN_HEAD_DIM, DN_HEAD_DIM), F32)
    dn_args = (acoef, dtb, acoeft, dtbt)
    (sf_c,) = _deltanet(kv_c, ab_c, abt_c, *dn_args, zeros_s, ctx_len, False, False)
    (sb_c,) = _deltanet(kv_c, ab_c, abt_c, *dn_args, zeros_s, ctx_len, True, False)
    blk_len = min(256, t_len)
    o_f, _ = _deltanet(kvq_x, ab_x, abt_x, *dn_args, sf_c, blk_len, False, True)
    o_b, _ = _deltanet(kvq_x, ab_x, abt_x, *dn_args, sb_c, blk_len, True, True)

    x1 = _post(x, mods, yf, yb, u_x, o_f, o_b, gates_x,
               s5_d[l][None, :], s5_glu_w[l].astype(BF16), s5_glu_b[l][None, :], dn_norm_w[l][None, :],
               w_branch_s5[l].astype(BF16), w_branch_dn[l].astype(BF16), w_out[l].astype(BF16),
               norm_post_mix[l][None, :], tm=min(512, t_len))

    ht, c1, e1, r2, p2 = _route(x1, mods, norm_pre_ffn[l][None, :], peer_w_q[l].astype(BF16),
                                peer_sub_keys[l].astype(BF16), tm=min(256, t_len))
    out = _peer(ht, peer_u[l].astype(BF16), peer_v[l].astype(BF16), c1, e1, r2, p2, x1, mods,
                norm_post_ffn[l][None, :], tn=min(1024, t_len), eb=512)
    return out
```

```python
import functools
import math

import jax
import jax.numpy as jnp
from jax import lax
from jax.experimental import pallas as pl
from jax.experimental.pallas import tpu as pltpu

F32 = jnp.float32
BF16 = jnp.bfloat16

D_MODEL = 1024
NORM_EPS = 1e-6
GRID_W = 64
S5_WIDTH = 512
S5_GROUP = 16
S5_GROUPS = 32
S5_STATE = 64
S5_COLS = S5_GROUPS * S5_STATE
DN_HEADS = 8
DN_HEAD_DIM = 128
DN_WIDTH = 1024
DN_CONV = 5
DN_CHUNK = 64
PEER_HEADS = 8
PEER_N_KEYS = 128
PEER_KEY_DIM = 128
PEER_TOPK = 16
LANES = 128

COL_U = 0
COL_K = 512
COL_V = 1536
COL_AB = 2560
COL_Q = 2592
COL_Z = 3616
COL_G5 = 4640
COL_GD = 5664

VMEM_LIMIT = 56 * 1024 * 1024


def _dot(a, b):
    return jnp.dot(a, b, preferred_element_type=F32)


def _dot_nt(a, b):
    return lax.dot_general(a, b, (((1,), (1,)), ((), ())), preferred_element_type=F32)


def _dot_tn(a, b):
    return lax.dot_general(a, b, (((0,), (0,)), ((), ())), preferred_element_type=F32)


def _sigmoid(x):
    return jax.nn.sigmoid(x)


def _silu(x):
    return x * _sigmoid(x)


def _gelu(x):
    c = math.sqrt(2.0 / math.pi)
    return x * (0.5 * (1.0 + jnp.tanh(c * (x + 0.044715 * (x * x * x)))))


def _softplus(x):
    return jnp.maximum(x, 0.0) + jnp.log1p(jnp.exp(-jnp.abs(x)))


def _rms(x):
    return x * lax.rsqrt(jnp.mean(x * x, axis=-1, keepdims=True) + NORM_EPS)


def _params(sem):
    return pltpu.CompilerParams(dimension_semantics=sem, vmem_limit_bytes=VMEM_LIMIT)


def _adaln_kernel(c_ref, w_ref, b_ref, o_ref):
    c = c_ref[...]
    o_ref[...] = _dot(_silu(c).astype(BF16), w_ref[...]) + b_ref[...]


def _adaln(cv, w_mod, b_mod):
    rows = cv.shape[0]
    ncol = w_mod.shape[1] // D_MODEL
    return pl.pallas_call(
        _adaln_kernel,
        grid=(ncol,),
        in_specs=[pl.BlockSpec((rows, D_MODEL), lambda j: (0, 0)),
                  pl.BlockSpec((D_MODEL, D_MODEL), lambda j: (0, j)),
                  pl.BlockSpec((1, D_MODEL), lambda j: (0, j))],
        out_specs=pl.BlockSpec((rows, D_MODEL), lambda j: (0, j)),
        out_shape=jax.ShapeDtypeStruct((rows, w_mod.shape[1]), F32),
        compiler_params=_params(("arbitrary",)),
        name="adaln",
    )(cv, w_mod, b_mod)


def _inproj_kernel(x_ref, mod_ref, gain_ref, wc_ref, cw_ref, *rest, ncs, row_len, has_gates):
    if has_gates:
        wg_ref, wu_ref, wab_ref, wabt_ref, kvq_ref, gates_ref, u_ref, ab_ref, abt_ref, hx_scr = rest
    else:
        wu_ref, wab_ref, wabt_ref, kvq_ref, u_ref, ab_ref, abt_ref, hx_scr = rest
    j = pl.program_id(2)

    @pl.when(j == 0)
    def _():
        h = _rms(x_ref[...]) * gain_ref[...]
        h = h * (1.0 + mod_ref[1:2, :]) + mod_ref[0:1, :]
        hb = h.astype(BF16)
        hx_scr[...] = hb
        u_ref[...] = _dot(hb, wu_ref[...])
        ab_ref[...] = _dot(hb, wab_ref[...])
        abt_ref[...] = _dot_nt(wabt_ref[...], hb)

    @pl.when(j < ncs)
    def _():
        kvq_ref[...] = _dot(hx_scr[...], wc_ref[...])
        tm, tn = kvq_ref.shape
        tpos = lax.broadcasted_iota(jnp.int32, (tm, LANES), 0) % row_len
        valid = [(tpos + (t - 2) >= 0) & (tpos + (t - 2) < row_len) for t in range(DN_CONV)]
        for cbk in range(tn // LANES):
            cols = slice(cbk * LANES, (cbk + 1) * LANES)
            xc = kvq_ref[:, cols]
            acc = xc * cw_ref[2:3, cols]
            for t in (0, 1, 3, 4):
                acc = acc + jnp.where(valid[t], pltpu.roll(xc, (2 - t) % tm, axis=0), 0.0) * cw_ref[t:t + 1, cols]
            kvq_ref[:, cols] = _silu(acc)

    if has_gates:
        @pl.when(j >= ncs)
        def _():
            gates_ref[...] = _dot(hx_scr[...], wg_ref[...])


def _inproj(x, mods, gain, w_conv, conv_w, w_gate, w_u, w_ab, w_abt, tm, tn, row_len):
    bsz, t_len, d = x.shape
    cc = w_conv.shape[1]
    ncs = cc // tn
    has_gates = w_gate is not None
    ngs = (w_gate.shape[1] // tn) if has_gates else 0
    cstep = lambda j: jnp.minimum(j, ncs - 1)
    gstep = lambda j: jnp.maximum(j - ncs, 0)
    in_specs = [pl.BlockSpec((None, tm, d), lambda b, i, j: (b, i, 0)),
                pl.BlockSpec((None, 6, d), lambda b, i, j: (b, 0, 0)),
                pl.BlockSpec((1, d), lambda b, i, j: (0, 0)),
                pl.BlockSpec((d, tn), lambda b, i, j: (0, cstep(j))),
                pl.BlockSpec((DN_CONV, tn), lambda b, i, j: (0, cstep(j)))]
    args = [x, mods, gain, w_conv, conv_w]
    out_specs = [pl.BlockSpec((None, tm, tn), lambda b, i, j: (b, i, cstep(j)))]
    out_shape = [jax.ShapeDtypeStruct((bsz, t_len, cc), F32)]
    if has_gates:
        in_specs.append(pl.BlockSpec((d, tn), lambda b, i, j: (0, gstep(j))))
        args.append(w_gate)
        out_specs.append(pl.BlockSpec((None, tm, tn), lambda b, i, j: (b, i, gstep(j))))
        out_shape.append(jax.ShapeDtypeStruct((bsz, t_len, w_gate.shape[1]), F32))
    in_specs += [pl.BlockSpec((d, S5_WIDTH), lambda b, i, j: (0, 0)),
                 pl.BlockSpec((d, LANES), lambda b, i, j: (0, 0)),
                 pl.BlockSpec((32, d), lambda b, i, j: (0, 0))]
    args += [w_u, w_ab, w_abt]
    out_specs += [pl.BlockSpec((None, tm, S5_WIDTH), lambda b, i, j: (b, i, 0)),
                  pl.BlockSpec((None, tm, LANES), lambda b, i, j: (b, i, 0)),
                  pl.BlockSpec((None, 32, tm), lambda b, i, j: (b, 0, i))]
    out_shape += [jax.ShapeDtypeStruct((bsz, t_len, S5_WIDTH), F32),
                  jax.ShapeDtypeStruct((bsz, t_len, LANES), F32),
                  jax.ShapeDtypeStruct((bsz, 32, t_len), F32)]
    outs = pl.pallas_call(
        functools.partial(_inproj_kernel, ncs=ncs, row_len=row_len, has_gates=has_gates),
        grid=(bsz, t_len // tm, ncs + ngs),
        in_specs=in_specs,
        out_specs=out_specs,
        out_shape=out_shape,
        scratch_shapes=[pltpu.VMEM((tm, d), BF16)],
        compiler_params=_params(("arbitrary", "arbitrary", "arbitrary")),
        name="inproj",
    )(*args)
    if has_gates:
        return outs
    return (outs[0], None) + tuple(outs[1:])


def _s5_kernel(uf_ref, ub_ref, wbf_ref, wbb_ref, lamf_ref, lamb_ref, cf_ref, cb_ref,
               h0f_ref, h0b_ref, *rest, tc, bsz, need_out):
    if need_out:
        yf_ref, yb_ref, hff_ref, hfb_ref, xsf, xsb, hcf, hcb = rest
    else:
        hff_ref, hfb_ref, xsf, xsb, hcf, hcb = rest
        yf_ref = yb_ref = None
    i = pl.program_id(0)
    half = S5_COLS
    cw = 512
    nblk = half // cw

    @pl.when(i == 0)
    def _():
        hcf[...] = h0f_ref[...]
        hcb[...] = h0b_ref[...]

    dirs = ((uf_ref, wbf_ref, lamf_ref, cf_ref, yf_ref, xsf, hcf, False),
            (ub_ref, wbb_ref, lamb_ref, cb_ref, yb_ref, xsb, hcb, True))
    for u_ref, wb_ref, lam_ref, c_ref, y_ref, xs, hc, rev in dirs:
        for j in range(4):
            lhs = jnp.concatenate(
                [u_ref[:, t * S5_WIDTH + j * LANES: t * S5_WIDTH + (j + 1) * LANES] for t in range(tc)],
                axis=0).astype(BF16)
            r = _dot(lhs, wb_ref[j])
            xs[:, j * cw:(j + 1) * cw] = r[:, :cw]
            xs[:, half + j * cw: half + (j + 1) * cw] = r[:, cw:]
        for cb in range(nblk):
            lo, hi = cb * cw, (cb + 1) * cw
            ar = jnp.broadcast_to(lam_ref[0:1, lo:hi], (bsz, cw))
            ai = jnp.broadcast_to(lam_ref[1:2, lo:hi], (bsz, cw))

            def body(s, carry, lo=lo, hi=hi, ar=ar, ai=ai, xs=xs, rev=rev):
                hr, hi_ = carry
                t = (tc - 1 - s) if rev else s
                row = pl.multiple_of(t * bsz, bsz)
                xr = xs[pl.ds(row, bsz), lo:hi]
                xi = xs[pl.ds(row, bsz), half + lo: half + hi]
                nr = ar * hr - ai * hi_ + xr
                ni = ar * hi_ + ai * hr + xi
                xs[pl.ds(row, bsz), lo:hi] = nr
                xs[pl.ds(row, bsz), half + lo: half + hi] = ni
                return nr, ni

            hr, hi_ = lax.fori_loop(0, tc, body, (hc[:, lo:hi], hc[:, half + lo: half + hi]))
            hc[:, lo:hi] = hr
            hc[:, half + lo: half + hi] = hi_
        if need_out:
            for j in range(4):
                hre = xs[:, j * cw:(j + 1) * cw].astype(BF16)
                him = xs[:, half + j * cw: half + (j + 1) * cw].astype(BF16)
                y = _dot(hre, c_ref[j, 0]) + _dot(him, c_ref[j, 1])
                for t in range(tc):
                    y_ref[:, t * S5_WIDTH + j * LANES: t * S5_WIDTH + (j + 1) * LANES] = y[t * bsz:(t + 1) * bsz]
    hff_ref[...] = hcf[...]
    hfb_ref[...] = hcb[...]


def _s5(u2, wbf, wbb, lamf, lamb, cf, cb, h0f, h0b, tc, need_out):
    bsz = u2.shape[0]
    t_len = u2.shape[1] // S5_WIDTH
    nc = t_len // tc
    blk = tc * S5_WIDTH
    const2 = lambda i: (0, 0)
    const3 = lambda i: (0, 0, 0)
    const4 = lambda i: (0, 0, 0, 0)
    st_spec = pl.BlockSpec((bsz, 2 * S5_COLS), const2)
    st_shape = jax.ShapeDtypeStruct((bsz, 2 * S5_COLS), F32)
    out_specs = [st_spec, st_spec]
    out_shape = [st_shape, st_shape]
    if need_out:
        y_shape = jax.ShapeDtypeStruct(u2.shape, F32)
        out_specs = [pl.BlockSpec((bsz, blk), lambda i: (0, i)),
                     pl.BlockSpec((bsz, blk), lambda i: (0, nc - 1 - i))] + out_specs
        out_shape = [y_shape, y_shape] + out_shape
    return pl.pallas_call(
        functools.partial(_s5_kernel, tc=tc, bsz=bsz, need_out=need_out),
        grid=(nc,),
        in_specs=[pl.BlockSpec((bsz, blk), lambda i: (0, i)),
                  pl.BlockSpec((bsz, blk), lambda i: (0, nc - 1 - i)),
                  pl.BlockSpec(wbf.shape, const3), pl.BlockSpec(wbb.shape, const3),
                  pl.BlockSpec(lamf.shape, const2), pl.BlockSpec(lamb.shape, const2),
                  pl.BlockSpec(cf.shape, const4), pl.BlockSpec(cb.shape, const4),
                  st_spec, st_spec],
        out_specs=out_specs,
        out_shape=out_shape,
        scratch_shapes=[pltpu.VMEM((tc * bsz, 2 * S5_COLS), F32),
                        pltpu.VMEM((tc * bsz, 2 * S5_COLS), F32),
                        pltpu.VMEM((bsz, 2 * S5_COLS), F32),
                        pltpu.VMEM((bsz, 2 * S5_COLS), F32)],
        compiler_params=_params(("arbitrary",)),
        name="s5_scan",
    )(u2, u2, wbf, wbb, lamf, lamb, cf, cb, h0f, h0b)


def _dn_kernel(act_scr, ab_ref, abt_ref, acoef_ref, dtb_ref, acoeft_ref, dtbt_ref, s0_ref,
               *rest, blk_len, rev, need_out):
    if need_out:
        o_ref, sfin_ref, s_scr = rest
    else:
        sfin_ref, s_scr = rest
        o_ref = None
    cs = DN_CHUNK
    nch = blk_len // cs
    goff = 16 if rev else 0
    boff = goff + 8

    @pl.when(pl.program_id(1) == 0)
    def _():
        s_scr[...] = s0_ref[...]

    ab = ab_ref[...]
    g_cols = acoef_ref[...] * _softplus(ab + dtb_ref[...])
    beta_cols = _sigmoid(ab)
    g_rows = acoeft_ref[...] * _softplus(abt_ref[...] + dtbt_ref[...])

    ri = lax.broadcasted_iota(jnp.int32, (cs, cs), 0)
    ci = lax.broadcasted_iota(jnp.int32, (cs, cs), 1)
    incl = (ri <= ci) if rev else (ri >= ci)
    strict = incl & (ri != ci)
    tri = incl.astype(F32)
    eye = (ri == ci).astype(F32)
    last = 0 if rev else cs - 1
    scale = DN_HEAD_DIM ** -0.5

    order = list(range(nch - 1, -1, -1)) if rev else list(range(nch))
    units = [(c, h) for c in order for h in range(DN_HEADS)]
    nu = len(units)
    gc_cols = {}
    gc_rows = {}
    for c in order:
        rows = slice(c * cs, (c + 1) * cs)
        gc_cols[c] = jnp.dot(tri, g_cols[rows, :], precision=lax.Precision.HIGHEST, preferred_element_type=F32)
        gc_rows[c] = lax.dot_general(g_rows[:, rows], tri, (((1,), (1,)), ((), ())),
                                     precision=lax.Precision.HIGHEST, preferred_element_type=F32)

    def head_block(base, c, h):
        return act_scr[c * cs:(c + 1) * cs, base + h * DN_HEAD_DIM: base + (h + 1) * DN_HEAD_DIM]

    def l2n(t):
        return t * lax.rsqrt(jnp.sum(t * t, axis=-1, keepdims=True) + 1e-6)

    gch = [gc_cols[c][:, goff + h: goff + h + 1] for c, h in units]
    gcr = [gc_rows[c][goff + h: goff + h + 1, :] for c, h in units]
    dec = [jnp.where(incl, jnp.exp(jnp.minimum(gch[u] - gcr[u], 0.0)), 0.0) for u in range(nu)]
    bh = [beta_cols[c * cs:(c + 1) * cs, boff + h: boff + h + 1] for c, h in units]
    k = [l2n(head_block(0, c, h)) for c, h in units]
    kbf = [t.astype(BF16) for t in k]
    kb = [k[u] * bh[u] for u in range(nu)]
    kkt = [_dot_nt(kb[u].astype(BF16), kbf[u]) for u in range(nu)]
    a = [jnp.where(strict, kkt[u] * dec[u], 0.0) for u in range(nu)]
    abf = [t.astype(BF16) for t in a]
    m = [_dot(t, t) for t in abf]
    p = [eye - t for t in a]
    for lvl in range(5):
        mb = [t.astype(BF16) for t in m]
        if lvl < 4:
            r = [_dot(jnp.concatenate([m[u], p[u]], axis=0).astype(BF16), mb[u]) for u in range(nu)]
            m = [t[:cs] for t in r]
            p = [p[u] + r[u][cs:] for u in range(nu)]
        else:
            r = [_dot(p[u].astype(BF16), mb[u]) for u in range(nu)]
            p = [p[u] + r[u] for u in range(nu)]
    eg = [jnp.exp(t) for t in gch]
    rhs = [jnp.concatenate([head_block(DN_WIDTH, c, h) * bh[u], kb[u] * eg[u]], axis=-1).astype(BF16)
           for u, (c, h) in enumerate(units)]
    uw = [_dot(p[u].astype(BF16), rhs[u]) for u in range(nu)]
    gl = [t[last:last + 1, :] for t in gch]
    kd = [(k[u] * jnp.exp(gl[u] - gch[u])).astype(BF16) for u in range(nu)]
    if need_out:
        q = [l2n(head_block(2 * DN_WIDTH, c, h)) * scale for c, h in units]
        qkt = [_dot_nt(q[u].astype(BF16), kbf[u]) for u in range(nu)]
        at = [jnp.where(incl, qkt[u] * dec[u], 0.0).astype(BF16) for u in range(nu)]
        qd = [(q[u] * eg[u]).astype(BF16) for u in range(nu)]

    for ci, c in enumerate(order):
        us = range(ci * DN_HEADS, (ci + 1) * DN_HEADS)
        s_old = [s_scr[h] for h in range(DN_HEADS)]
        s_bf = [t.astype(BF16) for t in s_old]
        ws = [_dot(uw[u][:, DN_HEAD_DIM:].astype(BF16), s_bf[h]) for h, u in enumerate(us)]
        if need_out:
            qs = [_dot(qd[u], s_bf[h]) for h, u in enumerate(us)]
        vn = [(uw[u][:, :DN_HEAD_DIM] - ws[h]).astype(BF16) for h, u in enumerate(us)]
        if need_out:
            av = [_dot(at[u], vn[h]) for h, u in enumerate(us)]
        kv = [_dot_tn(kd[u], vn[h]) for h, u in enumerate(us)]
        for h, u in enumerate(us):
            if need_out:
                o_ref[c * cs:(c + 1) * cs, h * DN_HEAD_DIM:(h + 1) * DN_HEAD_DIM] = qs[h] + av[h]
            s_scr[h] = s_old[h] * jnp.exp(gl[u]) + kv[h]
    sfin_ref[...] = s_scr[...]


def _deltanet(kvq, ab, abt, acoef, dtb, acoeft, dtbt, s0, blk_len, rev, need_out):
    bsz, t_len, ncols = kvq.shape
    nb = t_len // blk_len
    bi = (lambda b, i: (b, nb - 1 - i, 0)) if rev else (lambda b, i: (b, i, 0))
    bit = (lambda b, i: (b, 0, nb - 1 - i)) if rev else (lambda b, i: (b, 0, i))
    c2 = lambda b, i: (0, 0)
    st_spec = pl.BlockSpec((None, DN_HEADS, DN_HEAD_DIM, DN_HEAD_DIM), lambda b, i: (b, 0, 0, 0))
    st_shape = jax.ShapeDtypeStruct((bsz, DN_HEADS, DN_HEAD_DIM, DN_HEAD_DIM), F32)
    out_specs = [st_spec]
    out_shape = [st_shape]
    if need_out:
        out_specs = [pl.BlockSpec((None, blk_len, DN_WIDTH), bi)] + out_specs
        out_shape = [jax.ShapeDtypeStruct((bsz, t_len, DN_WIDTH), F32)] + out_shape
    return pl.pallas_call(
        functools.partial(_dn_kernel, blk_len=blk_len, rev=rev, need_out=need_out),
        grid=(bsz, nb),
        in_specs=[pl.BlockSpec((None, blk_len, ncols), bi),
                  pl.BlockSpec((None, blk_len, LANES), bi),
                  pl.BlockSpec((None, 32, blk_len), bit),
                  pl.BlockSpec((1, LANES), c2), pl.BlockSpec((1, LANES), c2),
                  pl.BlockSpec((32, 1), c2), pl.BlockSpec((32, 1), c2),
                  st_spec],
        out_specs=out_specs,
        out_shape=out_shape,
        scratch_shapes=[pltpu.VMEM((DN_HEADS, DN_HEAD_DIM, DN_HEAD_DIM), F32)],
        compiler_params=_params(("arbitrary", "arbitrary")),
        name="deltanet_bwd" if rev else "deltanet_fwd",
    )(kvq, ab, abt, acoef, dtb, acoeft, dtbt, s0)


def _post_kernel(x_ref, mod_ref, yf_ref, yb_ref, u_ref, of_ref, ob_ref, z_ref, g5_ref, gd_ref,
                 s5d_ref, glw_ref, glb_ref, dnw_ref, wb5_ref, wbd_ref, wo_ref, gain_ref, o_ref):
    y5 = _gelu(yf_ref[...] + yb_ref[...] + s5d_ref[...] * u_ref[...])
    y5 = y5 * _sigmoid(_dot(y5.astype(BF16), glw_ref[...]) + glb_ref[...])
    o = of_ref[...] + ob_ref[...]
    parts = []
    for h in range(DN_HEADS):
        oh = o[:, h * DN_HEAD_DIM:(h + 1) * DN_HEAD_DIM]
        parts.append(_rms(oh) * dnw_ref[...])
    y_dn = jnp.concatenate(parts, axis=-1) * _silu(z_ref[...])
    merged = (_sigmoid(g5_ref[...]) * _dot(y5.astype(BF16), wb5_ref[...])
              + _sigmoid(gd_ref[...]) * _dot(y_dn.astype(BF16), wbd_ref[...]))
    x_mix = _dot(merged.astype(BF16), wo_ref[...])
    o_ref[...] = x_ref[...] + mod_ref[2:3, :] * (_rms(x_mix) * gain_ref[...])


def _post(x, mods, yf, yb, u, o_f, o_b, gates, s5d, glw, glb, dnw, wb5, wbd, wo, gain, tm):
    bsz, t_len, d = x.shape
    tok = lambda w: pl.BlockSpec((None, tm, w), lambda b, i: (b, i, 0))
    colblk = lambda k: pl.BlockSpec((None, tm, d), lambda b, i, k=k: (b, i, k))
    full = lambda a: pl.BlockSpec(a.shape, lambda b, i: (0,) * a.ndim)
    return pl.pallas_call(
        _post_kernel,
        grid=(bsz, t_len // tm),
        in_specs=[tok(d), pl.BlockSpec((None, 6, d), lambda b, i: (b, 0, 0)),
                  tok(S5_WIDTH), tok(S5_WIDTH), tok(S5_WIDTH), tok(d), tok(d),
                  colblk(0), colblk(1), colblk(2),
                  full(s5d), full(glw), full(glb), full(dnw), full(wb5), full(wbd), full(wo), full(gain)],
        out_specs=tok(d),
        out_shape=jax.ShapeDtypeStruct(x.shape, F32),
        compiler_params=_params(("arbitrary", "arbitrary")),
        name="mixer_out",
    )(x, mods, yf, yb, u, o_f, o_b, gates, gates, gates, s5d, glw, glb, dnw, wb5, wbd, wo, gain)


_CAND_COUNTS = [PEER_TOPK // (a + 1) for a in range(PEER_TOPK)]
_N_CAND = sum(_CAND_COUNTS)
_CAND_ROWS = -(-_N_CAND // 8) * 8


def _extract_topk(work, k, out_vals_ref=None, want_rank=False, break_ties=True):
    nrows = work.shape[0]
    iota = lax.broadcasted_iota(jnp.int32, work.shape, 0)
    rank = jnp.full(work.shape, float(k), F32) if want_rank else None
    vals = []
    for r in range(k):
        m = jnp.max(work, axis=0, keepdims=True)
        if break_ties:
            idx = jnp.min(jnp.where(work == m, iota, nrows), axis=0, keepdims=True)
            sel = iota == idx
        else:
            sel = work == m
        work = jnp.where(sel, -jnp.inf, work)
        if want_rank:
            rank = jnp.where(sel, float(r), rank)
        vals.append(m)
        if out_vals_ref is not None:
            out_vals_ref[r:r + 1, :] = m
    return vals, work, rank


def _route_kernel(x_ref, mod_ref, gain_ref, wq_ref, keys_ref,
                  ht_ref, c1_ref, e1_ref, r2_ref, p2_ref, q_scr, sv_scr, cand_scr):
    h = _rms(x_ref[...]) * gain_ref[...]
    h = h * (1.0 + mod_ref[4:5, :]) + mod_ref[3:4, :]
    hb = h.astype(BF16)
    ht_ref[...] = h.T.astype(BF16)
    q_scr[...] = _dot(hb, wq_ref[...]).astype(BF16)
    topk = float(PEER_TOPK)

    def route_head(hd, scores, break_ties):
        svals = []
        ranks = []
        for j in range(2):
            vals, _, rank = _extract_topk(scores[j], PEER_TOPK, sv_scr.at[j], want_rank=True,
                                          break_ties=break_ties)
            svals.append(vals)
            ranks.append(rank)
        cand_scr[...] = jnp.full(cand_scr.shape, -jnp.inf, F32)
        off = 0
        for a in range(PEER_TOPK):
            nb = _CAND_COUNTS[a]
            cand_scr[off: off + nb, :] = svals[0][a] + sv_scr[1, 0:nb, :]
            off += nb
        top, left, _ = _extract_topk(cand_scr[...], PEER_TOPK, break_ties=break_ties)
        m0 = top[0]
        z = jnp.exp(top[0] - m0)
        for r in range(1, PEER_TOPK):
            z = z + jnp.exp(top[r] - m0)
        taken = (left == -jnp.inf).astype(F32)
        c1 = jnp.zeros(ranks[0].shape, F32)
        n_taken = jnp.zeros_like(m0)
        off = 0
        for a in range(PEER_TOPK):
            nb = _CAND_COUNTS[a]
            cnt = jnp.sum(taken[off: off + nb, :], axis=0, keepdims=True)
            c1 = jnp.where(ranks[0] == float(a), cnt, c1)
            n_taken = n_taken + cnt
            off += nb
        member = [r < topk for r in ranks]
        c1_ref[hd] = c1
        r2_ref[hd] = ranks[1]
        e1_ref[hd] = jnp.where(member[0], jnp.exp(scores[0] - svals[0][0]), 0.0)
        p2_ref[hd] = jnp.where(member[1], jnp.exp(scores[1] - svals[1][0]), 0.0) / z
        n_member = [jnp.sum(mb.astype(F32), axis=0, keepdims=True) for mb in member]
        return jnp.abs(n_member[0] - topk) + jnp.abs(n_member[1] - topk) + jnp.abs(n_taken - topk)

    def head(hd, carry):
        scores = []
        for j in range(2):
            col = pl.multiple_of((hd * 2 + j) * PEER_KEY_DIM, PEER_KEY_DIM)
            scores.append(_dot_nt(keys_ref[hd, j], q_scr[:, pl.ds(col, PEER_KEY_DIM)]))
        excess = jnp.max(route_head(hd, scores, break_ties=False))

        @pl.when(excess > 0.0)
        def _():
            route_head(hd, scores, break_ties=True)

        return carry

    lax.fori_loop(0, PEER_HEADS, head, 0)


def _route(x1, mods, gain, wq, keys, tm):
    bsz, t_len, d = x1.shape
    n_tok = bsz * t_len
    nt = t_len // tm
    tile = lambda b, i: (0, 0, b * nt + i)
    hk = pl.BlockSpec((PEER_HEADS, PEER_N_KEYS, tm), tile)
    hk_shape = jax.ShapeDtypeStruct((PEER_HEADS, PEER_N_KEYS, n_tok), F32)
    return pl.pallas_call(
        _route_kernel,
        grid=(bsz, nt),
        in_specs=[pl.BlockSpec((None, tm, d), lambda b, i: (b, i, 0)),
                  pl.BlockSpec((None, 6, d), lambda b, i: (b, 0, 0)),
                  pl.BlockSpec((1, d), lambda b, i: (0, 0)),
                  pl.BlockSpec(wq.shape, lambda b, i: (0, 0)),
                  pl.BlockSpec(keys.shape, lambda b, i: (0, 0, 0, 0))],
        out_specs=[pl.BlockSpec((d, tm), lambda b, i: (0, b * nt + i)),
                   hk, hk, hk, hk],
        out_shape=[jax.ShapeDtypeStruct((d, n_tok), BF16),
                   hk_shape, hk_shape, hk_shape, hk_shape],
        scratch_shapes=[pltpu.VMEM((tm, wq.shape[1]), BF16),
                        pltpu.VMEM((2, PEER_TOPK, tm), F32),
                        pltpu.VMEM((_CAND_ROWS, tm), F32)],
        compiler_params=_params(("arbitrary", "arbitrary")),
        name="peer_route",
    )(x1, mods, gain, wq, keys)


def _peer_kernel(ht_ref, u_ref, vt_ref, c1_ref, e1_ref, r2_ref, p2_ref, x1_ref, mod_ref, gain_ref,
                 o_ref, acc, act_scr, a_scr, *, nsub):
    e = pl.program_id(2)

    @pl.when(e == 0)
    def _():
        acc[...] = jnp.zeros(acc.shape, F32)

    ncol = act_scr.shape[1] // LANES
    tr = 32
    nrt = PEER_N_KEYS // tr

    act_scr[...] = _dot(u_ref[...], ht_ref[...])
    for ct in range(ncol):
        cols = slice(ct * LANES, (ct + 1) * LANES)
        for rt in range(nrt):
            krows = slice(rt * tr, (rt + 1) * tr)
            ws = [None] * nsub
            for hd in range(PEER_HEADS):
                r2t = r2_ref[hd, krows, cols]
                p2t = p2_ref[hd, krows, cols]
                for ii in range(nsub):
                    contrib = jnp.where(r2t < c1_ref[hd, ii:ii + 1, cols], p2t * e1_ref[hd, ii:ii + 1, cols], 0.0)
                    ws[ii] = contrib if ws[ii] is None else ws[ii] + contrib
            for ii in range(nsub):
                rows = slice(ii * PEER_N_KEYS + rt * tr, ii * PEER_N_KEYS + (rt + 1) * tr)
                a_scr[rows, cols] = (_gelu(act_scr[rows, cols]) * ws[ii]).astype(BF16)
    acc[...] += _dot(vt_ref[...], a_scr[...])

    @pl.when(e == pl.num_programs(2) - 1)
    def _():
        out = acc[...].T
        o_ref[...] = x1_ref[...] + mod_ref[5:6, :] * (_rms(out) * gain_ref[...])


def _peer(ht, u_tab, v_tab, c1, e1, r2, p2, x1, mods, gain, tn, eb):
    bsz, t_len, d = x1.shape
    nt = t_len // tn
    n_exp = u_tab.shape[0]
    nsub = eb // PEER_N_KEYS
    ne = n_exp // eb
    n_tok = bsz * t_len
    vt_tab = v_tab.reshape(ne, eb, d).transpose(0, 2, 1)
    c1v = c1.reshape(PEER_HEADS, ne, nsub, n_tok)
    e1v = e1.reshape(PEER_HEADS, ne, nsub, n_tok)
    sub = pl.BlockSpec((PEER_HEADS, None, nsub, tn), lambda b, i, e: (0, e, 0, b * nt + i))
    hk = pl.BlockSpec((PEER_HEADS, PEER_N_KEYS, tn), lambda b, i, e: (0, 0, b * nt + i))
    return pl.pallas_call(
        functools.partial(_peer_kernel, nsub=nsub),
        grid=(bsz, nt, ne),
        in_specs=[pl.BlockSpec((d, tn), lambda b, i, e: (0, b * nt + i)),
                  pl.BlockSpec((eb, d), lambda b, i, e: (e, 0)),
                  pl.BlockSpec((None, d, eb), lambda b, i, e: (e, 0, 0)),
                  sub, sub, hk, hk,
                  pl.BlockSpec((None, tn, d), lambda b, i, e: (b, i, 0)),
                  pl.BlockSpec((None, 6, d), lambda b, i, e: (b, 0, 0)),
                  pl.BlockSpec((1, d), lambda b, i, e: (0, 0))],
        out_specs=pl.BlockSpec((None, tn, d), lambda b, i, e: (b, i, 0)),
        out_shape=jax.ShapeDtypeStruct(x1.shape, F32),
        scratch_shapes=[pltpu.VMEM((d, tn), F32), pltpu.VMEM((eb, tn), F32), pltpu.VMEM((eb, tn), BF16)],
        compiler_params=_params(("arbitrary", "arbitrary", "arbitrary")),
        name="peer_experts",
    )(ht, u_tab, vt_tab, c1v, e1v, r2, p2, x1, mods, gain)


def _s5_coeffs(lam_re, lam_im, log_step, b_re, b_im, c_re, c_im):
    step = jnp.exp(log_step)[:, None]
    mag = jnp.exp(lam_re * step)
    ab_re, ab_im = mag * jnp.cos(lam_im * step), mag * jnp.sin(lam_im * step)
    den = lam_re * lam_re + lam_im * lam_im
    f_re = ((ab_re - 1.0) * lam_re + ab_im * lam_im) / den
    f_im = (ab_im * lam_re - (ab_re - 1.0) * lam_im) / den
    bb_re = f_re[..., None] * b_re - f_im[..., None] * b_im
    bb_im = f_re[..., None] * b_im + f_im[..., None] * b_re
    lam = jnp.stack([ab_re.reshape(-1), ab_im.reshape(-1)])
    eye8 = jnp.eye(8, dtype=F32)

    def in_blockdiag(bb):
        t = bb.reshape(4, 8, S5_STATE, S5_GROUP).transpose(0, 1, 3, 2)
        return jnp.einsum('jghp,gk->jghkp', t, eye8).reshape(4, LANES, 8 * S5_STATE)

    def out_blockdiag(cc):
        t = cc.reshape(4, 8, S5_GROUP, S5_STATE).transpose(0, 1, 3, 2)
        return jnp.einsum('jgph,gk->jgpkh', t, eye8).reshape(4, 8 * S5_STATE, LANES)

    wb = jnp.concatenate([in_blockdiag(bb_re), in_blockdiag(bb_im)], axis=-1).astype(BF16)
    cmat = jnp.stack([out_blockdiag(c_re), -out_blockdiag(c_im)], axis=1).astype(BF16)
    return wb, lam, cmat


def _gate_rows(a_log, dt_bias):
    neg_a = -jnp.exp(a_log)
    acoef = jnp.zeros((LANES,), F32).at[0:8].set(neg_a[0]).at[16:24].set(neg_a[1])
    dtb = jnp.zeros((LANES,), F32).at[0:8].set(dt_bias[0]).at[16:24].set(dt_bias[1])
    return acoef[None, :], dtb[None, :], acoef[:32, None], dtb[:32, None]


def kernel(x, c, ctx, c_ctx, w_mod, b_mod, norm_pre_mix, norm_post_mix, norm_pre_ffn, norm_post_ffn, w_in, s5_lam_re, s5_lam_im, s5_log_step, s5_b_re, s5_b_im, s5_c_re, s5_c_im, s5_d, s5_glu_w, s5_glu_b, dn_conv_w, dn_a_log, dn_dt_bias, dn_norm_w, w_branch_s5, w_branch_dn, w_out, peer_w_q, peer_sub_keys, peer_u, peer_v):
    bsz, t_len, d = x.shape
    ctx_len = ctx.shape[1]
    l = 0

    pad = (-(bsz + 1)) % 8
    cv = jnp.concatenate([c, c_ctx[None, :], jnp.zeros((pad, d), F32)], axis=0)
    mod_all = _adaln(cv, w_mod[l].astype(BF16), b_mod[l][None, :])
    mods = mod_all[:bsz].reshape(bsz, 6, d)
    mods_ctx = jnp.broadcast_to(mod_all[bsz].reshape(1, 6, d), (bsz, 6, d))

    w = w_in[l]
    w_kvq = jnp.concatenate([w[:, COL_K:COL_AB], w[:, COL_Q:COL_Z]], axis=1).astype(BF16)
    w_gates = w[:, COL_Z:].astype(BF16)
    w_ctx = w[:, COL_K:COL_AB].astype(BF16)
    w_u = w[:, COL_U:COL_K].astype(BF16)
    w_ab32 = w[:, COL_AB:COL_Q]
    w_ab = jnp.pad(w_ab32, ((0, 0), (0, LANES - 32))).astype(BF16)
    w_abt = w_ab32.T.astype(BF16)
    gain_pre = norm_pre_mix[l][None, :]
    conv_w = dn_conv_w[l]

    kv_c, _, u_c, ab_c, abt_c = _inproj(ctx, mods_ctx, gain_pre, w_ctx, conv_w[:, :2 * DN_WIDTH], None,
                                        w_u, w_ab, w_abt, tm=ctx_len, tn=2 * DN_WIDTH, row_len=ctx_len)
    kvq_x, gates_x, u_x, ab_x, abt_x = _inproj(x, mods, gain_pre, w_kvq, conv_w, w_gates, w_u, w_ab, w_abt,
                                               tm=min(1024, t_len), tn=1536, row_len=GRID_W)

    wbf, lamf, cf = _s5_coeffs(s5_lam_re[l, 0], s5_lam_im[l, 0], s5_log_step[l, 0], s5_b_re[l, 0], s5_b_im[l, 0],
                               s5_c_re[l, 0], s5_c_im[l, 0])
    wbb, lamb, cb = _s5_coeffs(s5_lam_re[l, 1], s5_lam_im[l, 1], s5_log_step[l, 1], s5_b_re[l, 1], s5_b_im[l, 1],
                               s5_c_re[l, 1], s5_c_im[l, 1])
    zeros_h = jnp.zeros((bsz, 2 * S5_COLS), F32)
    tc = 32
    hf_c, hb_c = _s5(u_c.reshape(bsz, ctx_len * S5_WIDTH), wbf, wbb, lamf, lamb, cf, cb, zeros_h, zeros_h, tc, False)
    yf, yb, _, _ = _s5(u_x.reshape(bsz, t_len * S5_WIDTH), wbf, wbb, lamf, lamb, cf, cb, hf_c, hb_c, tc, True)
    yf = yf.reshape(bsz, t_len, S5_WIDTH)
    yb = yb.reshape(bsz, t_len, S5_WIDTH)

    acoef, dtb, acoeft, dtbt = _gate_rows(dn_a_log[l], dn_dt_bias[l])
    zeros_s = jnp.zeros((bsz, DN_HEADS, DN_HEAD_DIM, DN_HEAD_DIM), F32)
    dn_args = (acoef, dtb, acoeft, dtbt)
    (sf_c,) = _deltanet(kv_c, ab_c, abt_c, *dn_args, zeros_s, ctx_len, False, False)
    (sb_c,) = _deltanet(kv_c, ab_c, abt_c, *dn_args, zeros_s, ctx_len, True, False)
    blk_len = min(256, t_len)
    o_f, _ = _deltanet(kvq_x, ab_x, abt_x, *dn_args, sf_c, blk_len, False, True)
    o_b, _ = _deltanet(kvq_x, ab_x, abt_x, *dn_args, sb_c, blk_len, True, True)

    x1 = _post(x, mods, yf, yb, u_x, o_f, o_b, gates_x,
               s5_d[l][None, :], s5_glu_w[l].astype(BF16), s5_glu_b[l][None, :], dn_norm_w[l][None, :],
               w_branch_s5[l].astype(BF16), w_branch_dn[l].astype(BF16), w_out[l].astype(BF16),
               norm_post_mix[l][None, :], tm=min(512, t_len))

    ht, c1, e1, r2, p2 = _route(x1, mods, norm_pre_ffn[l][None, :], peer_w_q[l].astype(BF16),
                                peer_sub_keys[l].astype(BF16), tm=min(256, t_len))
    out = _peer(ht, peer_u[l].astype(BF16), peer_v[l].astype(BF16), c1, e1, r2, p2, x1, mods,
                norm_post_ffn[l][None, :], tn=min(1024, t_len), eb=512)
    return out
```

```python
import functools
import math

import jax
import jax.numpy as jnp
from jax import lax
from jax.experimental import pallas as pl
from jax.experimental.pallas import tpu as pltpu

F32 = jnp.float32
BF16 = jnp.bfloat16

D_MODEL = 1024
NORM_EPS = 1e-6
GRID_W = 64
S5_WIDTH = 512
S5_GROUP = 16
S5_GROUPS = 32
S5_STATE = 64
S5_COLS = S5_GROUPS * S5_STATE
DN_HEADS = 8
DN_HEAD_DIM = 128
DN_WIDTH = 1024
DN_CONV = 5
DN_CHUNK = 64
PEER_HEADS = 8
PEER_N_KEYS = 128
PEER_KEY_DIM = 128
PEER_TOPK = 16
LANES = 128

COL_U = 0
COL_K = 512
COL_V = 1536
COL_AB = 2560
COL_Q = 2592
COL_Z = 3616
COL_G5 = 4640
COL_GD = 5664

VMEM_LIMIT = 56 * 1024 * 1024


def _dot(a, b):
    return jnp.dot(a, b, preferred_element_type=F32)


def _dot_nt(a, b):
    return lax.dot_general(a, b, (((1,), (1,)), ((), ())), preferred_element_type=F32)


def _dot_tn(a, b):
    return lax.dot_general(a, b, (((0,), (0,)), ((), ())), preferred_element_type=F32)


def _sigmoid(x):
    return jax.nn.sigmoid(x)


def _silu(x):
    return x * _sigmoid(x)


def _gelu(x):
    c = math.sqrt(2.0 / math.pi)
    return x * (0.5 * (1.0 + jnp.tanh(c * (x + 0.044715 * (x * x * x)))))


def _softplus(x):
    return jnp.maximum(x, 0.0) + jnp.log1p(jnp.exp(-jnp.abs(x)))


def _rms(x):
    return x * lax.rsqrt(jnp.mean(x * x, axis=-1, keepdims=True) + NORM_EPS)


def _params(sem):
    return pltpu.CompilerParams(dimension_semantics=sem, vmem_limit_bytes=VMEM_LIMIT)


def _adaln_kernel(c_ref, w_ref, b_ref, o_ref):
    c = c_ref[...]
    o_ref[...] = _dot(_silu(c).astype(BF16), w_ref[...]) + b_ref[...]


def _adaln(cv, w_mod, b_mod):
    rows = cv.shape[0]
    ncol = w_mod.shape[1] // D_MODEL
    return pl.pallas_call(
        _adaln_kernel,
        grid=(ncol,),
        in_specs=[pl.BlockSpec((rows, D_MODEL), lambda j: (0, 0)),
                  pl.BlockSpec((D_MODEL, D_MODEL), lambda j: (0, j)),
                  pl.BlockSpec((1, D_MODEL), lambda j: (0, j))],
        out_specs=pl.BlockSpec((rows, D_MODEL), lambda j: (0, j)),
        out_shape=jax.ShapeDtypeStruct((rows, w_mod.shape[1]), F32),
        compiler_params=_params(("arbitrary",)),
        name="adaln",
    )(cv, w_mod, b_mod)


def _inproj_kernel(x_ref, mod_ref, gain_ref, wc_ref, cw_ref, *rest, ncs, row_len, has_gates):
    if has_gates:
        wg_ref, wu_ref, wab_ref, wabt_ref, kvq_ref, gates_ref, u_ref, ab_ref, abt_ref, hx_scr = rest
    else:
        wu_ref, wab_ref, wabt_ref, kvq_ref, u_ref, ab_ref, abt_ref, hx_scr = rest
    j = pl.program_id(2)

    @pl.when(j == 0)
    def _():
        h = _rms(x_ref[...]) * gain_ref[...]
        h = h * (1.0 + mod_ref[1:2, :]) + mod_ref[0:1, :]
        hb = h.astype(BF16)
        hx_scr[...] = hb
        u_ref[...] = _dot(hb, wu_ref[...])
        ab_ref[...] = _dot(hb, wab_ref[...])
        abt_ref[...] = _dot_nt(wabt_ref[...], hb)

    @pl.when(j < ncs)
    def _():
        kvq_ref[...] = _dot(hx_scr[...], wc_ref[...])
        tm, tn = kvq_ref.shape
        tpos = lax.broadcasted_iota(jnp.int32, (tm, LANES), 0) % row_len
        valid = [(tpos + (t - 2) >= 0) & (tpos + (t - 2) < row_len) for t in range(DN_CONV)]
        for cbk in range(tn // LANES):
            cols = slice(cbk * LANES, (cbk + 1) * LANES)
            xc = kvq_ref[:, cols]
            acc = xc * cw_ref[2:3, cols]
            for t in (0, 1, 3, 4):
                acc = acc + jnp.where(valid[t], pltpu.roll(xc, (2 - t) % tm, axis=0), 0.0) * cw_ref[t:t + 1, cols]
            kvq_ref[:, cols] = _silu(acc)

    if has_gates:
        @pl.when(j >= ncs)
        def _():
            gates_ref[...] = _dot(hx_scr[...], wg_ref[...])


def _inproj(x, mods, gain, w_conv, conv_w, w_gate, w_u, w_ab, w_abt, tm, tn, row_len):
    bsz, t_len, d = x.shape
    cc = w_conv.shape[1]
    ncs = cc // tn
    has_gates = w_gate is not None
    ngs = (w_gate.shape[1] // tn) if has_gates else 0
    cstep = lambda j: jnp.minimum(j, ncs - 1)
    gstep = lambda j: jnp.maximum(j - ncs, 0)
    in_specs = [pl.BlockSpec((None, tm, d), lambda b, i, j: (b, i, 0)),
                pl.BlockSpec((None, 6, d), lambda b, i, j: (b, 0, 0)),
                pl.BlockSpec((1, d), lambda b, i, j: (0, 0)),
                pl.BlockSpec((d, tn), lambda b, i, j: (0, cstep(j))),
                pl.BlockSpec((DN_CONV, tn), lambda b, i, j: (0, cstep(j)))]
    args = [x, mods, gain, w_conv, conv_w]
    out_specs = [pl.BlockSpec((None, tm, tn), lambda b, i, j: (b, i, cstep(j)))]
    out_shape = [jax.ShapeDtypeStruct((bsz, t_len, cc), F32)]
    if has_gates:
        in_specs.append(pl.BlockSpec((d, tn), lambda b, i, j: (0, gstep(j))))
        args.append(w_gate)
        out_specs.append(pl.BlockSpec((None, tm, tn), lambda b, i, j: (b, i, gstep(j))))
        out_shape.append(jax.ShapeDtypeStruct((bsz, t_len, w_gate.shape[1]), F32))
    in_specs += [pl.BlockSpec((d, S5_WIDTH), lambda b, i, j: (0, 0)),
                 pl.BlockSpec((d, LANES), lambda b, i, j: (0, 0)),
                 pl.BlockSpec((32, d), lambda b, i, j: (0, 0))]
    args += [w_u, w_ab, w_abt]
    out_specs += [pl.BlockSpec((None, tm, S5_WIDTH), lambda b, i, j: (b, i, 0)),
                  pl.BlockSpec((None, tm, LANES), lambda b, i, j: (b, i, 0)),
                  pl.BlockSpec((None, 32, tm), lambda b, i, j: (b, 0, i))]
    out_shape += [jax.ShapeDtypeStruct((bsz, t_len, S5_WIDTH), F32),
                  jax.ShapeDtypeStruct((bsz, t_len, LANES), F32),
                  jax.ShapeDtypeStruct((bsz, 32, t_len), F32)]
    outs = pl.pallas_call(
        functools.partial(_inproj_kernel, ncs=ncs, row_len=row_len, has_gates=has_gates),
        grid=(bsz, t_len // tm, ncs + ngs),
        in_specs=in_specs,
        out_specs=out_specs,
        out_shape=out_shape,
        scratch_shapes=[pltpu.VMEM((tm, d), BF16)],
        compiler_params=_params(("arbitrary", "arbitrary", "arbitrary")),
        name="inproj",
    )(*args)
    if has_gates:
        return outs
    return (outs[0], None) + tuple(outs[1:])


def _s5_kernel(uf_ref, ub_ref, wbf_ref, wbb_ref, lamf_ref, lamb_ref, cf_ref, cb_ref,
               h0f_ref, h0b_ref, *rest, tc, bsz, need_out):
    if need_out:
        yf_ref, yb_ref, hff_ref, hfb_ref, xsf, xsb, hcf, hcb = rest
    else:
        hff_ref, hfb_ref, xsf, xsb, hcf, hcb = rest
        yf_ref = yb_ref = None
    i = pl.program_id(0)
    half = S5_COLS
    cw = 512
    nblk = half // cw

    @pl.when(i == 0)
    def _():
        hcf[...] = h0f_ref[...]
        hcb[...] = h0b_ref[...]

    dirs = ((uf_ref, wbf_ref, lamf_ref, cf_ref, yf_ref, xsf, hcf, False),
            (ub_ref, wbb_ref, lamb_ref, cb_ref, yb_ref, xsb, hcb, True))
    for u_ref, wb_ref, lam_ref, c_ref, y_ref, xs, hc, rev in dirs:
        for j in range(4):
            lhs = jnp.concatenate(
                [u_ref[:, t, j * LANES:(j + 1) * LANES] for t in range(tc)],
                axis=0).astype(BF16)
            r = _dot(lhs, wb_ref[j])
            xs[:, j * cw:(j + 1) * cw] = r[:, :cw]
            xs[:, half + j * cw: half + (j + 1) * cw] = r[:, cw:]
        for cb in range(nblk):
            lo, hi = cb * cw, (cb + 1) * cw
            ar = jnp.broadcast_to(lam_ref[0:1, lo:hi], (bsz, cw))
            ai = jnp.broadcast_to(lam_ref[1:2, lo:hi], (bsz, cw))

            def body(s, carry, lo=lo, hi=hi, ar=ar, ai=ai, xs=xs, rev=rev):
                hr, hi_ = carry
                t = (tc - 1 - s) if rev else s
                row = pl.multiple_of(t * bsz, bsz)
                xr = xs[pl.ds(row, bsz), lo:hi]
                xi = xs[pl.ds(row, bsz), half + lo: half + hi]
                nr = ar * hr - ai * hi_ + xr
                ni = ar * hi_ + ai * hr + xi
                xs[pl.ds(row, bsz), lo:hi] = nr
                xs[pl.ds(row, bsz), half + lo: half + hi] = ni
                return nr, ni

            hr, hi_ = lax.fori_loop(0, tc, body, (hc[:, lo:hi], hc[:, half + lo: half + hi]))
            hc[:, lo:hi] = hr
            hc[:, half + lo: half + hi] = hi_
        if need_out:
            for j in range(4):
                hre = xs[:, j * cw:(j + 1) * cw].astype(BF16)
                him = xs[:, half + j * cw: half + (j + 1) * cw].astype(BF16)
                y = _dot(hre, c_ref[j, 0]) + _dot(him, c_ref[j, 1])
                for t in range(tc):
                    y_ref[:, t, j * LANES:(j + 1) * LANES] = y[t * bsz:(t + 1) * bsz]
    hff_ref[...] = hcf[...]
    hfb_ref[...] = hcb[...]


def _s5(u, wbf, wbb, lamf, lamb, cf, cb, h0f, h0b, tc, need_out):
    bsz, t_len, _ = u.shape
    nc = t_len // tc
    const2 = lambda i: (0, 0)
    const3 = lambda i: (0, 0, 0)
    const4 = lambda i: (0, 0, 0, 0)
    st_spec = pl.BlockSpec((bsz, 2 * S5_COLS), const2)
    st_shape = jax.ShapeDtypeStruct((bsz, 2 * S5_COLS), F32)
    out_specs = [st_spec, st_spec]
    out_shape = [st_shape, st_shape]
    fwd_blk = pl.BlockSpec((bsz, tc, S5_WIDTH), lambda i: (0, i, 0))
    bwd_blk = pl.BlockSpec((bsz, tc, S5_WIDTH), lambda i: (0, nc - 1 - i, 0))
    if need_out:
        y_shape = jax.ShapeDtypeStruct(u.shape, F32)
        out_specs = [fwd_blk, bwd_blk] + out_specs
        out_shape = [y_shape, y_shape] + out_shape
    return pl.pallas_call(
        functools.partial(_s5_kernel, tc=tc, bsz=bsz, need_out=need_out),
        grid=(nc,),
        in_specs=[fwd_blk, bwd_blk,
                  pl.BlockSpec(wbf.shape, const3), pl.BlockSpec(wbb.shape, const3),
                  pl.BlockSpec(lamf.shape, const2), pl.BlockSpec(lamb.shape, const2),
                  pl.BlockSpec(cf.shape, const4), pl.BlockSpec(cb.shape, const4),
                  st_spec, st_spec],
        out_specs=out_specs,
        out_shape=out_shape,
        scratch_shapes=[pltpu.VMEM((tc * bsz, 2 * S5_COLS), F32),
                        pltpu.VMEM((tc * bsz, 2 * S5_COLS), F32),
                        pltpu.VMEM((bsz, 2 * S5_COLS), F32),
                        pltpu.VMEM((bsz, 2 * S5_COLS), F32)],
        compiler_params=_params(("arbitrary",)),
        name="s5_scan",
    )(u, u, wbf, wbb, lamf, lamb, cf, cb, h0f, h0b)


def _dn_kernel(act_scr, ab_ref, abt_ref, acoef_ref, dtb_ref, acoeft_ref, dtbt_ref, s0_ref,
               *rest, blk_len, rev, need_out):
    if need_out:
        o_ref, sfin_ref, s_scr = rest
    else:
        sfin_ref, s_scr = rest
        o_ref = None
    cs = DN_CHUNK
    nch = blk_len // cs
    goff = 16 if rev else 0
    boff = goff + 8

    @pl.when(pl.program_id(1) == 0)
    def _():
        s_scr[...] = s0_ref[...]

    ab = ab_ref[...]
    g_cols = acoef_ref[...] * _softplus(ab + dtb_ref[...])
    beta_cols = _sigmoid(ab)
    g_rows = acoeft_ref[...] * _softplus(abt_ref[...] + dtbt_ref[...])

    ri = lax.broadcasted_iota(jnp.int32, (cs, cs), 0)
    ci = lax.broadcasted_iota(jnp.int32, (cs, cs), 1)
    incl = (ri <= ci) if rev else (ri >= ci)
    strict = incl & (ri != ci)
    tri = incl.astype(F32)
    eye = (ri == ci).astype(F32)
    last = 0 if rev else cs - 1
    scale = DN_HEAD_DIM ** -0.5

    order = list(range(nch - 1, -1, -1)) if rev else list(range(nch))
    units = [(c, h) for c in order for h in range(DN_HEADS)]
    nu = len(units)
    gc_cols = {}
    gc_rows = {}
    for c in order:
        rows = slice(c * cs, (c + 1) * cs)
        gc_cols[c] = jnp.dot(tri, g_cols[rows, :], precision=lax.Precision.HIGHEST, preferred_element_type=F32)
        gc_rows[c] = lax.dot_general(g_rows[:, rows], tri, (((1,), (1,)), ((), ())),
                                     precision=lax.Precision.HIGHEST, preferred_element_type=F32)

    def head_block(base, c, h):
        return act_scr[c * cs:(c + 1) * cs, base + h * DN_HEAD_DIM: base + (h + 1) * DN_HEAD_DIM]

    def l2n(t):
        return t * lax.rsqrt(jnp.sum(t * t, axis=-1, keepdims=True) + 1e-6)

    gch = [gc_cols[c][:, goff + h: goff + h + 1] for c, h in units]
    gcr = [gc_rows[c][goff + h: goff + h + 1, :] for c, h in units]
    dec = [jnp.where(incl, jnp.exp(jnp.minimum(gch[u] - gcr[u], 0.0)), 0.0) for u in range(nu)]
    bh = [beta_cols[c * cs:(c + 1) * cs, boff + h: boff + h + 1] for c, h in units]
    k = [l2n(head_block(0, c, h)) for c, h in units]
    kbf = [t.astype(BF16) for t in k]
    kb = [k[u] * bh[u] for u in range(nu)]
    kkt = [_dot_nt(kb[u].astype(BF16), kbf[u]) for u in range(nu)]
    a = [jnp.where(strict, kkt[u] * dec[u], 0.0) for u in range(nu)]
    abf = [t.astype(BF16) for t in a]
    m = [_dot(t, t) for t in abf]
    p = [eye - t for t in a]
    for lvl in range(5):
        mb = [t.astype(BF16) for t in m]
        if lvl < 4:
            r = [_dot(jnp.concatenate([m[u], p[u]], axis=0).astype(BF16), mb[u]) for u in range(nu)]
            m = [t[:cs] for t in r]
            p = [p[u] + r[u][cs:] for u in range(nu)]
        else:
            r = [_dot(p[u].astype(BF16), mb[u]) for u in range(nu)]
            p = [p[u] + r[u] for u in range(nu)]
    eg = [jnp.exp(t) for t in gch]
    rhs = [jnp.concatenate([head_block(DN_WIDTH, c, h) * bh[u], kb[u] * eg[u]], axis=-1).astype(BF16)
           for u, (c, h) in enumerate(units)]
    uw = [_dot(p[u].astype(BF16), rhs[u]) for u in range(nu)]
    gl = [t[last:last + 1, :] for t in gch]
    kd = [(k[u] * jnp.exp(gl[u] - gch[u])).astype(BF16) for u in range(nu)]
    if need_out:
        q = [l2n(head_block(2 * DN_WIDTH, c, h)) * scale for c, h in units]
        qkt = [_dot_nt(q[u].astype(BF16), kbf[u]) for u in range(nu)]
        at = [jnp.where(incl, qkt[u] * dec[u], 0.0).astype(BF16) for u in range(nu)]
        qd = [(q[u] * eg[u]).astype(BF16) for u in range(nu)]

    for ci, c in enumerate(order):
        us = range(ci * DN_HEADS, (ci + 1) * DN_HEADS)
        s_old = [s_scr[h] for h in range(DN_HEADS)]
        s_bf = [t.astype(BF16) for t in s_old]
        ws = [_dot(uw[u][:, DN_HEAD_DIM:].astype(BF16), s_bf[h]) for h, u in enumerate(us)]
        if need_out:
            qs = [_dot(qd[u], s_bf[h]) for h, u in enumerate(us)]
        vn = [(uw[u][:, :DN_HEAD_DIM] - ws[h]).astype(BF16) for h, u in enumerate(us)]
        if need_out:
            av = [_dot(at[u], vn[h]) for h, u in enumerate(us)]
        kv = [_dot_tn(kd[u], vn[h]) for h, u in enumerate(us)]
        for h, u in enumerate(us):
            if need_out:
                o_ref[c * cs:(c + 1) * cs, h * DN_HEAD_DIM:(h + 1) * DN_HEAD_DIM] = qs[h] + av[h]
            s_scr[h] = s_old[h] * jnp.exp(gl[u]) + kv[h]
    sfin_ref[...] = s_scr[...]


def _deltanet(kvq, ab, abt, acoef, dtb, acoeft, dtbt, s0, blk_len, rev, need_out):
    bsz, t_len, ncols = kvq.shape
    nb = t_len // blk_len
    bi = (lambda b, i: (b, nb - 1 - i, 0)) if rev else (lambda b, i: (b, i, 0))
    bit = (lambda b, i: (b, 0, nb - 1 - i)) if rev else (lambda b, i: (b, 0, i))
    c2 = lambda b, i: (0, 0)
    st_spec = pl.BlockSpec((None, DN_HEADS, DN_HEAD_DIM, DN_HEAD_DIM), lambda b, i: (b, 0, 0, 0))
    st_shape = jax.ShapeDtypeStruct((bsz, DN_HEADS, DN_HEAD_DIM, DN_HEAD_DIM), F32)
    out_specs = [st_spec]
    out_shape = [st_shape]
    if need_out:
        out_specs = [pl.BlockSpec((None, blk_len, DN_WIDTH), bi)] + out_specs
        out_shape = [jax.ShapeDtypeStruct((bsz, t_len, DN_WIDTH), F32)] + out_shape
    return pl.pallas_call(
        functools.partial(_dn_kernel, blk_len=blk_len, rev=rev, need_out=need_out),
        grid=(bsz, nb),
        in_specs=[pl.BlockSpec((None, blk_len, ncols), bi),
                  pl.BlockSpec((None, blk_len, LANES), bi),
                  pl.BlockSpec((None, 32, blk_len), bit),
                  pl.BlockSpec((1, LANES), c2), pl.BlockSpec((1, LANES), c2),
                  pl.BlockSpec((32, 1), c2), pl.BlockSpec((32, 1), c2),
                  st_spec],
        out_specs=out_specs,
        out_shape=out_shape,
        scratch_shapes=[pltpu.VMEM((DN_HEADS, DN_HEAD_DIM, DN_HEAD_DIM), F32)],
        compiler_params=_params(("arbitrary", "arbitrary")),
        name="deltanet_bwd" if rev else "deltanet_fwd",
    )(kvq, ab, abt, acoef, dtb, acoeft, dtbt, s0)


def _post_kernel(x_ref, mod_ref, yf_ref, yb_ref, u_ref, of_ref, ob_ref, z_ref, g5_ref, gd_ref,
                 s5d_ref, glw_ref, glb_ref, dnw_ref, wb5_ref, wbd_ref, wo_ref, gain_ref, o_ref):
    y5 = _gelu(yf_ref[...] + yb_ref[...] + s5d_ref[...] * u_ref[...])
    y5 = y5 * _sigmoid(_dot(y5.astype(BF16), glw_ref[...]) + glb_ref[...])
    o = of_ref[...] + ob_ref[...]
    parts = []
    for h in range(DN_HEADS):
        oh = o[:, h * DN_HEAD_DIM:(h + 1) * DN_HEAD_DIM]
        parts.append(_rms(oh) * dnw_ref[...])
    y_dn = jnp.concatenate(parts, axis=-1) * _silu(z_ref[...])
    merged = (_sigmoid(g5_ref[...]) * _dot(y5.astype(BF16), wb5_ref[...])
              + _sigmoid(gd_ref[...]) * _dot(y_dn.astype(BF16), wbd_ref[...]))
    x_mix = _dot(merged.astype(BF16), wo_ref[...])
    o_ref[...] = x_ref[...] + mod_ref[2:3, :] * (_rms(x_mix) * gain_ref[...])


def _post(x, mods, yf, yb, u, o_f, o_b, gates, s5d, glw, glb, dnw, wb5, wbd, wo, gain, tm):
    bsz, t_len, d = x.shape
    tok = lambda w: pl.BlockSpec((None, tm, w), lambda b, i: (b, i, 0))
    colblk = lambda k: pl.BlockSpec((None, tm, d), lambda b, i, k=k: (b, i, k))
    full = lambda a: pl.BlockSpec(a.shape, lambda b, i: (0,) * a.ndim)
    return pl.pallas_call(
        _post_kernel,
        grid=(bsz, t_len // tm),
        in_specs=[tok(d), pl.BlockSpec((None, 6, d), lambda b, i: (b, 0, 0)),
                  tok(S5_WIDTH), tok(S5_WIDTH), tok(S5_WIDTH), tok(d), tok(d),
                  colblk(0), colblk(1), colblk(2),
                  full(s5d), full(glw), full(glb), full(dnw), full(wb5), full(wbd), full(wo), full(gain)],
        out_specs=tok(d),
        out_shape=jax.ShapeDtypeStruct(x.shape, F32),
        compiler_params=_params(("arbitrary", "arbitrary")),
        name="mixer_out",
    )(x, mods, yf, yb, u, o_f, o_b, gates, gates, gates, s5d, glw, glb, dnw, wb5, wbd, wo, gain)


_CAND_COUNTS = [PEER_TOPK // (a + 1) for a in range(PEER_TOPK)]
_N_CAND = sum(_CAND_COUNTS)
_CAND_ROWS = -(-_N_CAND // 8) * 8


def _extract_topk(work, k, out_vals_ref=None, want_rank=False, break_ties=True):
    nrows = work.shape[0]
    iota = lax.broadcasted_iota(jnp.int32, work.shape, 0)
    rank = jnp.full(work.shape, float(k), F32) if want_rank else None
    vals = []
    for r in range(k):
        m = jnp.max(work, axis=0, keepdims=True)
        if break_ties:
            idx = jnp.min(jnp.where(work == m, iota, nrows), axis=0, keepdims=True)
            sel = iota == idx
        else:
            sel = work == m
        work = jnp.where(sel, -jnp.inf, work)
        if want_rank:
            rank = jnp.where(sel, float(r), rank)
        vals.append(m)
        if out_vals_ref is not None:
            out_vals_ref[r:r + 1, :] = m
    return vals, work, rank


def _route_kernel(x_ref, mod_ref, gain_ref, wq_ref, keys_ref,
                  ht_ref, c1_ref, e1_ref, r2_ref, p2_ref, q_scr, sv_scr, cand_scr):
    h = _rms(x_ref[...]) * gain_ref[...]
    h = h * (1.0 + mod_ref[4:5, :]) + mod_ref[3:4, :]
    hb = h.astype(BF16)
    ht_ref[...] = h.T.astype(BF16)
    q_scr[...] = _dot(hb, wq_ref[...]).astype(BF16)
    topk = float(PEER_TOPK)

    def route_head(hd, scores, break_ties):
        svals = []
        ranks = []
        for j in range(2):
            vals, _, rank = _extract_topk(scores[j], PEER_TOPK, sv_scr.at[j], want_rank=(break_ties or j == 1),
                                          break_ties=break_ties)
            svals.append(vals)
            ranks.append(rank)
        if break_ties:
            is_rank = [ranks[0] == float(a) for a in range(PEER_TOPK)]
            member = [r < topk for r in ranks]
        else:
            is_rank = [scores[0] == svals[0][a] for a in range(PEER_TOPK)]
            member = [scores[0] >= svals[0][PEER_TOPK - 1], ranks[1] < topk]
        cand_scr[...] = jnp.full(cand_scr.shape, -jnp.inf, F32)
        off = 0
        for a in range(PEER_TOPK):
            nb = _CAND_COUNTS[a]
            cand_scr[off: off + nb, :] = svals[0][a] + sv_scr[1, 0:nb, :]
            off += nb
        top, left, _ = _extract_topk(cand_scr[...], PEER_TOPK, break_ties=break_ties)
        m0 = top[0]
        z = jnp.exp(top[0] - m0)
        for r in range(1, PEER_TOPK):
            z = z + jnp.exp(top[r] - m0)
        taken = (left == -jnp.inf).astype(F32)
        c1 = jnp.zeros(scores[0].shape, F32)
        n_taken = jnp.zeros_like(m0)
        off = 0
        for a in range(PEER_TOPK):
            nb = _CAND_COUNTS[a]
            cnt = jnp.sum(taken[off: off + nb, :], axis=0, keepdims=True)
            c1 = jnp.where(is_rank[a], cnt, c1)
            n_taken = n_taken + cnt
            off += nb
        c1_ref[hd] = c1
        r2_ref[hd] = ranks[1]
        e1_ref[hd] = jnp.where(member[0], jnp.exp(scores[0] - svals[0][0]), 0.0)
        p2_ref[hd] = jnp.where(member[1], jnp.exp(scores[1] - svals[1][0]), 0.0) / z
        n_member = [jnp.sum(mb.astype(F32), axis=0, keepdims=True) for mb in member]
        return jnp.abs(n_member[0] - topk) + jnp.abs(n_member[1] - topk) + jnp.abs(n_taken - topk)

    def head(hd, carry):
        scores = []
        for j in range(2):
            col = pl.multiple_of((hd * 2 + j) * PEER_KEY_DIM, PEER_KEY_DIM)
            scores.append(_dot_nt(keys_ref[hd, j], q_scr[:, pl.ds(col, PEER_KEY_DIM)]))
        excess = jnp.max(route_head(hd, scores, break_ties=False))

        @pl.when(excess > 0.0)
        def _():
            route_head(hd, scores, break_ties=True)

        return carry

    lax.fori_loop(0, PEER_HEADS, head, 0)


def _route(x1, mods, gain, wq, keys, tm):
    bsz, t_len, d = x1.shape
    n_tok = bsz * t_len
    nt = t_len // tm
    tile = lambda b, i: (0, 0, b * nt + i)
    hk = pl.BlockSpec((PEER_HEADS, PEER_N_KEYS, tm), tile)
    hk_shape = jax.ShapeDtypeStruct((PEER_HEADS, PEER_N_KEYS, n_tok), F32)
    return pl.pallas_call(
        _route_kernel,
        grid=(bsz, nt),
        in_specs=[pl.BlockSpec((None, tm, d), lambda b, i: (b, i, 0)),
                  pl.BlockSpec((None, 6, d), lambda b, i: (b, 0, 0)),
                  pl.BlockSpec((1, d), lambda b, i: (0, 0)),
                  pl.BlockSpec(wq.shape, lambda b, i: (0, 0)),
                  pl.BlockSpec(keys.shape, lambda b, i: (0, 0, 0, 0))],
        out_specs=[pl.BlockSpec((d, tm), lambda b, i: (0, b * nt + i)),
                   hk, hk, hk, hk],
        out_shape=[jax.ShapeDtypeStruct((d, n_tok), BF16),
                   hk_shape, hk_shape, hk_shape, hk_shape],
        scratch_shapes=[pltpu.VMEM((tm, wq.shape[1]), BF16),
                        pltpu.VMEM((2, PEER_TOPK, tm), F32),
                        pltpu.VMEM((_CAND_ROWS, tm), F32)],
        compiler_params=_params(("arbitrary", "arbitrary")),
        name="peer_route",
    )(x1, mods, gain, wq, keys)


def _peer_kernel(ht_ref, u_ref, vt_ref, c1_ref, e1_ref, r2_ref, p2_ref, x1_ref, mod_ref, gain_ref,
                 o_ref, acc, act_scr, a_scr, *, nsub):
    e = pl.program_id(2)

    @pl.when(e == 0)
    def _():
        acc[...] = jnp.zeros(acc.shape, F32)

    ncol = act_scr.shape[1] // LANES
    tr = 32
    nrt = PEER_N_KEYS // tr

    act_scr[...] = _dot(u_ref[...], ht_ref[...])
    for ct in range(ncol):
        cols = slice(ct * LANES, (ct + 1) * LANES)
        for rt in range(nrt):
            krows = slice(rt * tr, (rt + 1) * tr)
            ws = [None] * nsub
            for hd in range(PEER_HEADS):
                r2t = r2_ref[hd, krows, cols]
                p2t = p2_ref[hd, krows, cols]
                for ii in range(nsub):
                    contrib = jnp.where(r2t < c1_ref[hd, ii:ii + 1, cols], p2t * e1_ref[hd, ii:ii + 1, cols], 0.0)
                    ws[ii] = contrib if ws[ii] is None else ws[ii] + contrib
            for ii in range(nsub):
                rows = slice(ii * PEER_N_KEYS + rt * tr, ii * PEER_N_KEYS + (rt + 1) * tr)
                a_scr[rows, cols] = (_gelu(act_scr[rows, cols]) * ws[ii]).astype(BF16)
    acc[...] += _dot(vt_ref[...], a_scr[...])

    @pl.when(e == pl.num_programs(2) - 1)
    def _():
        out = acc[...].T
        o_ref[...] = x1_ref[...] + mod_ref[5:6, :] * (_rms(out) * gain_ref[...])


def _peer(ht, u_tab, v_tab, c1, e1, r2, p2, x1, mods, gain, tn, eb):
    bsz, t_len, d = x1.shape
    nt = t_len // tn
    n_exp = u_tab.shape[0]
    nsub = eb // PEER_N_KEYS
    ne = n_exp // eb
    n_tok = bsz * t_len
    vt_tab = v_tab.reshape(ne, eb, d).transpose(0, 2, 1)
    c1v = c1.reshape(PEER_HEADS, ne, nsub, n_tok)
    e1v = e1.reshape(PEER_HEADS, ne, nsub, n_tok)
    sub = pl.BlockSpec((PEER_HEADS, None, nsub, tn), lambda b, i, e: (0, e, 0, b * nt + i))
    hk = pl.BlockSpec((PEER_HEADS, PEER_N_KEYS, tn), lambda b, i, e: (0, 0, b * nt + i))
    return pl.pallas_call(
        functools.partial(_peer_kernel, nsub=nsub),
        grid=(bsz, nt, ne),
        in_specs=[pl.BlockSpec((d, tn), lambda b, i, e: (0, b * nt + i)),
                  pl.BlockSpec((eb, d), lambda b, i, e: (e, 0)),
                  pl.BlockSpec((None, d, eb), lambda b, i, e: (e, 0, 0)),
                  sub, sub, hk, hk,
                  pl.BlockSpec((None, tn, d), lambda b, i, e: (b, i, 0)),
                  pl.BlockSpec((None, 6, d), lambda b, i, e: (b, 0, 0)),
                  pl.BlockSpec((1, d), lambda b, i, e: (0, 0))],
        out_specs=pl.BlockSpec((None, tn, d), lambda b, i, e: (b, i, 0)),
        out_shape=jax.ShapeDtypeStruct(x1.shape, F32),
        scratch_shapes=[pltpu.VMEM((d, tn), F32), pltpu.VMEM((eb, tn), F32), pltpu.VMEM((eb, tn), BF16)],
        compiler_params=_params(("arbitrary", "arbitrary", "arbitrary")),
        name="peer_experts",
    )(ht, u_tab, vt_tab, c1v, e1v, r2, p2, x1, mods, gain)


def _s5_coeffs(lam_re, lam_im, log_step, b_re, b_im, c_re, c_im):
    step = jnp.exp(log_step)[:, None]
    mag = jnp.exp(lam_re * step)
    ab_re, ab_im = mag * jnp.cos(lam_im * step), mag * jnp.sin(lam_im * step)
    den = lam_re * lam_re + lam_im * lam_im
    f_re = ((ab_re - 1.0) * lam_re + ab_im * lam_im) / den
    f_im = (ab_im * lam_re - (ab_re - 1.0) * lam_im) / den
    bb_re = f_re[..., None] * b_re - f_im[..., None] * b_im
    bb_im = f_re[..., None] * b_im + f_im[..., None] * b_re
    lam = jnp.stack([ab_re.reshape(-1), ab_im.reshape(-1)])
    eye8 = jnp.eye(8, dtype=F32)

    def in_blockdiag(bb):
        t = bb.reshape(4, 8, S5_STATE, S5_GROUP).transpose(0, 1, 3, 2)
        return jnp.einsum('jghp,gk->jghkp', t, eye8).reshape(4, LANES, 8 * S5_STATE)

    def out_blockdiag(cc):
        t = cc.reshape(4, 8, S5_GROUP, S5_STATE).transpose(0, 1, 3, 2)
        return jnp.einsum('jgph,gk->jgpkh', t, eye8).reshape(4, 8 * S5_STATE, LANES)

    wb = jnp.concatenate([in_blockdiag(bb_re), in_blockdiag(bb_im)], axis=-1).astype(BF16)
    cmat = jnp.stack([out_blockdiag(c_re), -out_blockdiag(c_im)], axis=1).astype(BF16)
    return wb, lam, cmat


def _gate_rows(a_log, dt_bias):
    neg_a = -jnp.exp(a_log)
    acoef = jnp.zeros((LANES,), F32).at[0:8].set(neg_a[0]).at[16:24].set(neg_a[1])
    dtb = jnp.zeros((LANES,), F32).at[0:8].set(dt_bias[0]).at[16:24].set(dt_bias[1])
    return acoef[None, :], dtb[None, :], acoef[:32, None], dtb[:32, None]


def kernel(x, c, ctx, c_ctx, w_mod, b_mod, norm_pre_mix, norm_post_mix, norm_pre_ffn, norm_post_ffn, w_in, s5_lam_re, s5_lam_im, s5_log_step, s5_b_re, s5_b_im, s5_c_re, s5_c_im, s5_d, s5_glu_w, s5_glu_b, dn_conv_w, dn_a_log, dn_dt_bias, dn_norm_w, w_branch_s5, w_branch_dn, w_out, peer_w_q, peer_sub_keys, peer_u, peer_v):
    bsz, t_len, d = x.shape
    ctx_len = ctx.shape[1]
    l = 0

    pad = (-(bsz + 1)) % 8
    cv = jnp.concatenate([c, c_ctx[None, :], jnp.zeros((pad, d), F32)], axis=0)
    mod_all = _adaln(cv, w_mod[l].astype(BF16), b_mod[l][None, :])
    mods = mod_all[:bsz].reshape(bsz, 6, d)
    mods_ctx = jnp.broadcast_to(mod_all[bsz].reshape(1, 6, d), (bsz, 6, d))

    w = w_in[l]
    w_kvq = jnp.concatenate([w[:, COL_K:COL_AB], w[:, COL_Q:COL_Z]], axis=1).astype(BF16)
    w_gates = w[:, COL_Z:].astype(BF16)
    w_ctx = w[:, COL_K:COL_AB].astype(BF16)
    w_u = w[:, COL_U:COL_K].astype(BF16)
    w_ab32 = w[:, COL_AB:COL_Q]
    w_ab = jnp.pad(w_ab32, ((0, 0), (0, LANES - 32))).astype(BF16)
    w_abt = w_ab32.T.astype(BF16)
    gain_pre = norm_pre_mix[l][None, :]
    conv_w = dn_conv_w[l]

    kv_c, _, u_c, ab_c, abt_c = _inproj(ctx, mods_ctx, gain_pre, w_ctx, conv_w[:, :2 * DN_WIDTH], None,
                                        w_u, w_ab, w_abt, tm=ctx_len, tn=2 * DN_WIDTH, row_len=ctx_len)
    kvq_x, gates_x, u_x, ab_x, abt_x = _inproj(x, mods, gain_pre, w_kvq, conv_w, w_gates, w_u, w_ab, w_abt,
                                               tm=min(1024, t_len), tn=1536, row_len=GRID_W)

    wbf, lamf, cf = _s5_coeffs(s5_lam_re[l, 0], s5_lam_im[l, 0], s5_log_step[l, 0], s5_b_re[l, 0], s5_b_im[l, 0],
                               s5_c_re[l, 0], s5_c_im[l, 0])
    wbb, lamb, cb = _s5_coeffs(s5_lam_re[l, 1], s5_lam_im[l, 1], s5_log_step[l, 1], s5_b_re[l, 1], s5_b_im[l, 1],
                               s5_c_re[l, 1], s5_c_im[l, 1])
    zeros_h = jnp.zeros((bsz, 2 * S5_COLS), F32)
    tc = 32
    hf_c, hb_c = _s5(u_c, wbf, wbb, lamf, lamb, cf, cb, zeros_h, zeros_h, tc, False)
    yf, yb, _, _ = _s5(u_x, wbf, wbb, lamf, lamb, cf, cb, hf_c, hb_c, tc, True)

    acoef, dtb, acoeft, dtbt = _gate_rows(dn_a_log[l], dn_dt_bias[l])
    zeros_s = jnp.zeros((bsz, DN_HEADS, DN_HEAD_DIM, DN_HEAD_DIM), F32)
    dn_args = (acoef, dtb, acoeft, dtbt)
    (sf_c,) = _deltanet(kv_c, ab_c, abt_c, *dn_args, zeros_s, ctx_len, False, False)
    (sb_c,) = _deltanet(kv_c, ab_c, abt_c, *dn_args, zeros_s, ctx_len, True, False)
    blk_len = min(256, t_len)
    o_f, _ = _deltanet(kvq_x, ab_x, abt_x, *dn_args, sf_c, blk_len, False, True)
    o_b, _ = _deltanet(kvq_x, ab_x, abt_x, *dn_args, sb_c, blk_len, True, True)

    x1 = _post(x, mods, yf, yb, u_x, o_f, o_b, gates_x,
               s5_d[l][None, :], s5_glu_w[l].astype(BF16), s5_glu_b[l][None, :], dn_norm_w[l][None, :],
               w_branch_s5[l].astype(BF16), w_branch_dn[l].astype(BF16), w_out[l].astype(BF16),
               norm_post_mix[l][None, :], tm=min(512, t_len))

    ht, c1, e1, r2, p2 = _route(x1, mods, norm_pre_ffn[l][None, :], peer_w_q[l].astype(BF16),
                                peer_sub_keys[l].astype(BF16), tm=min(256, t_len))
    out = _peer(ht, peer_u[l].astype(BF16), peer_v[l].astype(BF16), c1, e1, r2, p2, x1, mods,
                norm_post_ffn[l][None, :], tn=min(1024, t_len), eb=512)
    return out
```

```python
import functools
import math

import jax
import jax.numpy as jnp
from jax import lax
from jax.experimental import pallas as pl
from jax.experimental.pallas import tpu as pltpu

F32 = jnp.float32
BF16 = jnp.bfloat16

D_MODEL = 1024
NORM_EPS = 1e-6
GRID_W = 64
S5_WIDTH = 512
S5_GROUP = 16
S5_GROUPS = 32
S5_STATE = 64
S5_COLS = S5_GROUPS * S5_STATE
DN_HEADS = 8
DN_HEAD_DIM = 128
DN_WIDTH = 1024
DN_CONV = 5
DN_CHUNK = 64
PEER_HEADS = 8
PEER_N_KEYS = 128
PEER_KEY_DIM = 128
PEER_TOPK = 16
PEER_EXPERT_BLOCK = 512
PEER_SUB = PEER_EXPERT_BLOCK // PEER_N_KEYS
LANES = 128

COL_U = 0
COL_K = 512
COL_V = 1536
COL_AB = 2560
COL_Q = 2592
COL_Z = 3616
COL_G5 = 4640
COL_GD = 5664

VMEM_LIMIT = 56 * 1024 * 1024


def _dot(a, b):
    return jnp.dot(a, b, preferred_element_type=F32)


def _dot_nt(a, b):
    return lax.dot_general(a, b, (((1,), (1,)), ((), ())), preferred_element_type=F32)


def _dot_tn(a, b):
    return lax.dot_general(a, b, (((0,), (0,)), ((), ())), preferred_element_type=F32)


def _sigmoid(x):
    return jax.nn.sigmoid(x)


def _silu(x):
    return x * _sigmoid(x)


def _gelu(x):
    c = math.sqrt(2.0 / math.pi)
    return x * (0.5 * (1.0 + jnp.tanh(c * (x + 0.044715 * (x * x * x)))))


def _softplus(x):
    return jnp.maximum(x, 0.0) + jnp.log1p(jnp.exp(-jnp.abs(x)))


def _rms(x):
    return x * lax.rsqrt(jnp.mean(x * x, axis=-1, keepdims=True) + NORM_EPS)


def _params(sem):
    return pltpu.CompilerParams(dimension_semantics=sem, vmem_limit_bytes=VMEM_LIMIT)


def _adaln_kernel(c_ref, w_ref, b_ref, o_ref):
    c = c_ref[...]
    o_ref[...] = _dot(_silu(c).astype(BF16), w_ref[...]) + b_ref[...]


def _adaln(cv, w_mod, b_mod):
    rows = cv.shape[0]
    ncol = w_mod.shape[1] // D_MODEL
    return pl.pallas_call(
        _adaln_kernel,
        grid=(ncol,),
        in_specs=[pl.BlockSpec((rows, D_MODEL), lambda j: (0, 0)),
                  pl.BlockSpec((D_MODEL, D_MODEL), lambda j: (0, j)),
                  pl.BlockSpec((1, D_MODEL), lambda j: (0, j))],
        out_specs=pl.BlockSpec((rows, D_MODEL), lambda j: (0, j)),
        out_shape=jax.ShapeDtypeStruct((rows, w_mod.shape[1]), F32),
        compiler_params=_params(("arbitrary",)),
        name="adaln",
    )(cv, w_mod, b_mod)


def _inproj_kernel(x_ref, mod_ref, gain_ref, wc_ref, cw_ref, *rest, ncs, row_len, has_gates):
    if has_gates:
        wg_ref, wu_ref, wab_ref, wabt_ref, kvq_ref, gates_ref, u_ref, ab_ref, abt_ref, hx_scr = rest
    else:
        wu_ref, wab_ref, wabt_ref, kvq_ref, u_ref, ab_ref, abt_ref, hx_scr = rest
    j = pl.program_id(2)

    @pl.when(j == 0)
    def _():
        h = _rms(x_ref[...]) * gain_ref[...]
        h = h * (1.0 + mod_ref[1:2, :]) + mod_ref[0:1, :]
        hb = h.astype(BF16)
        hx_scr[...] = hb
        u_ref[...] = _dot(hb, wu_ref[...])
        ab_ref[...] = _dot(hb, wab_ref[...])
        abt_ref[...] = _dot_nt(wabt_ref[...], hb)

    @pl.when(j < ncs)
    def _():
        kvq_ref[...] = _dot(hx_scr[...], wc_ref[...])
        tm, tn = kvq_ref.shape
        tpos = lax.broadcasted_iota(jnp.int32, (tm, LANES), 0) % row_len
        valid = [(tpos + (t - 2) >= 0) & (tpos + (t - 2) < row_len) for t in range(DN_CONV)]
        for cbk in range(tn // LANES):
            cols = slice(cbk * LANES, (cbk + 1) * LANES)
            xc = kvq_ref[:, cols]
            acc = xc * cw_ref[2:3, cols]
            for t in (0, 1, 3, 4):
                acc = acc + jnp.where(valid[t], pltpu.roll(xc, (2 - t) % tm, axis=0), 0.0) * cw_ref[t:t + 1, cols]
            kvq_ref[:, cols] = _silu(acc)

    if has_gates:
        @pl.when(j >= ncs)
        def _():
            gates_ref[...] = _dot(hx_scr[...], wg_ref[...])


def _inproj(x, mods, gain, w_conv, conv_w, w_gate, w_u, w_ab, w_abt, tm, tn, row_len):
    bsz, t_len, d = x.shape
    cc = w_conv.shape[1]
    ncs = cc // tn
    has_gates = w_gate is not None
    ngs = (w_gate.shape[1] // tn) if has_gates else 0
    cstep = lambda j: jnp.minimum(j, ncs - 1)
    gstep = lambda j: jnp.maximum(j - ncs, 0)
    in_specs = [pl.BlockSpec((None, tm, d), lambda b, i, j: (b, i, 0)),
                pl.BlockSpec((None, 6, d), lambda b, i, j: (b, 0, 0)),
                pl.BlockSpec((1, d), lambda b, i, j: (0, 0)),
                pl.BlockSpec((d, tn), lambda b, i, j: (0, cstep(j))),
                pl.BlockSpec((DN_CONV, tn), lambda b, i, j: (0, cstep(j)))]
    args = [x, mods, gain, w_conv, conv_w]
    out_specs = [pl.BlockSpec((None, tm, tn), lambda b, i, j: (b, i, cstep(j)))]
    out_shape = [jax.ShapeDtypeStruct((bsz, t_len, cc), F32)]
    if has_gates:
        in_specs.append(pl.BlockSpec((d, tn), lambda b, i, j: (0, gstep(j))))
        args.append(w_gate)
        out_specs.append(pl.BlockSpec((None, tm, tn), lambda b, i, j: (b, i, gstep(j))))
        out_shape.append(jax.ShapeDtypeStruct((bsz, t_len, w_gate.shape[1]), F32))
    in_specs += [pl.BlockSpec((d, S5_WIDTH), lambda b, i, j: (0, 0)),
                 pl.BlockSpec((d, LANES), lambda b, i, j: (0, 0)),
                 pl.BlockSpec((32, d), lambda b, i, j: (0, 0))]
    args += [w_u, w_ab, w_abt]
    out_specs += [pl.BlockSpec((None, tm, S5_WIDTH), lambda b, i, j: (b, i, 0)),
                  pl.BlockSpec((None, tm, LANES), lambda b, i, j: (b, i, 0)),
                  pl.BlockSpec((None, 32, tm), lambda b, i, j: (b, 0, i))]
    out_shape += [jax.ShapeDtypeStruct((bsz, t_len, S5_WIDTH), F32),
                  jax.ShapeDtypeStruct((bsz, t_len, LANES), F32),
                  jax.ShapeDtypeStruct((bsz, 32, t_len), F32)]
    outs = pl.pallas_call(
        functools.partial(_inproj_kernel, ncs=ncs, row_len=row_len, has_gates=has_gates),
        grid=(bsz, t_len // tm, ncs + ngs),
        in_specs=in_specs,
        out_specs=out_specs,
        out_shape=out_shape,
        scratch_shapes=[pltpu.VMEM((tm, d), BF16)],
        compiler_params=_params(("arbitrary", "arbitrary", "arbitrary")),
        name="inproj",
    )(*args)
    if has_gates:
        return outs
    return (outs[0], None) + tuple(outs[1:])


def _s5_kernel(uf_ref, ub_ref, wbf_ref, wbb_ref, lamf_ref, lamb_ref, cf_ref, cb_ref,
               h0f_ref, h0b_ref, *rest, tc, bsz, need_out):
    if need_out:
        yf_ref, yb_ref, hff_ref, hfb_ref, xsf, xsb, hcf, hcb = rest
    else:
        hff_ref, hfb_ref, xsf, xsb, hcf, hcb = rest
        yf_ref = yb_ref = None
    i = pl.program_id(0)
    half = S5_COLS
    cw = 512
    nblk = half // cw

    @pl.when(i == 0)
    def _():
        hcf[...] = h0f_ref[...]
        hcb[...] = h0b_ref[...]

    dirs = ((uf_ref, wbf_ref, lamf_ref, cf_ref, yf_ref, xsf, hcf, False),
            (ub_ref, wbb_ref, lamb_ref, cb_ref, yb_ref, xsb, hcb, True))
    for u_ref, wb_ref, lam_ref, c_ref, y_ref, xs, hc, rev in dirs:
        for j in range(4):
            lhs = jnp.concatenate(
                [u_ref[:, t, j * LANES:(j + 1) * LANES] for t in range(tc)],
                axis=0).astype(BF16)
            r = _dot(lhs, wb_ref[j])
            xs[:, j * cw:(j + 1) * cw] = r[:, :cw]
            xs[:, half + j * cw: half + (j + 1) * cw] = r[:, cw:]
        for cb in range(nblk):
            lo, hi = cb * cw, (cb + 1) * cw
            ar = jnp.broadcast_to(lam_ref[0:1, lo:hi], (bsz, cw))
            ai = jnp.broadcast_to(lam_ref[1:2, lo:hi], (bsz, cw))

            def body(s, carry, lo=lo, hi=hi, ar=ar, ai=ai, xs=xs, rev=rev):
                hr, hi_ = carry
                t = (tc - 1 - s) if rev else s
                row = pl.multiple_of(t * bsz, bsz)
                xr = xs[pl.ds(row, bsz), lo:hi]
                xi = xs[pl.ds(row, bsz), half + lo: half + hi]
                nr = ar * hr - ai * hi_ + xr
                ni = ar * hi_ + ai * hr + xi
                xs[pl.ds(row, bsz), lo:hi] = nr
                xs[pl.ds(row, bsz), half + lo: half + hi] = ni
                return nr, ni

            hr, hi_ = lax.fori_loop(0, tc, body, (hc[:, lo:hi], hc[:, half + lo: half + hi]))
            hc[:, lo:hi] = hr
            hc[:, half + lo: half + hi] = hi_
        if need_out:
            for j in range(4):
                hre = xs[:, j * cw:(j + 1) * cw].astype(BF16)
                him = xs[:, half + j * cw: half + (j + 1) * cw].astype(BF16)
                y = _dot(hre, c_ref[j, 0]) + _dot(him, c_ref[j, 1])
                for t in range(tc):
                    y_ref[:, t, j * LANES:(j + 1) * LANES] = y[t * bsz:(t + 1) * bsz]
    hff_ref[...] = hcf[...]
    hfb_ref[...] = hcb[...]


def _s5(u, wbf, wbb, lamf, lamb, cf, cb, h0f, h0b, tc, need_out):
    bsz, t_len, _ = u.shape
    nc = t_len // tc
    const2 = lambda i: (0, 0)
    const3 = lambda i: (0, 0, 0)
    const4 = lambda i: (0, 0, 0, 0)
    st_spec = pl.BlockSpec((bsz, 2 * S5_COLS), const2)
    st_shape = jax.ShapeDtypeStruct((bsz, 2 * S5_COLS), F32)
    out_specs = [st_spec, st_spec]
    out_shape = [st_shape, st_shape]
    fwd_blk = pl.BlockSpec((bsz, tc, S5_WIDTH), lambda i: (0, i, 0))
    bwd_blk = pl.BlockSpec((bsz, tc, S5_WIDTH), lambda i: (0, nc - 1 - i, 0))
    if need_out:
        y_shape = jax.ShapeDtypeStruct(u.shape, F32)
        out_specs = [fwd_blk, bwd_blk] + out_specs
        out_shape = [y_shape, y_shape] + out_shape
    return pl.pallas_call(
        functools.partial(_s5_kernel, tc=tc, bsz=bsz, need_out=need_out),
        grid=(nc,),
        in_specs=[fwd_blk, bwd_blk,
                  pl.BlockSpec(wbf.shape, const3), pl.BlockSpec(wbb.shape, const3),
                  pl.BlockSpec(lamf.shape, const2), pl.BlockSpec(lamb.shape, const2),
                  pl.BlockSpec(cf.shape, const4), pl.BlockSpec(cb.shape, const4),
                  st_spec, st_spec],
        out_specs=out_specs,
        out_shape=out_shape,
        scratch_shapes=[pltpu.VMEM((tc * bsz, 2 * S5_COLS), F32),
                        pltpu.VMEM((tc * bsz, 2 * S5_COLS), F32),
                        pltpu.VMEM((bsz, 2 * S5_COLS), F32),
                        pltpu.VMEM((bsz, 2 * S5_COLS), F32)],
        compiler_params=_params(("arbitrary",)),
        name="s5_scan",
    )(u, u, wbf, wbb, lamf, lamb, cf, cb, h0f, h0b)


def _dn_kernel(act_scr, ab_ref, abt_ref, acoef_ref, dtb_ref, acoeft_ref, dtbt_ref, s0_ref,
               *rest, blk_len, rev, need_out):
    if need_out:
        o_ref, sfin_ref, s_scr = rest
    else:
        sfin_ref, s_scr = rest
        o_ref = None
    cs = DN_CHUNK
    nch = blk_len // cs
    goff = 16 if rev else 0
    boff = goff + 8

    @pl.when(pl.program_id(1) == 0)
    def _():
        s_scr[...] = s0_ref[...]

    ab = ab_ref[...]
    g_cols = acoef_ref[...] * _softplus(ab + dtb_ref[...])
    beta_cols = _sigmoid(ab)
    g_rows = acoeft_ref[...] * _softplus(abt_ref[...] + dtbt_ref[...])

    ri = lax.broadcasted_iota(jnp.int32, (cs, cs), 0)
    ci = lax.broadcasted_iota(jnp.int32, (cs, cs), 1)
    incl = (ri <= ci) if rev else (ri >= ci)
    strict = incl & (ri != ci)
    tri = incl.astype(F32)
    eye = (ri == ci).astype(F32)
    last = 0 if rev else cs - 1
    scale = DN_HEAD_DIM ** -0.5

    order = list(range(nch - 1, -1, -1)) if rev else list(range(nch))
    units = [(c, h) for c in order for h in range(DN_HEADS)]
    nu = len(units)
    gc_cols = {}
    gc_rows = {}
    for c in order:
        rows = slice(c * cs, (c + 1) * cs)
        gc_cols[c] = jnp.dot(tri, g_cols[rows, :], precision=lax.Precision.HIGHEST, preferred_element_type=F32)
        gc_rows[c] = lax.dot_general(g_rows[:, rows], tri, (((1,), (1,)), ((), ())),
                                     precision=lax.Precision.HIGHEST, preferred_element_type=F32)

    def head_block(base, c, h):
        return act_scr[c * cs:(c + 1) * cs, base + h * DN_HEAD_DIM: base + (h + 1) * DN_HEAD_DIM]

    def l2n(t):
        return t * lax.rsqrt(jnp.sum(t * t, axis=-1, keepdims=True) + 1e-6)

    gch = [gc_cols[c][:, goff + h: goff + h + 1] for c, h in units]
    gcr = [gc_rows[c][goff + h: goff + h + 1, :] for c, h in units]
    dec = [jnp.where(incl, jnp.exp(jnp.minimum(gch[u] - gcr[u], 0.0)), 0.0) for u in range(nu)]
    bh = [beta_cols[c * cs:(c + 1) * cs, boff + h: boff + h + 1] for c, h in units]
    k = [l2n(head_block(0, c, h)) for c, h in units]
    kbf = [t.astype(BF16) for t in k]
    kb = [k[u] * bh[u] for u in range(nu)]
    kkt = [_dot_nt(kb[u].astype(BF16), kbf[u]) for u in range(nu)]
    a = [jnp.where(strict, kkt[u] * dec[u], 0.0) for u in range(nu)]
    abf = [t.astype(BF16) for t in a]
    m = [_dot(t, t) for t in abf]
    p = [eye - t for t in a]
    for lvl in range(5):
        mb = [t.astype(BF16) for t in m]
        if lvl < 4:
            r = [_dot(jnp.concatenate([m[u], p[u]], axis=0).astype(BF16), mb[u]) for u in range(nu)]
            m = [t[:cs] for t in r]
            p = [p[u] + r[u][cs:] for u in range(nu)]
        else:
            r = [_dot(p[u].astype(BF16), mb[u]) for u in range(nu)]
            p = [p[u] + r[u] for u in range(nu)]
    eg = [jnp.exp(t) for t in gch]
    rhs = [jnp.concatenate([head_block(DN_WIDTH, c, h) * bh[u], kb[u] * eg[u]], axis=-1).astype(BF16)
           for u, (c, h) in enumerate(units)]
    uw = [_dot(p[u].astype(BF16), rhs[u]) for u in range(nu)]
    gl = [t[last:last + 1, :] for t in gch]
    kd = [(k[u] * jnp.exp(gl[u] - gch[u])).astype(BF16) for u in range(nu)]
    if need_out:
        q = [l2n(head_block(2 * DN_WIDTH, c, h)) * scale for c, h in units]
        qkt = [_dot_nt(q[u].astype(BF16), kbf[u]) for u in range(nu)]
        at = [jnp.where(incl, qkt[u] * dec[u], 0.0).astype(BF16) for u in range(nu)]
        qd = [(q[u] * eg[u]).astype(BF16) for u in range(nu)]

    for ci, c in enumerate(order):
        us = range(ci * DN_HEADS, (ci + 1) * DN_HEADS)
        s_old = [s_scr[h] for h in range(DN_HEADS)]
        s_bf = [t.astype(BF16) for t in s_old]
        ws = [_dot(uw[u][:, DN_HEAD_DIM:].astype(BF16), s_bf[h]) for h, u in enumerate(us)]
        if need_out:
            qs = [_dot(qd[u], s_bf[h]) for h, u in enumerate(us)]
        vn = [(uw[u][:, :DN_HEAD_DIM] - ws[h]).astype(BF16) for h, u in enumerate(us)]
        if need_out:
            av = [_dot(at[u], vn[h]) for h, u in enumerate(us)]
        kv = [_dot_tn(kd[u], vn[h]) for h, u in enumerate(us)]
        for h, u in enumerate(us):
            if need_out:
                o_ref[c * cs:(c + 1) * cs, h * DN_HEAD_DIM:(h + 1) * DN_HEAD_DIM] = qs[h] + av[h]
            s_scr[h] = s_old[h] * jnp.exp(gl[u]) + kv[h]
    sfin_ref[...] = s_scr[...]


def _deltanet(kvq, ab, abt, acoef, dtb, acoeft, dtbt, s0, blk_len, rev, need_out):
    bsz, t_len, ncols = kvq.shape
    nb = t_len // blk_len
    bi = (lambda b, i: (b, nb - 1 - i, 0)) if rev else (lambda b, i: (b, i, 0))
    bit = (lambda b, i: (b, 0, nb - 1 - i)) if rev else (lambda b, i: (b, 0, i))
    c2 = lambda b, i: (0, 0)
    st_spec = pl.BlockSpec((None, DN_HEADS, DN_HEAD_DIM, DN_HEAD_DIM), lambda b, i: (b, 0, 0, 0))
    st_shape = jax.ShapeDtypeStruct((bsz, DN_HEADS, DN_HEAD_DIM, DN_HEAD_DIM), F32)
    out_specs = [st_spec]
    out_shape = [st_shape]
    if need_out:
        out_specs = [pl.BlockSpec((None, blk_len, DN_WIDTH), bi)] + out_specs
        out_shape = [jax.ShapeDtypeStruct((bsz, t_len, DN_WIDTH), F32)] + out_shape
    return pl.pallas_call(
        functools.partial(_dn_kernel, blk_len=blk_len, rev=rev, need_out=need_out),
        grid=(bsz, nb),
        in_specs=[pl.BlockSpec((None, blk_len, ncols), bi),
                  pl.BlockSpec((None, blk_len, LANES), bi),
                  pl.BlockSpec((None, 32, blk_len), bit),
                  pl.BlockSpec((1, LANES), c2), pl.BlockSpec((1, LANES), c2),
                  pl.BlockSpec((32, 1), c2), pl.BlockSpec((32, 1), c2),
                  st_spec],
        out_specs=out_specs,
        out_shape=out_shape,
        scratch_shapes=[pltpu.VMEM((DN_HEADS, DN_HEAD_DIM, DN_HEAD_DIM), F32)],
        compiler_params=_params(("arbitrary", "arbitrary")),
        name="deltanet_bwd" if rev else "deltanet_fwd",
    )(kvq, ab, abt, acoef, dtb, acoeft, dtbt, s0)


def _post_kernel(x_ref, mod_ref, yf_ref, yb_ref, u_ref, of_ref, ob_ref, z_ref, g5_ref, gd_ref,
                 s5d_ref, glw_ref, glb_ref, dnw_ref, wb5_ref, wbd_ref, wo_ref, gain_ref, o_ref):
    y5 = _gelu(yf_ref[...] + yb_ref[...] + s5d_ref[...] * u_ref[...])
    y5 = y5 * _sigmoid(_dot(y5.astype(BF16), glw_ref[...]) + glb_ref[...])
    o = of_ref[...] + ob_ref[...]
    parts = []
    for h in range(DN_HEADS):
        oh = o[:, h * DN_HEAD_DIM:(h + 1) * DN_HEAD_DIM]
        parts.append(_rms(oh) * dnw_ref[...])
    y_dn = jnp.concatenate(parts, axis=-1) * _silu(z_ref[...])
    merged = (_sigmoid(g5_ref[...]) * _dot(y5.astype(BF16), wb5_ref[...])
              + _sigmoid(gd_ref[...]) * _dot(y_dn.astype(BF16), wbd_ref[...]))
    x_mix = _dot(merged.astype(BF16), wo_ref[...])
    o_ref[...] = x_ref[...] + mod_ref[2:3, :] * (_rms(x_mix) * gain_ref[...])


def _post(x, mods, yf, yb, u, o_f, o_b, gates, s5d, glw, glb, dnw, wb5, wbd, wo, gain, tm):
    bsz, t_len, d = x.shape
    tok = lambda w: pl.BlockSpec((None, tm, w), lambda b, i: (b, i, 0))
    colblk = lambda k: pl.BlockSpec((None, tm, d), lambda b, i, k=k: (b, i, k))
    full = lambda a: pl.BlockSpec(a.shape, lambda b, i: (0,) * a.ndim)
    return pl.pallas_call(
        _post_kernel,
        grid=(bsz, t_len // tm),
        in_specs=[tok(d), pl.BlockSpec((None, 6, d), lambda b, i: (b, 0, 0)),
                  tok(S5_WIDTH), tok(S5_WIDTH), tok(S5_WIDTH), tok(d), tok(d),
                  colblk(0), colblk(1), colblk(2),
                  full(s5d), full(glw), full(glb), full(dnw), full(wb5), full(wbd), full(wo), full(gain)],
        out_specs=tok(d),
        out_shape=jax.ShapeDtypeStruct(x.shape, F32),
        compiler_params=_params(("arbitrary", "arbitrary")),
        name="mixer_out",
    )(x, mods, yf, yb, u, o_f, o_b, gates, gates, gates, s5d, glw, glb, dnw, wb5, wbd, wo, gain)


_CAND_COUNTS = [PEER_TOPK // (a + 1) for a in range(PEER_TOPK)]
_N_CAND = sum(_CAND_COUNTS)
_CAND_ROWS = -(-_N_CAND // 8) * 8


def _extract_topk(work, k, out_vals_ref=None, want_rank=False, break_ties=True):
    nrows = work.shape[0]
    iota = lax.broadcasted_iota(jnp.int32, work.shape, 0)
    rank = jnp.full(work.shape, float(k), F32) if want_rank else None
    vals = []
    for r in range(k):
        m = jnp.max(work, axis=0, keepdims=True)
        if break_ties:
            idx = jnp.min(jnp.where(work == m, iota, nrows), axis=0, keepdims=True)
            sel = iota == idx
        else:
            sel = work == m
        work = jnp.where(sel, -jnp.inf, work)
        if want_rank:
            rank = jnp.where(sel, float(r), rank)
        vals.append(m)
        if out_vals_ref is not None:
            out_vals_ref[r:r + 1, :] = m
    return vals, work, rank


def _route_kernel(x_ref, mod_ref, gain_ref, wq_ref, keys_ref,
                  ht_ref, c1_ref, e1_ref, r2_ref, p2_ref, q_scr, sv_scr, cand_scr):
    h = _rms(x_ref[...]) * gain_ref[...]
    h = h * (1.0 + mod_ref[4:5, :]) + mod_ref[3:4, :]
    hb = h.astype(BF16)
    ht_ref[...] = h.T.astype(BF16)
    q_scr[...] = _dot(hb, wq_ref[...]).astype(BF16)
    topk = float(PEER_TOPK)

    def route_head(hd, scores, break_ties):
        svals = []
        ranks = []
        for j in range(2):
            vals, _, rank = _extract_topk(scores[j], PEER_TOPK, sv_scr.at[j], want_rank=(break_ties or j == 1),
                                          break_ties=break_ties)
            svals.append(vals)
            ranks.append(rank)
        if break_ties:
            is_rank = [ranks[0] == float(a) for a in range(PEER_TOPK)]
            member = [r < topk for r in ranks]
        else:
            is_rank = [scores[0] == svals[0][a] for a in range(PEER_TOPK)]
            member = [scores[0] >= svals[0][PEER_TOPK - 1], ranks[1] < topk]
        cand_scr[...] = jnp.full(cand_scr.shape, -jnp.inf, F32)
        off = 0
        for a in range(PEER_TOPK):
            nb = _CAND_COUNTS[a]
            cand_scr[off: off + nb, :] = svals[0][a] + sv_scr[1, 0:nb, :]
            off += nb
        top, left, _ = _extract_topk(cand_scr[...], PEER_TOPK, break_ties=break_ties)
        m0 = top[0]
        z = jnp.exp(top[0] - m0)
        for r in range(1, PEER_TOPK):
            z = z + jnp.exp(top[r] - m0)
        taken = (left == -jnp.inf).astype(F32)
        c1 = jnp.zeros(scores[0].shape, F32)
        n_taken = jnp.zeros_like(m0)
        off = 0
        for a in range(PEER_TOPK):
            nb = _CAND_COUNTS[a]
            cnt = jnp.sum(taken[off: off + nb, :], axis=0, keepdims=True)
            c1 = jnp.where(is_rank[a], cnt, c1)
            n_taken = n_taken + cnt
            off += nb
        e1 = jnp.where(member[0], jnp.exp(scores[0] - svals[0][0]), 0.0)
        for g in range(PEER_N_KEYS // PEER_SUB):
            c1_ref[hd, g] = c1[g * PEER_SUB:(g + 1) * PEER_SUB]
            e1_ref[hd, g] = e1[g * PEER_SUB:(g + 1) * PEER_SUB]
        r2_ref[hd] = ranks[1]
        p2_ref[hd] = jnp.where(member[1], jnp.exp(scores[1] - svals[1][0]), 0.0) / z
        n_member = [jnp.sum(mb.astype(F32), axis=0, keepdims=True) for mb in member]
        return jnp.abs(n_member[0] - topk) + jnp.abs(n_member[1] - topk) + jnp.abs(n_taken - topk)

    def head(hd, carry):
        scores = []
        for j in range(2):
            col = pl.multiple_of((hd * 2 + j) * PEER_KEY_DIM, PEER_KEY_DIM)
            scores.append(_dot_nt(keys_ref[hd, j], q_scr[:, pl.ds(col, PEER_KEY_DIM)]))
        excess = jnp.max(route_head(hd, scores, break_ties=False))

        @pl.when(excess > 0.0)
        def _():
            route_head(hd, scores, break_ties=True)

        return carry

    lax.fori_loop(0, PEER_HEADS, head, 0)


def _route(x1, mods, gain, wq, keys, tm):
    bsz, t_len, d = x1.shape
    n_tok = bsz * t_len
    nt = t_len // tm
    tile = lambda b, i: (0, 0, b * nt + i)
    hk = pl.BlockSpec((PEER_HEADS, PEER_N_KEYS, tm), tile)
    hk_shape = jax.ShapeDtypeStruct((PEER_HEADS, PEER_N_KEYS, n_tok), F32)
    nblk = PEER_N_KEYS // PEER_SUB
    sub = pl.BlockSpec((PEER_HEADS, nblk, PEER_SUB, tm), lambda b, i: (0, 0, 0, b * nt + i))
    sub_shape = jax.ShapeDtypeStruct((PEER_HEADS, nblk, PEER_SUB, n_tok), F32)
    return pl.pallas_call(
        _route_kernel,
        grid=(bsz, nt),
        in_specs=[pl.BlockSpec((None, tm, d), lambda b, i: (b, i, 0)),
                  pl.BlockSpec((None, 6, d), lambda b, i: (b, 0, 0)),
                  pl.BlockSpec((1, d), lambda b, i: (0, 0)),
                  pl.BlockSpec(wq.shape, lambda b, i: (0, 0)),
                  pl.BlockSpec(keys.shape, lambda b, i: (0, 0, 0, 0))],
        out_specs=[pl.BlockSpec((d, tm), lambda b, i: (0, b * nt + i)),
                   sub, sub, hk, hk],
        out_shape=[jax.ShapeDtypeStruct((d, n_tok), BF16),
                   sub_shape, sub_shape, hk_shape, hk_shape],
        scratch_shapes=[pltpu.VMEM((tm, wq.shape[1]), BF16),
                        pltpu.VMEM((2, PEER_TOPK, tm), F32),
                        pltpu.VMEM((_CAND_ROWS, tm), F32)],
        compiler_params=_params(("arbitrary", "arbitrary")),
        name="peer_route",
    )(x1, mods, gain, wq, keys)


def _peer_kernel(ht_ref, u_ref, vt_ref, c1_ref, e1_ref, r2_ref, p2_ref, x1_ref, mod_ref, gain_ref,
                 o_ref, acc, act_scr, a_scr, *, nsub):
    e = pl.program_id(2)

    @pl.when(e == 0)
    def _():
        acc[...] = jnp.zeros(acc.shape, F32)

    ncol = act_scr.shape[1] // LANES
    tr = 32
    nrt = PEER_N_KEYS // tr

    act_scr[...] = _dot(u_ref[...], ht_ref[...])
    for ct in range(ncol):
        cols = slice(ct * LANES, (ct + 1) * LANES)
        for rt in range(nrt):
            krows = slice(rt * tr, (rt + 1) * tr)
            ws = [None] * nsub
            for hd in range(PEER_HEADS):
                r2t = r2_ref[hd, krows, cols]
                p2t = p2_ref[hd, krows, cols]
                for ii in range(nsub):
                    contrib = jnp.where(r2t < c1_ref[hd, ii:ii + 1, cols], p2t * e1_ref[hd, ii:ii + 1, cols], 0.0)
                    ws[ii] = contrib if ws[ii] is None else ws[ii] + contrib
            for ii in range(nsub):
                rows = slice(ii * PEER_N_KEYS + rt * tr, ii * PEER_N_KEYS + (rt + 1) * tr)
                a_scr[rows, cols] = (_gelu(act_scr[rows, cols]) * ws[ii]).astype(BF16)
    acc[...] += _dot(vt_ref[...], a_scr[...])

    @pl.when(e == pl.num_programs(2) - 1)
    def _():
        out = acc[...].T
        o_ref[...] = x1_ref[...] + mod_ref[5:6, :] * (_rms(out) * gain_ref[...])


def _peer(ht, u_tab, v_tab, c1, e1, r2, p2, x1, mods, gain, tn, eb):
    bsz, t_len, d = x1.shape
    nt = t_len // tn
    n_exp = u_tab.shape[0]
    nsub = eb // PEER_N_KEYS
    ne = n_exp // eb
    n_tok = bsz * t_len
    vt_tab = v_tab.reshape(ne, eb, d).transpose(0, 2, 1)
    assert c1.shape[1:3] == (ne, nsub)
    sub = pl.BlockSpec((PEER_HEADS, None, nsub, tn), lambda b, i, e: (0, e, 0, b * nt + i))
    hk = pl.BlockSpec((PEER_HEADS, PEER_N_KEYS, tn), lambda b, i, e: (0, 0, b * nt + i))
    return pl.pallas_call(
        functools.partial(_peer_kernel, nsub=nsub),
        grid=(bsz, nt, ne),
        in_specs=[pl.BlockSpec((d, tn), lambda b, i, e: (0, b * nt + i)),
                  pl.BlockSpec((eb, d), lambda b, i, e: (e, 0)),
                  pl.BlockSpec((None, d, eb), lambda b, i, e: (e, 0, 0)),
                  sub, sub, hk, hk,
                  pl.BlockSpec((None, tn, d), lambda b, i, e: (b, i, 0)),
                  pl.BlockSpec((None, 6, d), lambda b, i, e: (b, 0, 0)),
                  pl.BlockSpec((1, d), lambda b, i, e: (0, 0))],
        out_specs=pl.BlockSpec((None, tn, d), lambda b, i, e: (b, i, 0)),
        out_shape=jax.ShapeDtypeStruct(x1.shape, F32),
        scratch_shapes=[pltpu.VMEM((d, tn), F32), pltpu.VMEM((eb, tn), F32), pltpu.VMEM((eb, tn), BF16)],
        compiler_params=_params(("arbitrary", "arbitrary", "arbitrary")),
        name="peer_experts",
    )(ht, u_tab, vt_tab, c1, e1, r2, p2, x1, mods, gain)


def _s5_coeffs(lam_re, lam_im, log_step, b_re, b_im, c_re, c_im):
    step = jnp.exp(log_step)[:, None]
    mag = jnp.exp(lam_re * step)
    ab_re, ab_im = mag * jnp.cos(lam_im * step), mag * jnp.sin(lam_im * step)
    den = lam_re * lam_re + lam_im * lam_im
    f_re = ((ab_re - 1.0) * lam_re + ab_im * lam_im) / den
    f_im = (ab_im * lam_re - (ab_re - 1.0) * lam_im) / den
    bb_re = f_re[..., None] * b_re - f_im[..., None] * b_im
    bb_im = f_re[..., None] * b_im + f_im[..., None] * b_re
    lam = jnp.stack([ab_re.reshape(-1), ab_im.reshape(-1)])
    eye8 = jnp.eye(8, dtype=F32)

    def in_blockdiag(bb):
        t = bb.reshape(4, 8, S5_STATE, S5_GROUP).transpose(0, 1, 3, 2)
        return jnp.einsum('jghp,gk->jghkp', t, eye8).reshape(4, LANES, 8 * S5_STATE)

    def out_blockdiag(cc):
        t = cc.reshape(4, 8, S5_GROUP, S5_STATE).transpose(0, 1, 3, 2)
        return jnp.einsum('jgph,gk->jgpkh', t, eye8).reshape(4, 8 * S5_STATE, LANES)

    wb = jnp.concatenate([in_blockdiag(bb_re), in_blockdiag(bb_im)], axis=-1).astype(BF16)
    cmat = jnp.stack([out_blockdiag(c_re), -out_blockdiag(c_im)], axis=1).astype(BF16)
    return wb, lam, cmat


def _gate_rows(a_log, dt_bias):
    neg_a = -jnp.exp(a_log)
    acoef = jnp.zeros((LANES,), F32).at[0:8].set(neg_a[0]).at[16:24].set(neg_a[1])
    dtb = jnp.zeros((LANES,), F32).at[0:8].set(dt_bias[0]).at[16:24].set(dt_bias[1])
    return acoef[None, :], dtb[None, :], acoef[:32, None], dtb[:32, None]


def kernel(x, c, ctx, c_ctx, w_mod, b_mod, norm_pre_mix, norm_post_mix, norm_pre_ffn, norm_post_ffn, w_in, s5_lam_re, s5_lam_im, s5_log_step, s5_b_re, s5_b_im, s5_c_re, s5_c_im, s5_d, s5_glu_w, s5_glu_b, dn_conv_w, dn_a_log, dn_dt_bias, dn_norm_w, w_branch_s5, w_branch_dn, w_out, peer_w_q, peer_sub_keys, peer_u, peer_v):
    bsz, t_len, d = x.shape
    ctx_len = ctx.shape[1]
    l = 0

    pad = (-(bsz + 1)) % 8
    cv = jnp.concatenate([c, c_ctx[None, :], jnp.zeros((pad, d), F32)], axis=0)
    mod_all = _adaln(cv, w_mod[l].astype(BF16), b_mod[l][None, :])
    mods = mod_all[:bsz].reshape(bsz, 6, d)
    mods_ctx = jnp.broadcast_to(mod_all[bsz].reshape(1, 6, d), (bsz, 6, d))

    w = w_in[l]
    w_kvq = jnp.concatenate([w[:, COL_K:COL_AB], w[:, COL_Q:COL_Z]], axis=1).astype(BF16)
    w_gates = w[:, COL_Z:].astype(BF16)
    w_ctx = w[:, COL_K:COL_AB].astype(BF16)
    w_u = w[:, COL_U:COL_K].astype(BF16)
    w_ab32 = w[:, COL_AB:COL_Q]
    w_ab = jnp.pad(w_ab32, ((0, 0), (0, LANES - 32))).astype(BF16)
    w_abt = w_ab32.T.astype(BF16)
    gain_pre = norm_pre_mix[l][None, :]
    conv_w = dn_conv_w[l]

    kv_c, _, u_c, ab_c, abt_c = _inproj(ctx, mods_ctx, gain_pre, w_ctx, conv_w[:, :2 * DN_WIDTH], None,
                                        w_u, w_ab, w_abt, tm=ctx_len, tn=2 * DN_WIDTH, row_len=ctx_len)
    kvq_x, gates_x, u_x, ab_x, abt_x = _inproj(x, mods, gain_pre, w_kvq, conv_w, w_gates, w_u, w_ab, w_abt,
                                               tm=min(1024, t_len), tn=1536, row_len=GRID_W)

    wbf, lamf, cf = _s5_coeffs(s5_lam_re[l, 0], s5_lam_im[l, 0], s5_log_step[l, 0], s5_b_re[l, 0], s5_b_im[l, 0],
                               s5_c_re[l, 0], s5_c_im[l, 0])
    wbb, lamb, cb = _s5_coeffs(s5_lam_re[l, 1], s5_lam_im[l, 1], s5_log_step[l, 1], s5_b_re[l, 1], s5_b_im[l, 1],
                               s5_c_re[l, 1], s5_c_im[l, 1])
    zeros_h = jnp.zeros((bsz, 2 * S5_COLS), F32)
    tc = 32
    hf_c, hb_c = _s5(u_c, wbf, wbb, lamf, lamb, cf, cb, zeros_h, zeros_h, tc, False)
    yf, yb, _, _ = _s5(u_x, wbf, wbb, lamf, lamb, cf, cb, hf_c, hb_c, tc, True)

    acoef, dtb, acoeft, dtbt = _gate_rows(dn_a_log[l], dn_dt_bias[l])
    zeros_s = jnp.zeros((bsz, DN_HEADS, DN_HEAD_DIM, DN_HEAD_DIM), F32)
    dn_args = (acoef, dtb, acoeft, dtbt)
    (sf_c,) = _deltanet(kv_c, ab_c, abt_c, *dn_args, zeros_s, ctx_len, False, False)
    (sb_c,) = _deltanet(kv_c, ab_c, abt_c, *dn_args, zeros_s, ctx_len, True, False)
    blk_len = min(256, t_len)
    o_f, _ = _deltanet(kvq_x, ab_x, abt_x, *dn_args, sf_c, blk_len, False, True)
    o_b, _ = _deltanet(kvq_x, ab_x, abt_x, *dn_args, sb_c, blk_len, True, True)

    x1 = _post(x, mods, yf, yb, u_x, o_f, o_b, gates_x,
               s5_d[l][None, :], s5_glu_w[l].astype(BF16), s5_glu_b[l][None, :], dn_norm_w[l][None, :],
               w_branch_s5[l].astype(BF16), w_branch_dn[l].astype(BF16), w_out[l].astype(BF16),
               norm_post_mix[l][None, :], tm=min(512, t_len))

    ht, c1, e1, r2, p2 = _route(x1, mods, norm_pre_ffn[l][None, :], peer_w_q[l].astype(BF16),
                                peer_sub_keys[l].astype(BF16), tm=min(256, t_len))
    out = _peer(ht, peer_u[l].astype(BF16), peer_v[l].astype(BF16), c1, e1, r2, p2, x1, mods,
                norm_post_ffn[l][None, :], tn=min(1024, t_len), eb=PEER_EXPERT_BLOCK)
    return out
```

```python
import functools
import math

import jax
import jax.numpy as jnp
from jax import lax
from jax.experimental import pallas as pl
from jax.experimental.pallas import tpu as pltpu

F32 = jnp.float32
BF16 = jnp.bfloat16

D_MODEL = 1024
NORM_EPS = 1e-6
GRID_W = 64
S5_WIDTH = 512
S5_GROUP = 16
S5_GROUPS = 32
S5_STATE = 64
S5_COLS = S5_GROUPS * S5_STATE
DN_HEADS = 8
DN_HEAD_DIM = 128
DN_WIDTH = 1024
DN_CONV = 5
DN_CHUNK = 64
PEER_HEADS = 8
PEER_N_KEYS = 128
PEER_KEY_DIM = 128
PEER_TOPK = 16
PEER_EXPERT_BLOCK = 512
PEER_SUB = PEER_EXPERT_BLOCK // PEER_N_KEYS
LANES = 128

COL_U = 0
COL_K = 512
COL_V = 1536
COL_AB = 2560
COL_Q = 2592
COL_Z = 3616
COL_G5 = 4640
COL_GD = 5664

VMEM_LIMIT = 56 * 1024 * 1024


def _dot(a, b):
    return jnp.dot(a, b, preferred_element_type=F32)


def _dot_nt(a, b):
    return lax.dot_general(a, b, (((1,), (1,)), ((), ())), preferred_element_type=F32)


def _dot_tn(a, b):
    return lax.dot_general(a, b, (((0,), (0,)), ((), ())), preferred_element_type=F32)


def _sigmoid(x):
    return jax.nn.sigmoid(x)


def _silu(x):
    return x * _sigmoid(x)


def _gelu(x):
    c = math.sqrt(2.0 / math.pi)
    return x * (0.5 * (1.0 + jnp.tanh(c * (x + 0.044715 * (x * x * x)))))


def _softplus(x):
    return jnp.maximum(x, 0.0) + jnp.log1p(jnp.exp(-jnp.abs(x)))


def _rms(x):
    return x * lax.rsqrt(jnp.mean(x * x, axis=-1, keepdims=True) + NORM_EPS)


def _params(sem):
    return pltpu.CompilerParams(dimension_semantics=sem, vmem_limit_bytes=VMEM_LIMIT)


def _adaln_kernel(c_ref, w_ref, b_ref, o_ref):
    c = c_ref[...]
    o_ref[...] = _dot(_silu(c).astype(BF16), w_ref[...]) + b_ref[...]


def _adaln(cv, w_mod, b_mod):
    rows = cv.shape[0]
    ncol = w_mod.shape[1] // D_MODEL
    return pl.pallas_call(
        _adaln_kernel,
        grid=(ncol,),
        in_specs=[pl.BlockSpec((rows, D_MODEL), lambda j: (0, 0)),
                  pl.BlockSpec((D_MODEL, D_MODEL), lambda j: (0, j)),
                  pl.BlockSpec((1, D_MODEL), lambda j: (0, j))],
        out_specs=pl.BlockSpec((rows, D_MODEL), lambda j: (0, j)),
        out_shape=jax.ShapeDtypeStruct((rows, w_mod.shape[1]), F32),
        compiler_params=_params(("arbitrary",)),
        name="adaln",
    )(cv, w_mod, b_mod)


def _inproj_kernel(x_ref, mod_ref, gain_ref, wc_ref, cw_ref, *rest, ncs, row_len, has_gates):
    if has_gates:
        wg_ref, wu_ref, wab_ref, wabt_ref, kvq_ref, gates_ref, u_ref, ab_ref, abt_ref, hx_scr = rest
    else:
        wu_ref, wab_ref, wabt_ref, kvq_ref, u_ref, ab_ref, abt_ref, hx_scr = rest
    j = pl.program_id(2)

    @pl.when(j == 0)
    def _():
        h = _rms(x_ref[...]) * gain_ref[...]
        h = h * (1.0 + mod_ref[1:2, :]) + mod_ref[0:1, :]
        hb = h.astype(BF16)
        hx_scr[...] = hb
        u_ref[...] = _dot(hb, wu_ref[...])
        ab_ref[...] = _dot(hb, wab_ref[...])
        abt_ref[...] = _dot_nt(wabt_ref[...], hb)

    @pl.when(j < ncs)
    def _():
        kvq_ref[...] = _dot(hx_scr[...], wc_ref[...])
        tm, tn = kvq_ref.shape
        tpos = lax.broadcasted_iota(jnp.int32, (tm, LANES), 0) % row_len
        valid = [(tpos + (t - 2) >= 0) & (tpos + (t - 2) < row_len) for t in range(DN_CONV)]
        for cbk in range(tn // LANES):
            cols = slice(cbk * LANES, (cbk + 1) * LANES)
            xc = kvq_ref[:, cols]
            acc = xc * cw_ref[2:3, cols]
            for t in (0, 1, 3, 4):
                acc = acc + jnp.where(valid[t], pltpu.roll(xc, (2 - t) % tm, axis=0), 0.0) * cw_ref[t:t + 1, cols]
            kvq_ref[:, cols] = _silu(acc)

    if has_gates:
        @pl.when(j >= ncs)
        def _():
            gates_ref[...] = _dot(hx_scr[...], wg_ref[...])


def _inproj(x, mods, gain, w_conv, conv_w, w_gate, w_u, w_ab, w_abt, tm, tn, row_len):
    bsz, t_len, d = x.shape
    cc = w_conv.shape[1]
    ncs = cc // tn
    has_gates = w_gate is not None
    ngs = (w_gate.shape[1] // tn) if has_gates else 0
    cstep = lambda j: jnp.minimum(j, ncs - 1)
    gstep = lambda j: jnp.maximum(j - ncs, 0)
    in_specs = [pl.BlockSpec((None, tm, d), lambda b, i, j: (b, i, 0)),
                pl.BlockSpec((None, 6, d), lambda b, i, j: (b, 0, 0)),
                pl.BlockSpec((1, d), lambda b, i, j: (0, 0)),
                pl.BlockSpec((d, tn), lambda b, i, j: (0, cstep(j))),
                pl.BlockSpec((DN_CONV, tn), lambda b, i, j: (0, cstep(j)))]
    args = [x, mods, gain, w_conv, conv_w]
    out_specs = [pl.BlockSpec((None, tm, tn), lambda b, i, j: (b, i, cstep(j)))]
    out_shape = [jax.ShapeDtypeStruct((bsz, t_len, cc), F32)]
    if has_gates:
        in_specs.append(pl.BlockSpec((d, tn), lambda b, i, j: (0, gstep(j))))
        args.append(w_gate)
        out_specs.append(pl.BlockSpec((None, tm, tn), lambda b, i, j: (b, i, gstep(j))))
        out_shape.append(jax.ShapeDtypeStruct((bsz, t_len, w_gate.shape[1]), F32))
    in_specs += [pl.BlockSpec((d, S5_WIDTH), lambda b, i, j: (0, 0)),
                 pl.BlockSpec((d, LANES), lambda b, i, j: (0, 0)),
                 pl.BlockSpec((32, d), lambda b, i, j: (0, 0))]
    args += [w_u, w_ab, w_abt]
    out_specs += [pl.BlockSpec((None, tm, S5_WIDTH), lambda b, i, j: (b, i, 0)),
                  pl.BlockSpec((None, tm, LANES), lambda b, i, j: (b, i, 0)),
                  pl.BlockSpec((None, 32, tm), lambda b, i, j: (b, 0, i))]
    out_shape += [jax.ShapeDtypeStruct((bsz, t_len, S5_WIDTH), F32),
                  jax.ShapeDtypeStruct((bsz, t_len, LANES), F32),
                  jax.ShapeDtypeStruct((bsz, 32, t_len), F32)]
    outs = pl.pallas_call(
        functools.partial(_inproj_kernel, ncs=ncs, row_len=row_len, has_gates=has_gates),
        grid=(bsz, t_len // tm, ncs + ngs),
        in_specs=in_specs,
        out_specs=out_specs,
        out_shape=out_shape,
        scratch_shapes=[pltpu.VMEM((tm, d), BF16)],
        compiler_params=_params(("arbitrary", "arbitrary", "arbitrary")),
        name="inproj",
    )(*args)
    if has_gates:
        return outs
    return (outs[0], None) + tuple(outs[1:])


def _s5_kernel(uf_ref, ub_ref, wbf_ref, wbb_ref, lamf_ref, lamb_ref, cf_ref, cb_ref,
               h0f_ref, h0b_ref, *rest, tc, bsz, need_out):
    if need_out:
        yf_ref, yb_ref, hff_ref, hfb_ref, xsf, xsb, hcf, hcb = rest
    else:
        hff_ref, hfb_ref, xsf, xsb, hcf, hcb = rest
        yf_ref = yb_ref = None
    i = pl.program_id(0)
    half = S5_COLS
    cw = 512
    nblk = half // cw

    @pl.when(i == 0)
    def _():
        hcf[...] = h0f_ref[...]
        hcb[...] = h0b_ref[...]

    dirs = ((uf_ref, wbf_ref, lamf_ref, cf_ref, yf_ref, xsf, hcf, False),
            (ub_ref, wbb_ref, lamb_ref, cb_ref, yb_ref, xsb, hcb, True))
    for u_ref, wb_ref, lam_ref, c_ref, y_ref, xs, hc, rev in dirs:
        for j in range(4):
            lhs = jnp.concatenate(
                [u_ref[:, t * S5_WIDTH + j * LANES: t * S5_WIDTH + (j + 1) * LANES] for t in range(tc)],
                axis=0).astype(BF16)
            r = _dot(lhs, wb_ref[j])
            xs[:, j * cw:(j + 1) * cw] = r[:, :cw]
            xs[:, half + j * cw: half + (j + 1) * cw] = r[:, cw:]
        for cb in range(nblk):
            lo, hi = cb * cw, (cb + 1) * cw
            ar = jnp.broadcast_to(lam_ref[0:1, lo:hi], (bsz, cw))
            ai = jnp.broadcast_to(lam_ref[1:2, lo:hi], (bsz, cw))

            def body(s, carry, lo=lo, hi=hi, ar=ar, ai=ai, xs=xs, rev=rev):
                hr, hi_ = carry
                t = (tc - 1 - s) if rev else s
                row = pl.multiple_of(t * bsz, bsz)
                xr = xs[pl.ds(row, bsz), lo:hi]
                xi = xs[pl.ds(row, bsz), half + lo: half + hi]
                nr = ar * hr - ai * hi_ + xr
                ni = ar * hi_ + ai * hr + xi
                xs[pl.ds(row, bsz), lo:hi] = nr
                xs[pl.ds(row, bsz), half + lo: half + hi] = ni
                return nr, ni

            hr, hi_ = lax.fori_loop(0, tc, body, (hc[:, lo:hi], hc[:, half + lo: half + hi]))
            hc[:, lo:hi] = hr
            hc[:, half + lo: half + hi] = hi_
        if need_out:
            for j in range(4):
                hre = xs[:, j * cw:(j + 1) * cw].astype(BF16)
                him = xs[:, half + j * cw: half + (j + 1) * cw].astype(BF16)
                y = _dot(hre, c_ref[j, 0]) + _dot(him, c_ref[j, 1])
                for t in range(tc):
                    y_ref[:, t * S5_WIDTH + j * LANES: t * S5_WIDTH + (j + 1) * LANES] = y[t * bsz:(t + 1) * bsz]
    hff_ref[...] = hcf[...]
    hfb_ref[...] = hcb[...]


def _s5(u, wbf, wbb, lamf, lamb, cf, cb, h0f, h0b, tc, need_out):
    bsz = u.shape[0]
    t_len = u.shape[1] // S5_WIDTH
    nc = t_len // tc
    const2 = lambda i: (0, 0)
    const3 = lambda i: (0, 0, 0)
    const4 = lambda i: (0, 0, 0, 0)
    st_spec = pl.BlockSpec((bsz, 2 * S5_COLS), const2)
    st_shape = jax.ShapeDtypeStruct((bsz, 2 * S5_COLS), F32)
    out_specs = [st_spec, st_spec]
    out_shape = [st_shape, st_shape]
    fwd_blk = pl.BlockSpec((bsz, tc * S5_WIDTH), lambda i: (0, i))
    bwd_blk = pl.BlockSpec((bsz, tc * S5_WIDTH), lambda i: (0, nc - 1 - i))
    if need_out:
        y_shape = jax.ShapeDtypeStruct(u.shape, F32)
        out_specs = [fwd_blk, bwd_blk] + out_specs
        out_shape = [y_shape, y_shape] + out_shape
    return pl.pallas_call(
        functools.partial(_s5_kernel, tc=tc, bsz=bsz, need_out=need_out),
        grid=(nc,),
        in_specs=[fwd_blk, bwd_blk,
                  pl.BlockSpec(wbf.shape, const3), pl.BlockSpec(wbb.shape, const3),
                  pl.BlockSpec(lamf.shape, const2), pl.BlockSpec(lamb.shape, const2),
                  pl.BlockSpec(cf.shape, const4), pl.BlockSpec(cb.shape, const4),
                  st_spec, st_spec],
        out_specs=out_specs,
        out_shape=out_shape,
        scratch_shapes=[pltpu.VMEM((tc * bsz, 2 * S5_COLS), F32),
                        pltpu.VMEM((tc * bsz, 2 * S5_COLS), F32),
                        pltpu.VMEM((bsz, 2 * S5_COLS), F32),
                        pltpu.VMEM((bsz, 2 * S5_COLS), F32)],
        compiler_params=_params(("arbitrary",)),
        name="s5_scan",
    )(u, u, wbf, wbb, lamf, lamb, cf, cb, h0f, h0b)


def _dn_kernel(act_scr, ab_ref, abt_ref, acoef_ref, dtb_ref, acoeft_ref, dtbt_ref, s0_ref,
               *rest, blk_len, rev, need_out):
    if need_out:
        o_ref, sfin_ref, s_scr = rest
    else:
        sfin_ref, s_scr = rest
        o_ref = None
    cs = DN_CHUNK
    nch = blk_len // cs
    goff = 16 if rev else 0
    boff = goff + 8

    @pl.when(pl.program_id(1) == 0)
    def _():
        s_scr[...] = s0_ref[...]

    ab = ab_ref[...]
    g_cols = acoef_ref[...] * _softplus(ab + dtb_ref[...])
    beta_cols = _sigmoid(ab)
    g_rows = acoeft_ref[...] * _softplus(abt_ref[...] + dtbt_ref[...])

    ri = lax.broadcasted_iota(jnp.int32, (cs, cs), 0)
    ci = lax.broadcasted_iota(jnp.int32, (cs, cs), 1)
    incl = (ri <= ci) if rev else (ri >= ci)
    strict = incl & (ri != ci)
    tri = incl.astype(F32)
    eye = (ri == ci).astype(F32)
    last = 0 if rev else cs - 1
    scale = DN_HEAD_DIM ** -0.5

    order = list(range(nch - 1, -1, -1)) if rev else list(range(nch))
    units = [(c, h) for c in order for h in range(DN_HEADS)]
    nu = len(units)
    gc_cols = {}
    gc_rows = {}
    for c in order:
        rows = slice(c * cs, (c + 1) * cs)
        gc_cols[c] = jnp.dot(tri, g_cols[rows, :], precision=lax.Precision.HIGHEST, preferred_element_type=F32)
        gc_rows[c] = lax.dot_general(g_rows[:, rows], tri, (((1,), (1,)), ((), ())),
                                     precision=lax.Precision.HIGHEST, preferred_element_type=F32)

    def head_block(base, c, h):
        return act_scr[c * cs:(c + 1) * cs, base + h * DN_HEAD_DIM: base + (h + 1) * DN_HEAD_DIM]

    def l2n(t):
        return t * lax.rsqrt(jnp.sum(t * t, axis=-1, keepdims=True) + 1e-6)

    gch = [gc_cols[c][:, goff + h: goff + h + 1] for c, h in units]
    gcr = [gc_rows[c][goff + h: goff + h + 1, :] for c, h in units]
    dec = [jnp.where(incl, jnp.exp(jnp.minimum(gch[u] - gcr[u], 0.0)), 0.0) for u in range(nu)]
    bh = [beta_cols[c * cs:(c + 1) * cs, boff + h: boff + h + 1] for c, h in units]
    k = [l2n(head_block(0, c, h)) for c, h in units]
    kbf = [t.astype(BF16) for t in k]
    kb = [k[u] * bh[u] for u in range(nu)]
    kkt = [_dot_nt(kb[u].astype(BF16), kbf[u]) for u in range(nu)]
    a = [jnp.where(strict, kkt[u] * dec[u], 0.0) for u in range(nu)]
    abf = [t.astype(BF16) for t in a]
    m = [_dot(t, t) for t in abf]
    p = [eye - t for t in a]
    for lvl in range(5):
        mb = [t.astype(BF16) for t in m]
        if lvl < 4:
            r = [_dot(jnp.concatenate([m[u], p[u]], axis=0).astype(BF16), mb[u]) for u in range(nu)]
            m = [t[:cs] for t in r]
            p = [p[u] + r[u][cs:] for u in range(nu)]
        else:
            r = [_dot(p[u].astype(BF16), mb[u]) for u in range(nu)]
            p = [p[u] + r[u] for u in range(nu)]
    eg = [jnp.exp(t) for t in gch]
    rhs = [jnp.concatenate([head_block(DN_WIDTH, c, h) * bh[u], kb[u] * eg[u]], axis=-1).astype(BF16)
           for u, (c, h) in enumerate(units)]
    uw = [_dot(p[u].astype(BF16), rhs[u]) for u in range(nu)]
    gl = [t[last:last + 1, :] for t in gch]
    kd = [(k[u] * jnp.exp(gl[u] - gch[u])).astype(BF16) for u in range(nu)]
    if need_out:
        q = [l2n(head_block(2 * DN_WIDTH, c, h)) * scale for c, h in units]
        qkt = [_dot_nt(q[u].astype(BF16), kbf[u]) for u in range(nu)]
        at = [jnp.where(incl, qkt[u] * dec[u], 0.0).astype(BF16) for u in range(nu)]
        qd = [(q[u] * eg[u]).astype(BF16) for u in range(nu)]

    for ci, c in enumerate(order):
        us = range(ci * DN_HEADS, (ci + 1) * DN_HEADS)
        s_old = [s_scr[h] for h in range(DN_HEADS)]
        s_bf = [t.astype(BF16) for t in s_old]
        ws = [_dot(uw[u][:, DN_HEAD_DIM:].astype(BF16), s_bf[h]) for h, u in enumerate(us)]
        if need_out:
            qs = [_dot(qd[u], s_bf[h]) for h, u in enumerate(us)]
        vn = [(uw[u][:, :DN_HEAD_DIM] - ws[h]).astype(BF16) for h, u in enumerate(us)]
        if need_out:
            av = [_dot(at[u], vn[h]) for h, u in enumerate(us)]
        kv = [_dot_tn(kd[u], vn[h]) for h, u in enumerate(us)]
        for h, u in enumerate(us):
            if need_out:
                o_ref[c * cs:(c + 1) * cs, h * DN_HEAD_DIM:(h + 1) * DN_HEAD_DIM] = qs[h] + av[h]
            s_scr[h] = s_old[h] * jnp.exp(gl[u]) + kv[h]
    sfin_ref[...] = s_scr[...]


def _deltanet(kvq, ab, abt, acoef, dtb, acoeft, dtbt, s0, blk_len, rev, need_out):
    bsz, t_len, ncols = kvq.shape
    nb = t_len // blk_len
    bi = (lambda b, i: (b, nb - 1 - i, 0)) if rev else (lambda b, i: (b, i, 0))
    bit = (lambda b, i: (b, 0, nb - 1 - i)) if rev else (lambda b, i: (b, 0, i))
    c2 = lambda b, i: (0, 0)
    st_spec = pl.BlockSpec((None, DN_HEADS, DN_HEAD_DIM, DN_HEAD_DIM), lambda b, i: (b, 0, 0, 0))
    st_shape = jax.ShapeDtypeStruct((bsz, DN_HEADS, DN_HEAD_DIM, DN_HEAD_DIM), F32)
    out_specs = [st_spec]
    out_shape = [st_shape]
    if need_out:
        out_specs = [pl.BlockSpec((None, blk_len, DN_WIDTH), bi)] + out_specs
        out_shape = [jax.ShapeDtypeStruct((bsz, t_len, DN_WIDTH), F32)] + out_shape
    return pl.pallas_call(
        functools.partial(_dn_kernel, blk_len=blk_len, rev=rev, need_out=need_out),
        grid=(bsz, nb),
        in_specs=[pl.BlockSpec((None, blk_len, ncols), bi),
                  pl.BlockSpec((None, blk_len, LANES), bi),
                  pl.BlockSpec((None, 32, blk_len), bit),
                  pl.BlockSpec((1, LANES), c2), pl.BlockSpec((1, LANES), c2),
                  pl.BlockSpec((32, 1), c2), pl.BlockSpec((32, 1), c2),
                  st_spec],
        out_specs=out_specs,
        out_shape=out_shape,
        scratch_shapes=[pltpu.VMEM((DN_HEADS, DN_HEAD_DIM, DN_HEAD_DIM), F32)],
        compiler_params=_params(("arbitrary", "arbitrary")),
        name="deltanet_bwd" if rev else "deltanet_fwd",
    )(kvq, ab, abt, acoef, dtb, acoeft, dtbt, s0)


def _post_kernel(x_ref, mod_ref, yf_ref, yb_ref, u_ref, of_ref, ob_ref, z_ref, g5_ref, gd_ref,
                 s5d_ref, glw_ref, glb_ref, dnw_ref, wb5_ref, wbd_ref, wo_ref, gain_ref, o_ref):
    y5 = _gelu(yf_ref[...] + yb_ref[...] + s5d_ref[...] * u_ref[...])
    y5 = y5 * _sigmoid(_dot(y5.astype(BF16), glw_ref[...]) + glb_ref[...])
    o = of_ref[...] + ob_ref[...]
    parts = []
    for h in range(DN_HEADS):
        oh = o[:, h * DN_HEAD_DIM:(h + 1) * DN_HEAD_DIM]
        parts.append(_rms(oh) * dnw_ref[...])
    y_dn = jnp.concatenate(parts, axis=-1) * _silu(z_ref[...])
    merged = (_sigmoid(g5_ref[...]) * _dot(y5.astype(BF16), wb5_ref[...])
              + _sigmoid(gd_ref[...]) * _dot(y_dn.astype(BF16), wbd_ref[...]))
    x_mix = _dot(merged.astype(BF16), wo_ref[...])
    o_ref[...] = x_ref[...] + mod_ref[2:3, :] * (_rms(x_mix) * gain_ref[...])


def _post(x, mods, yf, yb, u, o_f, o_b, gates, s5d, glw, glb, dnw, wb5, wbd, wo, gain, tm):
    bsz, t_len, d = x.shape
    tok = lambda w: pl.BlockSpec((None, tm, w), lambda b, i: (b, i, 0))
    colblk = lambda k: pl.BlockSpec((None, tm, d), lambda b, i, k=k: (b, i, k))
    full = lambda a: pl.BlockSpec(a.shape, lambda b, i: (0,) * a.ndim)
    return pl.pallas_call(
        _post_kernel,
        grid=(bsz, t_len // tm),
        in_specs=[tok(d), pl.BlockSpec((None, 6, d), lambda b, i: (b, 0, 0)),
                  tok(S5_WIDTH), tok(S5_WIDTH), tok(S5_WIDTH), tok(d), tok(d),
                  colblk(0), colblk(1), colblk(2),
                  full(s5d), full(glw), full(glb), full(dnw), full(wb5), full(wbd), full(wo), full(gain)],
        out_specs=tok(d),
        out_shape=jax.ShapeDtypeStruct(x.shape, F32),
        compiler_params=_params(("arbitrary", "arbitrary")),
        name="mixer_out",
    )(x, mods, yf, yb, u, o_f, o_b, gates, gates, gates, s5d, glw, glb, dnw, wb5, wbd, wo, gain)


_CAND_COUNTS = [PEER_TOPK // (a + 1) for a in range(PEER_TOPK)]
_N_CAND = sum(_CAND_COUNTS)
_CAND_ROWS = -(-_N_CAND // 8) * 8


def _extract_topk(work, k, out_vals_ref=None, want_rank=False, break_ties=True):
    nrows = work.shape[0]
    iota = lax.broadcasted_iota(jnp.int32, work.shape, 0)
    rank = jnp.full(work.shape, float(k), F32) if want_rank else None
    vals = []
    for r in range(k):
        m = jnp.max(work, axis=0, keepdims=True)
        if break_ties:
            idx = jnp.min(jnp.where(work == m, iota, nrows), axis=0, keepdims=True)
            sel = iota == idx
        else:
            sel = work == m
        work = jnp.where(sel, -jnp.inf, work)
        if want_rank:
            rank = jnp.where(sel, float(r), rank)
        vals.append(m)
        if out_vals_ref is not None:
            out_vals_ref[r:r + 1, :] = m
    return vals, work, rank


def _route_kernel(x_ref, mod_ref, gain_ref, wq_ref, keys_ref,
                  ht_ref, c1_ref, e1_ref, r2_ref, p2_ref, q_scr, sv_scr, cand_scr):
    h = _rms(x_ref[...]) * gain_ref[...]
    h = h * (1.0 + mod_ref[4:5, :]) + mod_ref[3:4, :]
    hb = h.astype(BF16)
    ht_ref[...] = h.T.astype(BF16)
    q_scr[...] = _dot(hb, wq_ref[...]).astype(BF16)
    topk = float(PEER_TOPK)

    def route_head(hd, scores, break_ties):
        svals = []
        ranks = []
        for j in range(2):
            vals, _, rank = _extract_topk(scores[j], PEER_TOPK, sv_scr.at[j], want_rank=(break_ties or j == 1),
                                          break_ties=break_ties)
            svals.append(vals)
            ranks.append(rank)
        if break_ties:
            is_rank = [ranks[0] == float(a) for a in range(PEER_TOPK)]
            member = [r < topk for r in ranks]
        else:
            is_rank = [scores[0] == svals[0][a] for a in range(PEER_TOPK)]
            member = [scores[0] >= svals[0][PEER_TOPK - 1], ranks[1] < topk]
        cand_scr[...] = jnp.full(cand_scr.shape, -jnp.inf, F32)
        off = 0
        for a in range(PEER_TOPK):
            nb = _CAND_COUNTS[a]
            cand_scr[off: off + nb, :] = svals[0][a] + sv_scr[1, 0:nb, :]
            off += nb
        top, left, _ = _extract_topk(cand_scr[...], PEER_TOPK, break_ties=break_ties)
        m0 = top[0]
        z = jnp.exp(top[0] - m0)
        for r in range(1, PEER_TOPK):
            z = z + jnp.exp(top[r] - m0)
        taken = (left == -jnp.inf).astype(F32)
        c1 = jnp.zeros(scores[0].shape, F32)
        n_taken = jnp.zeros_like(m0)
        off = 0
        for a in range(PEER_TOPK):
            nb = _CAND_COUNTS[a]
            cnt = jnp.sum(taken[off: off + nb, :], axis=0, keepdims=True)
            c1 = jnp.where(is_rank[a], cnt, c1)
            n_taken = n_taken + cnt
            off += nb
        e1 = jnp.where(member[0], jnp.exp(scores[0] - svals[0][0]), 0.0)
        for g in range(PEER_N_KEYS // PEER_SUB):
            c1_ref[hd, g] = c1[g * PEER_SUB:(g + 1) * PEER_SUB]
            e1_ref[hd, g] = e1[g * PEER_SUB:(g + 1) * PEER_SUB]
        r2_ref[hd] = ranks[1]
        p2_ref[hd] = jnp.where(member[1], jnp.exp(scores[1] - svals[1][0]), 0.0) / z
        n_member = [jnp.sum(mb.astype(F32), axis=0, keepdims=True) for mb in member]
        return jnp.abs(n_member[0] - topk) + jnp.abs(n_member[1] - topk) + jnp.abs(n_taken - topk)

    def head(hd, carry):
        scores = []
        for j in range(2):
            col = pl.multiple_of((hd * 2 + j) * PEER_KEY_DIM, PEER_KEY_DIM)
            scores.append(_dot_nt(keys_ref[hd, j], q_scr[:, pl.ds(col, PEER_KEY_DIM)]))
        excess = jnp.max(route_head(hd, scores, break_ties=False))

        @pl.when(excess > 0.0)
        def _():
            route_head(hd, scores, break_ties=True)

        return carry

    lax.fori_loop(0, PEER_HEADS, head, 0)


def _route(x1, mods, gain, wq, keys, tm):
    bsz, t_len, d = x1.shape
    n_tok = bsz * t_len
    nt = t_len // tm
    tile = lambda b, i: (0, 0, b * nt + i)
    hk = pl.BlockSpec((PEER_HEADS, PEER_N_KEYS, tm), tile)
    hk_shape = jax.ShapeDtypeStruct((PEER_HEADS, PEER_N_KEYS, n_tok), F32)
    nblk = PEER_N_KEYS // PEER_SUB
    sub = pl.BlockSpec((PEER_HEADS, nblk, PEER_SUB, tm), lambda b, i: (0, 0, 0, b * nt + i))
    sub_shape = jax.ShapeDtypeStruct((PEER_HEADS, nblk, PEER_SUB, n_tok), F32)
    return pl.pallas_call(
        _route_kernel,
        grid=(bsz, nt),
        in_specs=[pl.BlockSpec((None, tm, d), lambda b, i: (b, i, 0)),
                  pl.BlockSpec((None, 6, d), lambda b, i: (b, 0, 0)),
                  pl.BlockSpec((1, d), lambda b, i: (0, 0)),
                  pl.BlockSpec(wq.shape, lambda b, i: (0, 0)),
                  pl.BlockSpec(keys.shape, lambda b, i: (0, 0, 0, 0))],
        out_specs=[pl.BlockSpec((d, tm), lambda b, i: (0, b * nt + i)),
                   sub, sub, hk, hk],
        out_shape=[jax.ShapeDtypeStruct((d, n_tok), BF16),
                   sub_shape, sub_shape, hk_shape, hk_shape],
        scratch_shapes=[pltpu.VMEM((tm, wq.shape[1]), BF16),
                        pltpu.VMEM((2, PEER_TOPK, tm), F32),
                        pltpu.VMEM((_CAND_ROWS, tm), F32)],
        compiler_params=_params(("arbitrary", "arbitrary")),
        name="peer_route",
    )(x1, mods, gain, wq, keys)


def _peer_kernel(ht_ref, u_ref, vt_ref, c1_ref, e1_ref, r2_ref, p2_ref, x1_ref, mod_ref, gain_ref,
                 o_ref, acc, act_scr, a_scr, *, nsub):
    e = pl.program_id(2)

    @pl.when(e == 0)
    def _():
        acc[...] = jnp.zeros(acc.shape, F32)

    ncol = act_scr.shape[1] // LANES
    tr = 32
    nrt = PEER_N_KEYS // tr

    act_scr[...] = _dot(u_ref[...], ht_ref[...])
    for ct in range(ncol):
        cols = slice(ct * LANES, (ct + 1) * LANES)
        for rt in range(nrt):
            krows = slice(rt * tr, (rt + 1) * tr)
            ws = [None] * nsub
            for hd in range(PEER_HEADS):
                r2t = r2_ref[hd, krows, cols]
                p2t = p2_ref[hd, krows, cols]
                for ii in range(nsub):
                    contrib = jnp.where(r2t < c1_ref[hd, ii:ii + 1, cols], p2t * e1_ref[hd, ii:ii + 1, cols], 0.0)
                    ws[ii] = contrib if ws[ii] is None else ws[ii] + contrib
            for ii in range(nsub):
                rows = slice(ii * PEER_N_KEYS + rt * tr, ii * PEER_N_KEYS + (rt + 1) * tr)
                a_scr[rows, cols] = (_gelu(act_scr[rows, cols]) * ws[ii]).astype(BF16)
    acc[...] += _dot(vt_ref[...], a_scr[...])

    @pl.when(e == pl.num_programs(2) - 1)
    def _():
        out = acc[...].T
        o_ref[...] = x1_ref[...] + mod_ref[5:6, :] * (_rms(out) * gain_ref[...])


def _peer(ht, u_tab, v_tab, c1, e1, r2, p2, x1, mods, gain, tn, eb):
    bsz, t_len, d = x1.shape
    nt = t_len // tn
    n_exp = u_tab.shape[0]
    nsub = eb // PEER_N_KEYS
    ne = n_exp // eb
    n_tok = bsz * t_len
    vt_tab = v_tab.reshape(ne, eb, d).transpose(0, 2, 1)
    assert c1.shape[1:3] == (ne, nsub)
    sub = pl.BlockSpec((PEER_HEADS, None, nsub, tn), lambda b, i, e: (0, e, 0, b * nt + i))
    hk = pl.BlockSpec((PEER_HEADS, PEER_N_KEYS, tn), lambda b, i, e: (0, 0, b * nt + i))
    return pl.pallas_call(
        functools.partial(_peer_kernel, nsub=nsub),
        grid=(bsz, nt, ne),
        in_specs=[pl.BlockSpec((d, tn), lambda b, i, e: (0, b * nt + i)),
                  pl.BlockSpec((eb, d), lambda b, i, e: (e, 0)),
                  pl.BlockSpec((None, d, eb), lambda b, i, e: (e, 0, 0)),
                  sub, sub, hk, hk,
                  pl.BlockSpec((None, tn, d), lambda b, i, e: (b, i, 0)),
                  pl.BlockSpec((None, 6, d), lambda b, i, e: (b, 0, 0)),
                  pl.BlockSpec((1, d), lambda b, i, e: (0, 0))],
        out_specs=pl.BlockSpec((None, tn, d), lambda b, i, e: (b, i, 0)),
        out_shape=jax.ShapeDtypeStruct(x1.shape, F32),
        scratch_shapes=[pltpu.VMEM((d, tn), F32), pltpu.VMEM((eb, tn), F32), pltpu.VMEM((eb, tn), BF16)],
        compiler_params=_params(("arbitrary", "arbitrary", "arbitrary")),
        name="peer_experts",
    )(ht, u_tab, vt_tab, c1, e1, r2, p2, x1, mods, gain)


def _s5_coeffs(lam_re, lam_im, log_step, b_re, b_im, c_re, c_im):
    step = jnp.exp(log_step)[:, None]
    mag = jnp.exp(lam_re * step)
    ab_re, ab_im = mag * jnp.cos(lam_im * step), mag * jnp.sin(lam_im * step)
    den = lam_re * lam_re + lam_im * lam_im
    f_re = ((ab_re - 1.0) * lam_re + ab_im * lam_im) / den
    f_im = (ab_im * lam_re - (ab_re - 1.0) * lam_im) / den
    bb_re = f_re[..., None] * b_re - f_im[..., None] * b_im
    bb_im = f_re[..., None] * b_im + f_im[..., None] * b_re
    lam = jnp.stack([ab_re.reshape(-1), ab_im.reshape(-1)])
    eye8 = jnp.eye(8, dtype=F32)

    def in_blockdiag(bb):
        t = bb.reshape(4, 8, S5_STATE, S5_GROUP).transpose(0, 1, 3, 2)
        return jnp.einsum('jghp,gk->jghkp', t, eye8).reshape(4, LANES, 8 * S5_STATE)

    def out_blockdiag(cc):
        t = cc.reshape(4, 8, S5_GROUP, S5_STATE).transpose(0, 1, 3, 2)
        return jnp.einsum('jgph,gk->jgpkh', t, eye8).reshape(4, 8 * S5_STATE, LANES)

    wb = jnp.concatenate([in_blockdiag(bb_re), in_blockdiag(bb_im)], axis=-1).astype(BF16)
    cmat = jnp.stack([out_blockdiag(c_re), -out_blockdiag(c_im)], axis=1).astype(BF16)
    return wb, lam, cmat


def _gate_rows(a_log, dt_bias):
    neg_a = -jnp.exp(a_log)
    acoef = jnp.zeros((LANES,), F32).at[0:8].set(neg_a[0]).at[16:24].set(neg_a[1])
    dtb = jnp.zeros((LANES,), F32).at[0:8].set(dt_bias[0]).at[16:24].set(dt_bias[1])
    return acoef[None, :], dtb[None, :], acoef[:32, None], dtb[:32, None]


def kernel(x, c, ctx, c_ctx, w_mod, b_mod, norm_pre_mix, norm_post_mix, norm_pre_ffn, norm_post_ffn, w_in, s5_lam_re, s5_lam_im, s5_log_step, s5_b_re, s5_b_im, s5_c_re, s5_c_im, s5_d, s5_glu_w, s5_glu_b, dn_conv_w, dn_a_log, dn_dt_bias, dn_norm_w, w_branch_s5, w_branch_dn, w_out, peer_w_q, peer_sub_keys, peer_u, peer_v):
    bsz, t_len, d = x.shape
    ctx_len = ctx.shape[1]
    l = 0

    pad = (-(bsz + 1)) % 8
    cv = jnp.concatenate([c, c_ctx[None, :], jnp.zeros((pad, d), F32)], axis=0)
    mod_all = _adaln(cv, w_mod[l].astype(BF16), b_mod[l][None, :])
    mods = mod_all[:bsz].reshape(bsz, 6, d)
    mods_ctx = jnp.broadcast_to(mod_all[bsz].reshape(1, 6, d), (bsz, 6, d))

    w = w_in[l]
    w_kvq = jnp.concatenate([w[:, COL_K:COL_AB], w[:, COL_Q:COL_Z]], axis=1).astype(BF16)
    w_gates = w[:, COL_Z:].astype(BF16)
    w_ctx = w[:, COL_K:COL_AB].astype(BF16)
    w_u = w[:, COL_U:COL_K].astype(BF16)
    w_ab32 = w[:, COL_AB:COL_Q]
    w_ab = jnp.pad(w_ab32, ((0, 0), (0, LANES - 32))).astype(BF16)
    w_abt = w_ab32.T.astype(BF16)
    gain_pre = norm_pre_mix[l][None, :]
    conv_w = dn_conv_w[l]

    kv_c, _, u_c, ab_c, abt_c = _inproj(ctx, mods_ctx, gain_pre, w_ctx, conv_w[:, :2 * DN_WIDTH], None,
                                        w_u, w_ab, w_abt, tm=ctx_len, tn=2 * DN_WIDTH, row_len=ctx_len)
    kvq_x, gates_x, u_x, ab_x, abt_x = _inproj(x, mods, gain_pre, w_kvq, conv_w, w_gates, w_u, w_ab, w_abt,
                                               tm=min(1024, t_len), tn=1536, row_len=GRID_W)

    wbf, lamf, cf = _s5_coeffs(s5_lam_re[l, 0], s5_lam_im[l, 0], s5_log_step[l, 0], s5_b_re[l, 0], s5_b_im[l, 0],
                               s5_c_re[l, 0], s5_c_im[l, 0])
    wbb, lamb, cb = _s5_coeffs(s5_lam_re[l, 1], s5_lam_im[l, 1], s5_log_step[l, 1], s5_b_re[l, 1], s5_b_im[l, 1],
                               s5_c_re[l, 1], s5_c_im[l, 1])
    zeros_h = jnp.zeros((bsz, 2 * S5_COLS), F32)
    tc = 32
    hf_c, hb_c = _s5(u_c.reshape(bsz, ctx_len * S5_WIDTH), wbf, wbb, lamf, lamb, cf, cb, zeros_h, zeros_h, tc, False)
    yf, yb, _, _ = _s5(u_x.reshape(bsz, t_len * S5_WIDTH), wbf, wbb, lamf, lamb, cf, cb, hf_c, hb_c, tc, True)
    yf = yf.reshape(bsz, t_len, S5_WIDTH)
    yb = yb.reshape(bsz, t_len, S5_WIDTH)

    acoef, dtb, acoeft, dtbt = _gate_rows(dn_a_log[l], dn_dt_bias[l])
    zeros_s = jnp.zeros((bsz, DN_HEADS, DN_HEAD_DIM, DN_HEAD_DIM), F32)
    dn_args = (acoef, dtb, acoeft, dtbt)
    (sf_c,) = _deltanet(kv_c, ab_c, abt_c, *dn_args, zeros_s, ctx_len, False, False)
    (sb_c,) = _deltanet(kv_c, ab_c, abt_c, *dn_args, zeros_s, ctx_len, True, False)
    blk_len = min(512, t_len)
    o_f, _ = _deltanet(kvq_x, ab_x, abt_x, *dn_args, sf_c, blk_len, False, True)
    o_b, _ = _deltanet(kvq_x, ab_x, abt_x, *dn_args, sb_c, blk_len, True, True)

    x1 = _post(x, mods, yf, yb, u_x, o_f, o_b, gates_x,
               s5_d[l][None, :], s5_glu_w[l].astype(BF16), s5_glu_b[l][None, :], dn_norm_w[l][None, :],
               w_branch_s5[l].astype(BF16), w_branch_dn[l].astype(BF16), w_out[l].astype(BF16),
               norm_post_mix[l][None, :], tm=min(512, t_len))

    ht, c1, e1, r2, p2 = _route(x1, mods, norm_pre_ffn[l][None, :], peer_w_q[l].astype(BF16),
                                peer_sub_keys[l].astype(BF16), tm=min(256, t_len))
    out = _peer(ht, peer_u[l].astype(BF16), peer_v[l].astype(BF16), c1, e1, r2, p2, x1, mods,
                norm_post_ffn[l][None, :], tn=min(1024, t_len), eb=PEER_EXPERT_BLOCK)
    return out
```

```python
import functools
import math

import jax
import jax.numpy as jnp
from jax import lax
from jax.experimental import pallas as pl
from jax.experimental.pallas import tpu as pltpu

F32 = jnp.float32
BF16 = jnp.bfloat16

D_MODEL = 1024
NORM_EPS = 1e-6
GRID_W = 64
S5_WIDTH = 512
S5_GROUP = 16
S5_GROUPS = 32
S5_STATE = 64
S5_COLS = S5_GROUPS * S5_STATE
DN_HEADS = 8
DN_HEAD_DIM = 128
DN_WIDTH = 1024
DN_CONV = 5
DN_CHUNK = 64
PEER_HEADS = 8
PEER_N_KEYS = 128
PEER_KEY_DIM = 128
PEER_TOPK = 16
PEER_EXPERT_BLOCK = 512
PEER_SUB = PEER_EXPERT_BLOCK // PEER_N_KEYS
LANES = 128

COL_U = 0
COL_K = 512
COL_V = 1536
COL_AB = 2560
COL_Q = 2592
COL_Z = 3616
COL_G5 = 4640
COL_GD = 5664

VMEM_LIMIT = 56 * 1024 * 1024


def _dot(a, b):
    return jnp.dot(a, b, preferred_element_type=F32)


def _dot_nt(a, b):
    return lax.dot_general(a, b, (((1,), (1,)), ((), ())), preferred_element_type=F32)


def _dot_tn(a, b):
    return lax.dot_general(a, b, (((0,), (0,)), ((), ())), preferred_element_type=F32)


def _sigmoid(x):
    return jax.nn.sigmoid(x)


def _silu(x):
    return x * _sigmoid(x)


def _gelu(x):
    c = math.sqrt(2.0 / math.pi)
    return x * (0.5 * (1.0 + jnp.tanh(c * (x + 0.044715 * (x * x * x)))))


def _softplus(x):
    return jnp.maximum(x, 0.0) + jnp.log1p(jnp.exp(-jnp.abs(x)))


def _rms(x):
    return x * lax.rsqrt(jnp.mean(x * x, axis=-1, keepdims=True) + NORM_EPS)


def _params(sem):
    return pltpu.CompilerParams(dimension_semantics=sem, vmem_limit_bytes=VMEM_LIMIT)


def _adaln_kernel(c_ref, w_ref, b_ref, o_ref):
    c = c_ref[...]
    o_ref[...] = _dot(_silu(c).astype(BF16), w_ref[...]) + b_ref[...]


def _adaln(cv, w_mod, b_mod):
    rows = cv.shape[0]
    ncol = w_mod.shape[1] // D_MODEL
    return pl.pallas_call(
        _adaln_kernel,
        grid=(ncol,),
        in_specs=[pl.BlockSpec((rows, D_MODEL), lambda j: (0, 0)),
                  pl.BlockSpec((D_MODEL, D_MODEL), lambda j: (0, j)),
                  pl.BlockSpec((1, D_MODEL), lambda j: (0, j))],
        out_specs=pl.BlockSpec((rows, D_MODEL), lambda j: (0, j)),
        out_shape=jax.ShapeDtypeStruct((rows, w_mod.shape[1]), F32),
        compiler_params=_params(("arbitrary",)),
        name="adaln",
    )(cv, w_mod, b_mod)


def _inproj_kernel(x_ref, mod_ref, gain_ref, wc_ref, cw_ref, *rest, ncs, row_len, has_gates):
    if has_gates:
        wg_ref, wu_ref, wab_ref, wabt_ref, kvq_ref, gates_ref, u_ref, ab_ref, abt_ref, hx_scr = rest
    else:
        wu_ref, wab_ref, wabt_ref, kvq_ref, u_ref, ab_ref, abt_ref, hx_scr = rest
    j = pl.program_id(2)

    @pl.when(j == 0)
    def _():
        h = _rms(x_ref[...]) * gain_ref[...]
        h = h * (1.0 + mod_ref[1:2, :]) + mod_ref[0:1, :]
        hb = h.astype(BF16)
        hx_scr[...] = hb
        u_ref[...] = _dot(hb, wu_ref[...])
        ab_ref[...] = _dot(hb, wab_ref[...])
        abt_ref[...] = _dot_nt(wabt_ref[...], hb)

    @pl.when(j < ncs)
    def _():
        kvq_ref[...] = _dot(hx_scr[...], wc_ref[...])
        tm, tn = kvq_ref.shape
        tpos = lax.broadcasted_iota(jnp.int32, (tm, LANES), 0) % row_len
        valid = [(tpos + (t - 2) >= 0) & (tpos + (t - 2) < row_len) for t in range(DN_CONV)]
        for cbk in range(tn // LANES):
            cols = slice(cbk * LANES, (cbk + 1) * LANES)
            xc = kvq_ref[:, cols]
            acc = xc * cw_ref[2:3, cols]
            for t in (0, 1, 3, 4):
                acc = acc + jnp.where(valid[t], pltpu.roll(xc, (2 - t) % tm, axis=0), 0.0) * cw_ref[t:t + 1, cols]
            kvq_ref[:, cols] = _silu(acc)

    if has_gates:
        @pl.when(j >= ncs)
        def _():
            gates_ref[...] = _dot(hx_scr[...], wg_ref[...])


def _inproj(x, mods, gain, w_conv, conv_w, w_gate, w_u, w_ab, w_abt, tm, tn, row_len):
    bsz, t_len, d = x.shape
    cc = w_conv.shape[1]
    ncs = cc // tn
    has_gates = w_gate is not None
    ngs = (w_gate.shape[1] // tn) if has_gates else 0
    cstep = lambda j: jnp.minimum(j, ncs - 1)
    gstep = lambda j: jnp.maximum(j - ncs, 0)
    in_specs = [pl.BlockSpec((None, tm, d), lambda b, i, j: (b, i, 0)),
                pl.BlockSpec((None, 6, d), lambda b, i, j: (b, 0, 0)),
                pl.BlockSpec((1, d), lambda b, i, j: (0, 0)),
                pl.BlockSpec((d, tn), lambda b, i, j: (0, cstep(j))),
                pl.BlockSpec((DN_CONV, tn), lambda b, i, j: (0, cstep(j)))]
    args = [x, mods, gain, w_conv, conv_w]
    out_specs = [pl.BlockSpec((None, tm, tn), lambda b, i, j: (b, i, cstep(j)))]
    out_shape = [jax.ShapeDtypeStruct((bsz, t_len, cc), F32)]
    if has_gates:
        in_specs.append(pl.BlockSpec((d, tn), lambda b, i, j: (0, gstep(j))))
        args.append(w_gate)
        out_specs.append(pl.BlockSpec((None, tm, tn), lambda b, i, j: (b, i, gstep(j))))
        out_shape.append(jax.ShapeDtypeStruct((bsz, t_len, w_gate.shape[1]), F32))
    in_specs += [pl.BlockSpec((d, S5_WIDTH), lambda b, i, j: (0, 0)),
                 pl.BlockSpec((d, LANES), lambda b, i, j: (0, 0)),
                 pl.BlockSpec((32, d), lambda b, i, j: (0, 0))]
    args += [w_u, w_ab, w_abt]
    out_specs += [pl.BlockSpec((None, tm, S5_WIDTH), lambda b, i, j: (b, i, 0)),
                  pl.BlockSpec((None, tm, LANES), lambda b, i, j: (b, i, 0)),
                  pl.BlockSpec((None, 32, tm), lambda b, i, j: (b, 0, i))]
    out_shape += [jax.ShapeDtypeStruct((bsz, t_len, S5_WIDTH), F32),
                  jax.ShapeDtypeStruct((bsz, t_len, LANES), F32),
                  jax.ShapeDtypeStruct((bsz, 32, t_len), F32)]
    outs = pl.pallas_call(
        functools.partial(_inproj_kernel, ncs=ncs, row_len=row_len, has_gates=has_gates),
        grid=(bsz, t_len // tm, ncs + ngs),
        in_specs=in_specs,
        out_specs=out_specs,
        out_shape=out_shape,
        scratch_shapes=[pltpu.VMEM((tm, d), BF16)],
        compiler_params=_params(("arbitrary", "arbitrary", "arbitrary")),
        name="inproj",
    )(*args)
    if has_gates:
        return outs
    return (outs[0], None) + tuple(outs[1:])


def _s5_kernel(uf_ref, ub_ref, wbf_ref, wbb_ref, lamf_ref, lamb_ref, cf_ref, cb_ref,
               h0f_ref, h0b_ref, *rest, tc, bsz, need_out):
    if need_out:
        yf_ref, yb_ref, hff_ref, hfb_ref, xsf, xsb, hcf, hcb = rest
    else:
        hff_ref, hfb_ref, xsf, xsb, hcf, hcb = rest
        yf_ref = yb_ref = None
    i = pl.program_id(0)
    half = S5_COLS
    cw = 512
    nblk = half // cw

    @pl.when(i == 0)
    def _():
        hcf[...] = h0f_ref[...]
        hcb[...] = h0b_ref[...]

    dirs = ((uf_ref, wbf_ref, lamf_ref, cf_ref, yf_ref, xsf, hcf, False),
            (ub_ref, wbb_ref, lamb_ref, cb_ref, yb_ref, xsb, hcb, True))
    for u_ref, wb_ref, lam_ref, c_ref, y_ref, xs, hc, rev in dirs:
        for j in range(4):
            lhs = jnp.concatenate(
                [u_ref[:, t * S5_WIDTH + j * LANES: t * S5_WIDTH + (j + 1) * LANES] for t in range(tc)],
                axis=0).astype(BF16)
            r = _dot(lhs, wb_ref[j])
            xs[:, j * cw:(j + 1) * cw] = r[:, :cw]
            xs[:, half + j * cw: half + (j + 1) * cw] = r[:, cw:]
        for cb in range(nblk):
            lo, hi = cb * cw, (cb + 1) * cw
            ar = jnp.broadcast_to(lam_ref[0:1, lo:hi], (bsz, cw))
            ai = jnp.broadcast_to(lam_ref[1:2, lo:hi], (bsz, cw))

            def body(s, carry, lo=lo, hi=hi, ar=ar, ai=ai, xs=xs, rev=rev):
                hr, hi_ = carry
                t = (tc - 1 - s) if rev else s
                row = pl.multiple_of(t * bsz, bsz)
                xr = xs[pl.ds(row, bsz), lo:hi]
                xi = xs[pl.ds(row, bsz), half + lo: half + hi]
                nr = ar * hr - ai * hi_ + xr
                ni = ar * hi_ + ai * hr + xi
                xs[pl.ds(row, bsz), lo:hi] = nr
                xs[pl.ds(row, bsz), half + lo: half + hi] = ni
                return nr, ni

            hr, hi_ = lax.fori_loop(0, tc, body, (hc[:, lo:hi], hc[:, half + lo: half + hi]))
            hc[:, lo:hi] = hr
            hc[:, half + lo: half + hi] = hi_
        if need_out:
            for j in range(4):
                hre = xs[:, j * cw:(j + 1) * cw].astype(BF16)
                him = xs[:, half + j * cw: half + (j + 1) * cw].astype(BF16)
                y = _dot(hre, c_ref[j, 0]) + _dot(him, c_ref[j, 1])
                for t in range(tc):
                    y_ref[:, t * S5_WIDTH + j * LANES: t * S5_WIDTH + (j + 1) * LANES] = y[t * bsz:(t + 1) * bsz]
    hff_ref[...] = hcf[...]
    hfb_ref[...] = hcb[...]


def _s5(u, wbf, wbb, lamf, lamb, cf, cb, h0f, h0b, tc, need_out):
    bsz = u.shape[0]
    t_len = u.shape[1] // S5_WIDTH
    nc = t_len // tc
    const2 = lambda i: (0, 0)
    const3 = lambda i: (0, 0, 0)
    const4 = lambda i: (0, 0, 0, 0)
    st_spec = pl.BlockSpec((bsz, 2 * S5_COLS), const2)
    st_shape = jax.ShapeDtypeStruct((bsz, 2 * S5_COLS), F32)
    out_specs = [st_spec, st_spec]
    out_shape = [st_shape, st_shape]
    fwd_blk = pl.BlockSpec((bsz, tc * S5_WIDTH), lambda i: (0, i))
    bwd_blk = pl.BlockSpec((bsz, tc * S5_WIDTH), lambda i: (0, nc - 1 - i))
    if need_out:
        y_shape = jax.ShapeDtypeStruct(u.shape, F32)
        out_specs = [fwd_blk, bwd_blk] + out_specs
        out_shape = [y_shape, y_shape] + out_shape
    return pl.pallas_call(
        functools.partial(_s5_kernel, tc=tc, bsz=bsz, need_out=need_out),
        grid=(nc,),
        in_specs=[fwd_blk, bwd_blk,
                  pl.BlockSpec(wbf.shape, const3), pl.BlockSpec(wbb.shape, const3),
                  pl.BlockSpec(lamf.shape, const2), pl.BlockSpec(lamb.shape, const2),
                  pl.BlockSpec(cf.shape, const4), pl.BlockSpec(cb.shape, const4),
                  st_spec, st_spec],
        out_specs=out_specs,
        out_shape=out_shape,
        scratch_shapes=[pltpu.VMEM((tc * bsz, 2 * S5_COLS), F32),
                        pltpu.VMEM((tc * bsz, 2 * S5_COLS), F32),
                        pltpu.VMEM((bsz, 2 * S5_COLS), F32),
                        pltpu.VMEM((bsz, 2 * S5_COLS), F32)],
        compiler_params=_params(("arbitrary",)),
        name="s5_scan",
    )(u, u, wbf, wbb, lamf, lamb, cf, cb, h0f, h0b)


def _dn_kernel(act_scr, ab_ref, abt_ref, acoef_ref, dtb_ref, acoeft_ref, dtbt_ref, s0_ref,
               *rest, blk_len, rev, need_out):
    if need_out:
        o_ref, sfin_ref, s_scr = rest
    else:
        sfin_ref, s_scr = rest
        o_ref = None
    cs = DN_CHUNK
    nch = blk_len // cs
    goff = 16 if rev else 0
    boff = goff + 8

    @pl.when(pl.program_id(1) == 0)
    def _():
        s_scr[...] = s0_ref[...]

    ab = ab_ref[...]
    g_cols = acoef_ref[...] * _softplus(ab + dtb_ref[...])
    beta_cols = _sigmoid(ab)
    g_rows = acoeft_ref[...] * _softplus(abt_ref[...] + dtbt_ref[...])

    ri = lax.broadcasted_iota(jnp.int32, (cs, cs), 0)
    ci = lax.broadcasted_iota(jnp.int32, (cs, cs), 1)
    incl = (ri <= ci) if rev else (ri >= ci)
    strict = incl & (ri != ci)
    tri = incl.astype(F32)
    eye = (ri == ci).astype(F32)
    last = 0 if rev else cs - 1
    scale = DN_HEAD_DIM ** -0.5

    order = list(range(nch - 1, -1, -1)) if rev else list(range(nch))
    units = [(c, h) for c in order for h in range(DN_HEADS)]
    nu = len(units)
    gc_cols = {}
    gc_rows = {}
    for c in order:
        rows = slice(c * cs, (c + 1) * cs)
        gc_cols[c] = jnp.dot(tri, g_cols[rows, :], precision=lax.Precision.HIGHEST, preferred_element_type=F32)
        gc_rows[c] = lax.dot_general(g_rows[:, rows], tri, (((1,), (1,)), ((), ())),
                                     precision=lax.Precision.HIGHEST, preferred_element_type=F32)

    def head_block(base, c, h):
        return act_scr[c * cs:(c + 1) * cs, base + h * DN_HEAD_DIM: base + (h + 1) * DN_HEAD_DIM]

    def l2n(t):
        return t * lax.rsqrt(jnp.sum(t * t, axis=-1, keepdims=True) + 1e-6)

    gch = [gc_cols[c][:, goff + h: goff + h + 1] for c, h in units]
    gcr = [gc_rows[c][goff + h: goff + h + 1, :] for c, h in units]
    dec = [jnp.where(incl, jnp.exp(jnp.minimum(gch[u] - gcr[u], 0.0)), 0.0) for u in range(nu)]
    bh = [beta_cols[c * cs:(c + 1) * cs, boff + h: boff + h + 1] for c, h in units]
    k = [l2n(head_block(0, c, h)) for c, h in units]
    kbf = [t.astype(BF16) for t in k]
    kb = [k[u] * bh[u] for u in range(nu)]
    kkt = [_dot_nt(kb[u].astype(BF16), kbf[u]) for u in range(nu)]
    a = [jnp.where(strict, kkt[u] * dec[u], 0.0) for u in range(nu)]
    abf = [t.astype(BF16) for t in a]
    m = [_dot(t, t) for t in abf]
    p = [eye - t for t in a]
    for lvl in range(5):
        mb = [t.astype(BF16) for t in m]
        if lvl < 4:
            r = [_dot(jnp.concatenate([m[u], p[u]], axis=0).astype(BF16), mb[u]) for u in range(nu)]
            m = [t[:cs] for t in r]
            p = [p[u] + r[u][cs:] for u in range(nu)]
        else:
            r = [_dot(p[u].astype(BF16), mb[u]) for u in range(nu)]
            p = [p[u] + r[u] for u in range(nu)]
    eg = [jnp.exp(t) for t in gch]
    rhs = [jnp.concatenate([head_block(DN_WIDTH, c, h) * bh[u], kb[u] * eg[u]], axis=-1).astype(BF16)
           for u, (c, h) in enumerate(units)]
    uw = [_dot(p[u].astype(BF16), rhs[u]) for u in range(nu)]
    gl = [t[last:last + 1, :] for t in gch]
    kd = [(k[u] * jnp.exp(gl[u] - gch[u])).astype(BF16) for u in range(nu)]
    if need_out:
        q = [l2n(head_block(2 * DN_WIDTH, c, h)) * scale for c, h in units]
        qkt = [_dot_nt(q[u].astype(BF16), kbf[u]) for u in range(nu)]
        at = [jnp.where(incl, qkt[u] * dec[u], 0.0).astype(BF16) for u in range(nu)]
        qd = [(q[u] * eg[u]).astype(BF16) for u in range(nu)]

    for ci, c in enumerate(order):
        us = range(ci * DN_HEADS, (ci + 1) * DN_HEADS)
        s_old = [s_scr[h] for h in range(DN_HEADS)]
        s_bf = [t.astype(BF16) for t in s_old]
        ws = [_dot(uw[u][:, DN_HEAD_DIM:].astype(BF16), s_bf[h]) for h, u in enumerate(us)]
        if need_out:
            qs = [_dot(qd[u], s_bf[h]) for h, u in enumerate(us)]
        vn = [(uw[u][:, :DN_HEAD_DIM] - ws[h]).astype(BF16) for h, u in enumerate(us)]
        if need_out:
            av = [_dot(at[u], vn[h]) for h, u in enumerate(us)]
        kv = [_dot_tn(kd[u], vn[h]) for h, u in enumerate(us)]
        for h, u in enumerate(us):
            if need_out:
                o_ref[c * cs:(c + 1) * cs, h * DN_HEAD_DIM:(h + 1) * DN_HEAD_DIM] = qs[h] + av[h]
            s_scr[h] = s_old[h] * jnp.exp(gl[u]) + kv[h]
    sfin_ref[...] = s_scr[...]


def _deltanet(kvq, ab, abt, acoef, dtb, acoeft, dtbt, s0, blk_len, rev, need_out):
    bsz, t_len, ncols = kvq.shape
    nb = t_len // blk_len
    bi = (lambda b, i: (b, nb - 1 - i, 0)) if rev else (lambda b, i: (b, i, 0))
    bit = (lambda b, i: (b, 0, nb - 1 - i)) if rev else (lambda b, i: (b, 0, i))
    c2 = lambda b, i: (0, 0)
    st_spec = pl.BlockSpec((None, DN_HEADS, DN_HEAD_DIM, DN_HEAD_DIM), lambda b, i: (b, 0, 0, 0))
    st_shape = jax.ShapeDtypeStruct((bsz, DN_HEADS, DN_HEAD_DIM, DN_HEAD_DIM), F32)
    out_specs = [st_spec]
    out_shape = [st_shape]
    if need_out:
        out_specs = [pl.BlockSpec((None, blk_len, DN_WIDTH), bi)] + out_specs
        out_shape = [jax.ShapeDtypeStruct((bsz, t_len, DN_WIDTH), F32)] + out_shape
    return pl.pallas_call(
        functools.partial(_dn_kernel, blk_len=blk_len, rev=rev, need_out=need_out),
        grid=(bsz, nb),
        in_specs=[pl.BlockSpec((None, blk_len, ncols), bi),
                  pl.BlockSpec((None, blk_len, LANES), bi),
                  pl.BlockSpec((None, 32, blk_len), bit),
                  pl.BlockSpec((1, LANES), c2), pl.BlockSpec((1, LANES), c2),
                  pl.BlockSpec((32, 1), c2), pl.BlockSpec((32, 1), c2),
                  st_spec],
        out_specs=out_specs,
        out_shape=out_shape,
        scratch_shapes=[pltpu.VMEM((DN_HEADS, DN_HEAD_DIM, DN_HEAD_DIM), F32)],
        compiler_params=_params(("arbitrary", "arbitrary")),
        name="deltanet_bwd" if rev else "deltanet_fwd",
    )(kvq, ab, abt, acoef, dtb, acoeft, dtbt, s0)


def _post_kernel(x_ref, mod_ref, yf_ref, yb_ref, u_ref, of_ref, ob_ref, z_ref, g5_ref, gd_ref,
                 s5d_ref, glw_ref, glb_ref, dnw_ref, wb5_ref, wbd_ref, wo_ref, gain_ref, o_ref):
    y5 = _gelu(yf_ref[...] + yb_ref[...] + s5d_ref[...] * u_ref[...])
    y5 = y5 * _sigmoid(_dot(y5.astype(BF16), glw_ref[...]) + glb_ref[...])
    o = of_ref[...] + ob_ref[...]
    parts = []
    for h in range(DN_HEADS):
        oh = o[:, h * DN_HEAD_DIM:(h + 1) * DN_HEAD_DIM]
        parts.append(_rms(oh) * dnw_ref[...])
    y_dn = jnp.concatenate(parts, axis=-1) * _silu(z_ref[...])
    merged = (_sigmoid(g5_ref[...]) * _dot(y5.astype(BF16), wb5_ref[...])
              + _sigmoid(gd_ref[...]) * _dot(y_dn.astype(BF16), wbd_ref[...]))
    x_mix = _dot(merged.astype(BF16), wo_ref[...])
    o_ref[...] = x_ref[...] + mod_ref[2:3, :] * (_rms(x_mix) * gain_ref[...])


def _post(x, mods, yf, yb, u, o_f, o_b, gates, s5d, glw, glb, dnw, wb5, wbd, wo, gain, tm):
    bsz, t_len, d = x.shape
    tok = lambda w: pl.BlockSpec((None, tm, w), lambda b, i: (b, i, 0))
    colblk = lambda k: pl.BlockSpec((None, tm, d), lambda b, i, k=k: (b, i, k))
    full = lambda a: pl.BlockSpec(a.shape, lambda b, i: (0,) * a.ndim)
    return pl.pallas_call(
        _post_kernel,
        grid=(bsz, t_len // tm),
        in_specs=[tok(d), pl.BlockSpec((None, 6, d), lambda b, i: (b, 0, 0)),
                  tok(S5_WIDTH), tok(S5_WIDTH), tok(S5_WIDTH), tok(d), tok(d),
                  colblk(0), colblk(1), colblk(2),
                  full(s5d), full(glw), full(glb), full(dnw), full(wb5), full(wbd), full(wo), full(gain)],
        out_specs=tok(d),
        out_shape=jax.ShapeDtypeStruct(x.shape, F32),
        compiler_params=_params(("arbitrary", "arbitrary")),
        name="mixer_out",
    )(x, mods, yf, yb, u, o_f, o_b, gates, gates, gates, s5d, glw, glb, dnw, wb5, wbd, wo, gain)


_CAND_COUNTS = [PEER_TOPK // (a + 1) for a in range(PEER_TOPK)]
_N_CAND = sum(_CAND_COUNTS)
_CAND_ROWS = -(-_N_CAND // 8) * 8


def _extract_topk(work, k, out_vals_ref=None, want_rank=False, break_ties=True):
    nrows = work.shape[0]
    iota = lax.broadcasted_iota(jnp.int32, work.shape, 0)
    rank = jnp.full(work.shape, float(k), F32) if want_rank else None
    vals = []
    for r in range(k):
        m = jnp.max(work, axis=0, keepdims=True)
        if break_ties:
            idx = jnp.min(jnp.where(work == m, iota, nrows), axis=0, keepdims=True)
            sel = iota == idx
        else:
            sel = work == m
        work = jnp.where(sel, -jnp.inf, work)
        if want_rank:
            rank = jnp.where(sel, float(r), rank)
        vals.append(m)
        if out_vals_ref is not None:
            out_vals_ref[r:r + 1, :] = m
    return vals, work, rank


def _route_kernel(x_ref, mod_ref, gain_ref, wq_ref, keys_ref,
                  ht_ref, c1_ref, e1_ref, r2_ref, p2_ref, q_scr, sv_a, sv_b, cand_a, cand_b):
    h = _rms(x_ref[...]) * gain_ref[...]
    h = h * (1.0 + mod_ref[4:5, :]) + mod_ref[3:4, :]
    hb = h.astype(BF16)
    ht_ref[...] = h.T.astype(BF16)
    q_scr[...] = _dot(hb, wq_ref[...]).astype(BF16)
    topk = float(PEER_TOPK)

    def route_head(hd, scores, break_ties, sv_scr, cand_scr):
        svals = []
        ranks = []
        for j in range(2):
            vals, _, rank = _extract_topk(scores[j], PEER_TOPK, sv_scr.at[j], want_rank=(break_ties or j == 1),
                                          break_ties=break_ties)
            svals.append(vals)
            ranks.append(rank)
        if break_ties:
            is_rank = [ranks[0] == float(a) for a in range(PEER_TOPK)]
            member = [r < topk for r in ranks]
        else:
            is_rank = [scores[0] == svals[0][a] for a in range(PEER_TOPK)]
            member = [scores[0] >= svals[0][PEER_TOPK - 1], ranks[1] < topk]
        cand_scr[...] = jnp.full(cand_scr.shape, -jnp.inf, F32)
        off = 0
        for a in range(PEER_TOPK):
            nb = _CAND_COUNTS[a]
            cand_scr[off: off + nb, :] = svals[0][a] + sv_scr[1, 0:nb, :]
            off += nb
        top, left, _ = _extract_topk(cand_scr[...], PEER_TOPK, break_ties=break_ties)
        m0 = top[0]
        z = jnp.exp(top[0] - m0)
        for r in range(1, PEER_TOPK):
            z = z + jnp.exp(top[r] - m0)
        taken = (left == -jnp.inf).astype(F32)
        c1 = jnp.zeros(scores[0].shape, F32)
        n_taken = jnp.zeros_like(m0)
        off = 0
        for a in range(PEER_TOPK):
            nb = _CAND_COUNTS[a]
            cnt = jnp.sum(taken[off: off + nb, :], axis=0, keepdims=True)
            c1 = jnp.where(is_rank[a], cnt, c1)
            n_taken = n_taken + cnt
            off += nb
        e1 = jnp.where(member[0], jnp.exp(scores[0] - svals[0][0]), 0.0)
        for g in range(PEER_N_KEYS // PEER_SUB):
            c1_ref[hd, g] = c1[g * PEER_SUB:(g + 1) * PEER_SUB]
            e1_ref[hd, g] = e1[g * PEER_SUB:(g + 1) * PEER_SUB]
        r2_ref[hd] = ranks[1]
        p2_ref[hd] = jnp.where(member[1], jnp.exp(scores[1] - svals[1][0]), 0.0) / z
        n_member = [jnp.sum(mb.astype(F32), axis=0, keepdims=True) for mb in member]
        return jnp.abs(n_member[0] - topk) + jnp.abs(n_member[1] - topk) + jnp.abs(n_taken - topk)

    def head_pair(pair, carry):
        work = []
        for k, (sv_scr, cand_scr) in enumerate(((sv_a, cand_a), (sv_b, cand_b))):
            hd = pair * 2 + k
            scores = []
            for j in range(2):
                col = pl.multiple_of((hd * 2 + j) * PEER_KEY_DIM, PEER_KEY_DIM)
                scores.append(_dot_nt(keys_ref[hd, j], q_scr[:, pl.ds(col, PEER_KEY_DIM)]))
            work.append((hd, scores, sv_scr, cand_scr))
        excess = [jnp.max(route_head(hd, scores, False, sv_scr, cand_scr)) for hd, scores, sv_scr, cand_scr in work]
        for (hd, scores, sv_scr, cand_scr), ex in zip(work, excess):
            @pl.when(ex > 0.0)
            def _(hd=hd, scores=scores, sv_scr=sv_scr, cand_scr=cand_scr):
                route_head(hd, scores, True, sv_scr, cand_scr)

        return carry

    lax.fori_loop(0, PEER_HEADS // 2, head_pair, 0)


def _route(x1, mods, gain, wq, keys, tm):
    bsz, t_len, d = x1.shape
    n_tok = bsz * t_len
    nt = t_len // tm
    tile = lambda b, i: (0, 0, b * nt + i)
    hk = pl.BlockSpec((PEER_HEADS, PEER_N_KEYS, tm), tile)
    hk_shape = jax.ShapeDtypeStruct((PEER_HEADS, PEER_N_KEYS, n_tok), F32)
    nblk = PEER_N_KEYS // PEER_SUB
    sub = pl.BlockSpec((PEER_HEADS, nblk, PEER_SUB, tm), lambda b, i: (0, 0, 0, b * nt + i))
    sub_shape = jax.ShapeDtypeStruct((PEER_HEADS, nblk, PEER_SUB, n_tok), F32)
    return pl.pallas_call(
        _route_kernel,
        grid=(bsz, nt),
        in_specs=[pl.BlockSpec((None, tm, d), lambda b, i: (b, i, 0)),
                  pl.BlockSpec((None, 6, d), lambda b, i: (b, 0, 0)),
                  pl.BlockSpec((1, d), lambda b, i: (0, 0)),
                  pl.BlockSpec(wq.shape, lambda b, i: (0, 0)),
                  pl.BlockSpec(keys.shape, lambda b, i: (0, 0, 0, 0))],
        out_specs=[pl.BlockSpec((d, tm), lambda b, i: (0, b * nt + i)),
                   sub, sub, hk, hk],
        out_shape=[jax.ShapeDtypeStruct((d, n_tok), BF16),
                   sub_shape, sub_shape, hk_shape, hk_shape],
        scratch_shapes=[pltpu.VMEM((tm, wq.shape[1]), BF16),
                        pltpu.VMEM((2, PEER_TOPK, tm), F32), pltpu.VMEM((2, PEER_TOPK, tm), F32),
                        pltpu.VMEM((_CAND_ROWS, tm), F32), pltpu.VMEM((_CAND_ROWS, tm), F32)],
        compiler_params=_params(("arbitrary", "arbitrary")),
        name="peer_route",
    )(x1, mods, gain, wq, keys)


def _peer_kernel(ht_ref, u_ref, vt_ref, c1_ref, e1_ref, r2_ref, p2_ref, x1_ref, mod_ref, gain_ref,
                 o_ref, acc, act_scr, a_scr, *, nsub):
    e = pl.program_id(2)

    @pl.when(e == 0)
    def _():
        acc[...] = jnp.zeros(acc.shape, F32)

    ncol = act_scr.shape[1] // LANES
    tr = 32
    nrt = PEER_N_KEYS // tr

    act_scr[...] = _dot(u_ref[...], ht_ref[...])
    for ct in range(ncol):
        cols = slice(ct * LANES, (ct + 1) * LANES)
        for rt in range(nrt):
            krows = slice(rt * tr, (rt + 1) * tr)
            ws = [None] * nsub
            for hd in range(PEER_HEADS):
                r2t = r2_ref[hd, krows, cols]
                p2t = p2_ref[hd, krows, cols]
                for ii in range(nsub):
                    contrib = jnp.where(r2t < c1_ref[hd, ii:ii + 1, cols], p2t * e1_ref[hd, ii:ii + 1, cols], 0.0)
                    ws[ii] = contrib if ws[ii] is None else ws[ii] + contrib
            for ii in range(nsub):
                rows = slice(ii * PEER_N_KEYS + rt * tr, ii * PEER_N_KEYS + (rt + 1) * tr)
                a_scr[rows, cols] = (_gelu(act_scr[rows, cols]) * ws[ii]).astype(BF16)
    acc[...] += _dot(vt_ref[...], a_scr[...])

    @pl.when(e == pl.num_programs(2) - 1)
    def _():
        out = acc[...].T
        o_ref[...] = x1_ref[...] + mod_ref[5:6, :] * (_rms(out) * gain_ref[...])


def _peer(ht, u_tab, v_tab, c1, e1, r2, p2, x1, mods, gain, tn, eb):
    bsz, t_len, d = x1.shape
    nt = t_len // tn
    n_exp = u_tab.shape[0]
    nsub = eb // PEER_N_KEYS
    ne = n_exp // eb
    n_tok = bsz * t_len
    vt_tab = v_tab.reshape(ne, eb, d).transpose(0, 2, 1)
    assert c1.shape[1:3] == (ne, nsub)
    sub = pl.BlockSpec((PEER_HEADS, None, nsub, tn), lambda b, i, e: (0, e, 0, b * nt + i))
    hk = pl.BlockSpec((PEER_HEADS, PEER_N_KEYS, tn), lambda b, i, e: (0, 0, b * nt + i))
    return pl.pallas_call(
        functools.partial(_peer_kernel, nsub=nsub),
        grid=(bsz, nt, ne),
        in_specs=[pl.BlockSpec((d, tn), lambda b, i, e: (0, b * nt + i)),
                  pl.BlockSpec((eb, d), lambda b, i, e: (e, 0)),
                  pl.BlockSpec((None, d, eb), lambda b, i, e: (e, 0, 0)),
                  sub, sub, hk, hk,
                  pl.BlockSpec((None, tn, d), lambda b, i, e: (b, i, 0)),
                  pl.BlockSpec((None, 6, d), lambda b, i, e: (b, 0, 0)),
                  pl.BlockSpec((1, d), lambda b, i, e: (0, 0))],
        out_specs=pl.BlockSpec((None, tn, d), lambda b, i, e: (b, i, 0)),
        out_shape=jax.ShapeDtypeStruct(x1.shape, F32),
        scratch_shapes=[pltpu.VMEM((d, tn), F32), pltpu.VMEM((eb, tn), F32), pltpu.VMEM((eb, tn), BF16)],
        compiler_params=_params(("arbitrary", "arbitrary", "arbitrary")),
        name="peer_experts",
    )(ht, u_tab, vt_tab, c1, e1, r2, p2, x1, mods, gain)


def _s5_coeffs(lam_re, lam_im, log_step, b_re, b_im, c_re, c_im):
    step = jnp.exp(log_step)[:, None]
    mag = jnp.exp(lam_re * step)
    ab_re, ab_im = mag * jnp.cos(lam_im * step), mag * jnp.sin(lam_im * step)
    den = lam_re * lam_re + lam_im * lam_im
    f_re = ((ab_re - 1.0) * lam_re + ab_im * lam_im) / den
    f_im = (ab_im * lam_re - (ab_re - 1.0) * lam_im) / den
    bb_re = f_re[..., None] * b_re - f_im[..., None] * b_im
    bb_im = f_re[..., None] * b_im + f_im[..., None] * b_re
    lam = jnp.stack([ab_re.reshape(-1), ab_im.reshape(-1)])
    eye8 = jnp.eye(8, dtype=F32)

    def in_blockdiag(bb):
        t = bb.reshape(4, 8, S5_STATE, S5_GROUP).transpose(0, 1, 3, 2)
        return jnp.einsum('jghp,gk->jghkp', t, eye8).reshape(4, LANES, 8 * S5_STATE)

    def out_blockdiag(cc):
        t = cc.reshape(4, 8, S5_GROUP, S5_STATE).transpose(0, 1, 3, 2)
        return jnp.einsum('jgph,gk->jgpkh', t, eye8).reshape(4, 8 * S5_STATE, LANES)

    wb = jnp.concatenate([in_blockdiag(bb_re), in_blockdiag(bb_im)], axis=-1).astype(BF16)
    cmat = jnp.stack([out_blockdiag(c_re), -out_blockdiag(c_im)], axis=1).astype(BF16)
    return wb, lam, cmat


def _gate_rows(a_log, dt_bias):
    neg_a = -jnp.exp(a_log)
    acoef = jnp.zeros((LANES,), F32).at[0:8].set(neg_a[0]).at[16:24].set(neg_a[1])
    dtb = jnp.zeros((LANES,), F32).at[0:8].set(dt_bias[0]).at[16:24].set(dt_bias[1])
    return acoef[None, :], dtb[None, :], acoef[:32, None], dtb[:32, None]


def kernel(x, c, ctx, c_ctx, w_mod, b_mod, norm_pre_mix, norm_post_mix, norm_pre_ffn, norm_post_ffn, w_in, s5_lam_re, s5_lam_im, s5_log_step, s5_b_re, s5_b_im, s5_c_re, s5_c_im, s5_d, s5_glu_w, s5_glu_b, dn_conv_w, dn_a_log, dn_dt_bias, dn_norm_w, w_branch_s5, w_branch_dn, w_out, peer_w_q, peer_sub_keys, peer_u, peer_v):
    bsz, t_len, d = x.shape
    ctx_len = ctx.shape[1]
    l = 0

    pad = (-(bsz + 1)) % 8
    cv = jnp.concatenate([c, c_ctx[None, :], jnp.zeros((pad, d), F32)], axis=0)
    mod_all = _adaln(cv, w_mod[l].astype(BF16), b_mod[l][None, :])
    mods = mod_all[:bsz].reshape(bsz, 6, d)
    mods_ctx = jnp.broadcast_to(mod_all[bsz].reshape(1, 6, d), (bsz, 6, d))

    w = w_in[l]
    w_kvq = jnp.concatenate([w[:, COL_K:COL_AB], w[:, COL_Q:COL_Z]], axis=1).astype(BF16)
    w_gates = w[:, COL_Z:].astype(BF16)
    w_ctx = w[:, COL_K:COL_AB].astype(BF16)
    w_u = w[:, COL_U:COL_K].astype(BF16)
    w_ab32 = w[:, COL_AB:COL_Q]
    w_ab = jnp.pad(w_ab32, ((0, 0), (0, LANES - 32))).astype(BF16)
    w_abt = w_ab32.T.astype(BF16)
    gain_pre = norm_pre_mix[l][None, :]
    conv_w = dn_conv_w[l]

    kv_c, _, u_c, ab_c, abt_c = _inproj(ctx, mods_ctx, gain_pre, w_ctx, conv_w[:, :2 * DN_WIDTH], None,
                                        w_u, w_ab, w_abt, tm=ctx_len, tn=2 * DN_WIDTH, row_len=ctx_len)
    kvq_x, gates_x, u_x, ab_x, abt_x = _inproj(x, mods, gain_pre, w_kvq, conv_w, w_gates, w_u, w_ab, w_abt,
                                               tm=min(1024, t_len), tn=1536, row_len=GRID_W)

    wbf, lamf, cf = _s5_coeffs(s5_lam_re[l, 0], s5_lam_im[l, 0], s5_log_step[l, 0], s5_b_re[l, 0], s5_b_im[l, 0],
                               s5_c_re[l, 0], s5_c_im[l, 0])
    wbb, lamb, cb = _s5_coeffs(s5_lam_re[l, 1], s5_lam_im[l, 1], s5_log_step[l, 1], s5_b_re[l, 1], s5_b_im[l, 1],
                               s5_c_re[l, 1], s5_c_im[l, 1])
    zeros_h = jnp.zeros((bsz, 2 * S5_COLS), F32)
    tc = 32
    hf_c, hb_c = _s5(u_c.reshape(bsz, ctx_len * S5_WIDTH), wbf, wbb, lamf, lamb, cf, cb, zeros_h, zeros_h, tc, False)
    yf, yb, _, _ = _s5(u_x.reshape(bsz, t_len * S5_WIDTH), wbf, wbb, lamf, lamb, cf, cb, hf_c, hb_c, tc, True)
    yf = yf.reshape(bsz, t_len, S5_WIDTH)
    yb = yb.reshape(bsz, t_len, S5_WIDTH)

    acoef, dtb, acoeft, dtbt = _gate_rows(dn_a_log[l], dn_dt_bias[l])
    zeros_s = jnp.zeros((bsz, DN_HEADS, DN_HEAD_DIM, DN_HEAD_DIM), F32)
    dn_args = (acoef, dtb, acoeft, dtbt)
    (sf_c,) = _deltanet(kv_c, ab_c, abt_c, *dn_args, zeros_s, ctx_len, False, False)
    (sb_c,) = _deltanet(kv_c, ab_c, abt_c, *dn_args, zeros_s, ctx_len, True, False)
    blk_len = min(512, t_len)
    o_f, _ = _deltanet(kvq_x, ab_x, abt_x, *dn_args, sf_c, blk_len, False, True)
    o_b, _ = _deltanet(kvq_x, ab_x, abt_x, *dn_args, sb_c, blk_len, True, True)

    x1 = _post(x, mods, yf, yb, u_x, o_f, o_b, gates_x,
               s5_d[l][None, :], s5_glu_w[l].astype(BF16), s5_glu_b[l][None, :], dn_norm_w[l][None, :],
               w_branch_s5[l].astype(BF16), w_branch_dn[l].astype(BF16), w_out[l].astype(BF16),
               norm_post_mix[l][None, :], tm=min(512, t_len))

    ht, c1, e1, r2, p2 = _route(x1, mods, norm_pre_ffn[l][None, :], peer_w_q[l].astype(BF16),
                                peer_sub_keys[l].astype(BF16), tm=min(512, t_len))
    out = _peer(ht, peer_u[l].astype(BF16), peer_v[l].astype(BF16), c1, e1, r2, p2, x1, mods,
                norm_post_ffn[l][None, :], tn=min(1024, t_len), eb=PEER_EXPERT_BLOCK)
    return out
```

```python
import functools
import math

import jax
import jax.numpy as jnp
from jax import lax
from jax.experimental import pallas as pl
from jax.experimental.pallas import tpu as pltpu

F32 = jnp.float32
BF16 = jnp.bfloat16

D_MODEL = 1024
NORM_EPS = 1e-6
GRID_W = 64
S5_WIDTH = 512
S5_GROUP = 16
S5_GROUPS = 32
S5_STATE = 64
S5_COLS = S5_GROUPS * S5_STATE
DN_HEADS = 8
DN_HEAD_DIM = 128
DN_WIDTH = 1024
DN_CONV = 5
DN_CHUNK = 64
DN_GATE_COLS = 4 * DN_HEADS
PEER_HEADS = 8
PEER_N_KEYS = 128
PEER_KEY_DIM = 128
PEER_TOPK = 16
LANES = 128

COL_U = 0
COL_K = COL_U + S5_WIDTH
COL_AB = COL_K + 2 * DN_WIDTH
COL_Q = COL_AB + 4 * DN_HEADS
COL_Z = COL_Q + DN_WIDTH

INPROJ_TOKENS = 1024
INPROJ_COLS = 1536
S5_TIME_CHUNK = 32
DN_BLOCK_TOKENS = 512
MIXER_TOKENS = 512
ROUTE_TOKENS = 512
ROUTE_HEADS_PER_TRIP = 2
PEER_TOKENS = 1024
PEER_EXPERT_BLOCK = 512
PEER_SUB = PEER_EXPERT_BLOCK // PEER_N_KEYS

VMEM_LIMIT = 56 * 1024 * 1024


def _dot(a, b):
    return jnp.dot(a, b, preferred_element_type=F32)


def _dot_nt(a, b):
    return lax.dot_general(a, b, (((1,), (1,)), ((), ())), preferred_element_type=F32)


def _dot_tn(a, b):
    return lax.dot_general(a, b, (((0,), (0,)), ((), ())), preferred_element_type=F32)


def _sigmoid(x):
    return jax.nn.sigmoid(x)


def _silu(x):
    return x * _sigmoid(x)


def _gelu(x):
    c = math.sqrt(2.0 / math.pi)
    return x * (0.5 * (1.0 + jnp.tanh(c * (x + 0.044715 * (x * x * x)))))


def _softplus(x):
    return jnp.maximum(x, 0.0) + jnp.log1p(jnp.exp(-jnp.abs(x)))


def _rms(x):
    return x * lax.rsqrt(jnp.mean(x * x, axis=-1, keepdims=True) + NORM_EPS)


def _params(sem):
    return pltpu.CompilerParams(dimension_semantics=sem, vmem_limit_bytes=VMEM_LIMIT)


def _adaln_kernel(c_ref, w_ref, b_ref, o_ref):
    c = c_ref[...]
    o_ref[...] = _dot(_silu(c).astype(BF16), w_ref[...]) + b_ref[...]


def _adaln(cv, w_mod, b_mod):
    rows = cv.shape[0]
    ncol = w_mod.shape[1] // D_MODEL
    return pl.pallas_call(
        _adaln_kernel,
        grid=(ncol,),
        in_specs=[pl.BlockSpec((rows, D_MODEL), lambda j: (0, 0)),
                  pl.BlockSpec((D_MODEL, D_MODEL), lambda j: (0, j)),
                  pl.BlockSpec((1, D_MODEL), lambda j: (0, j))],
        out_specs=pl.BlockSpec((rows, D_MODEL), lambda j: (0, j)),
        out_shape=jax.ShapeDtypeStruct((rows, w_mod.shape[1]), F32),
        compiler_params=_params(("arbitrary",)),
        name="adaln",
    )(cv, w_mod, b_mod)


def _inproj_kernel(x_ref, mod_ref, gain_ref, wc_ref, cw_ref, *rest, ncs, row_len, has_gates):
    if has_gates:
        wg_ref, wu_ref, wab_ref, wabt_ref, kvq_ref, gates_ref, u_ref, ab_ref, abt_ref, hx_scr = rest
    else:
        wu_ref, wab_ref, wabt_ref, kvq_ref, u_ref, ab_ref, abt_ref, hx_scr = rest
    j = pl.program_id(2)

    @pl.when(j == 0)
    def _():
        h = _rms(x_ref[...]) * gain_ref[...]
        h = h * (1.0 + mod_ref[1:2, :]) + mod_ref[0:1, :]
        hb = h.astype(BF16)
        hx_scr[...] = hb
        u_ref[...] = _dot(hb, wu_ref[...])
        ab_ref[...] = _dot(hb, wab_ref[...])
        abt_ref[...] = _dot_nt(wabt_ref[...], hb)

    @pl.when(j < ncs)
    def _():
        kvq_ref[...] = _dot(hx_scr[...], wc_ref[...])
        tm, tn = kvq_ref.shape
        tpos = lax.broadcasted_iota(jnp.int32, (tm, LANES), 0) % row_len
        valid = [(tpos + (t - 2) >= 0) & (tpos + (t - 2) < row_len) for t in range(DN_CONV)]
        for cbk in range(tn // LANES):
            cols = slice(cbk * LANES, (cbk + 1) * LANES)
            xc = kvq_ref[:, cols]
            acc = xc * cw_ref[2:3, cols]
            for t in (0, 1, 3, 4):
                acc = acc + jnp.where(valid[t], pltpu.roll(xc, (2 - t) % tm, axis=0), 0.0) * cw_ref[t:t + 1, cols]
            kvq_ref[:, cols] = _silu(acc)

    if has_gates:
        @pl.when(j >= ncs)
        def _():
            gates_ref[...] = _dot(hx_scr[...], wg_ref[...])


def _inproj(x, mods, gain, w_conv, conv_w, w_gate, w_u, w_ab, w_abt, tm, tn, row_len):
    bsz, t_len, d = x.shape
    cc = w_conv.shape[1]
    ncs = cc // tn
    has_gates = w_gate is not None
    ngs = (w_gate.shape[1] // tn) if has_gates else 0
    cstep = lambda j: jnp.minimum(j, ncs - 1)
    gstep = lambda j: jnp.maximum(j - ncs, 0)
    in_specs = [pl.BlockSpec((None, tm, d), lambda b, i, j: (b, i, 0)),
                pl.BlockSpec((None, 6, d), lambda b, i, j: (b, 0, 0)),
                pl.BlockSpec((1, d), lambda b, i, j: (0, 0)),
                pl.BlockSpec((d, tn), lambda b, i, j: (0, cstep(j))),
                pl.BlockSpec((DN_CONV, tn), lambda b, i, j: (0, cstep(j)))]
    args = [x, mods, gain, w_conv, conv_w]
    out_specs = [pl.BlockSpec((None, tm, tn), lambda b, i, j: (b, i, cstep(j)))]
    out_shape = [jax.ShapeDtypeStruct((bsz, t_len, cc), F32)]
    if has_gates:
        in_specs.append(pl.BlockSpec((d, tn), lambda b, i, j: (0, gstep(j))))
        args.append(w_gate)
        out_specs.append(pl.BlockSpec((None, tm, tn), lambda b, i, j: (b, i, gstep(j))))
        out_shape.append(jax.ShapeDtypeStruct((bsz, t_len, w_gate.shape[1]), F32))
    in_specs += [pl.BlockSpec((d, S5_WIDTH), lambda b, i, j: (0, 0)),
                 pl.BlockSpec((d, LANES), lambda b, i, j: (0, 0)),
                 pl.BlockSpec((DN_GATE_COLS, d), lambda b, i, j: (0, 0))]
    args += [w_u, w_ab, w_abt]
    out_specs += [pl.BlockSpec((None, tm, S5_WIDTH), lambda b, i, j: (b, i, 0)),
                  pl.BlockSpec((None, tm, LANES), lambda b, i, j: (b, i, 0)),
                  pl.BlockSpec((None, DN_GATE_COLS, tm), lambda b, i, j: (b, 0, i))]
    out_shape += [jax.ShapeDtypeStruct((bsz, t_len, S5_WIDTH), F32),
                  jax.ShapeDtypeStruct((bsz, t_len, LANES), F32),
                  jax.ShapeDtypeStruct((bsz, DN_GATE_COLS, t_len), F32)]
    outs = pl.pallas_call(
        functools.partial(_inproj_kernel, ncs=ncs, row_len=row_len, has_gates=has_gates),
        grid=(bsz, t_len // tm, ncs + ngs),
        in_specs=in_specs,
        out_specs=out_specs,
        out_shape=out_shape,
        scratch_shapes=[pltpu.VMEM((tm, d), BF16)],
        compiler_params=_params(("arbitrary", "arbitrary", "arbitrary")),
        name="inproj",
    )(*args)
    if has_gates:
        return outs
    return (outs[0], None) + tuple(outs[1:])


def _s5_kernel(uf_ref, ub_ref, wbf_ref, wbb_ref, lamf_ref, lamb_ref, cf_ref, cb_ref,
               h0f_ref, h0b_ref, *rest, tc, bsz, need_out):
    if need_out:
        yf_ref, yb_ref, hff_ref, hfb_ref, xsf, xsb, hcf, hcb = rest
    else:
        hff_ref, hfb_ref, xsf, xsb, hcf, hcb = rest
        yf_ref = yb_ref = None
    i = pl.program_id(0)
    half = S5_COLS
    cw = 512
    nblk = half // cw

    @pl.when(i == 0)
    def _():
        hcf[...] = h0f_ref[...]
        hcb[...] = h0b_ref[...]

    dirs = ((uf_ref, wbf_ref, lamf_ref, cf_ref, yf_ref, xsf, hcf, False),
            (ub_ref, wbb_ref, lamb_ref, cb_ref, yb_ref, xsb, hcb, True))
    for u_ref, wb_ref, lam_ref, c_ref, y_ref, xs, hc, rev in dirs:
        for j in range(4):
            lhs = jnp.concatenate(
                [u_ref[:, t * S5_WIDTH + j * LANES: t * S5_WIDTH + (j + 1) * LANES] for t in range(tc)],
                axis=0).astype(BF16)
            r = _dot(lhs, wb_ref[j])
            xs[:, j * cw:(j + 1) * cw] = r[:, :cw]
            xs[:, half + j * cw: half + (j + 1) * cw] = r[:, cw:]
        for cb in range(nblk):
            lo, hi = cb * cw, (cb + 1) * cw
            ar = jnp.broadcast_to(lam_ref[0:1, lo:hi], (bsz, cw))
            ai = jnp.broadcast_to(lam_ref[1:2, lo:hi], (bsz, cw))

            def body(s, carry, lo=lo, hi=hi, ar=ar, ai=ai, xs=xs, rev=rev):
                hr, hi_ = carry
                t = (tc - 1 - s) if rev else s
                row = pl.multiple_of(t * bsz, bsz)
                xr = xs[pl.ds(row, bsz), lo:hi]
                xi = xs[pl.ds(row, bsz), half + lo: half + hi]
                nr = ar * hr - ai * hi_ + xr
                ni = ar * hi_ + ai * hr + xi
                xs[pl.ds(row, bsz), lo:hi] = nr
                xs[pl.ds(row, bsz), half + lo: half + hi] = ni
                return nr, ni

            hr, hi_ = lax.fori_loop(0, tc, body, (hc[:, lo:hi], hc[:, half + lo: half + hi]))
            hc[:, lo:hi] = hr
            hc[:, half + lo: half + hi] = hi_
        if need_out:
            for j in range(4):
                hre = xs[:, j * cw:(j + 1) * cw].astype(BF16)
                him = xs[:, half + j * cw: half + (j + 1) * cw].astype(BF16)
                y = _dot(hre, c_ref[j, 0]) + _dot(him, c_ref[j, 1])
                for t in range(tc):
                    y_ref[:, t * S5_WIDTH + j * LANES: t * S5_WIDTH + (j + 1) * LANES] = y[t * bsz:(t + 1) * bsz]
    hff_ref[...] = hcf[...]
    hfb_ref[...] = hcb[...]


def _s5(u, wbf, wbb, lamf, lamb, cf, cb, h0f, h0b, tc, need_out):
    bsz = u.shape[0]
    t_len = u.shape[1] // S5_WIDTH
    nc = t_len // tc
    const2 = lambda i: (0, 0)
    const3 = lambda i: (0, 0, 0)
    const4 = lambda i: (0, 0, 0, 0)
    st_spec = pl.BlockSpec((bsz, 2 * S5_COLS), const2)
    st_shape = jax.ShapeDtypeStruct((bsz, 2 * S5_COLS), F32)
    out_specs = [st_spec, st_spec]
    out_shape = [st_shape, st_shape]
    fwd_blk = pl.BlockSpec((bsz, tc * S5_WIDTH), lambda i: (0, i))
    bwd_blk = pl.BlockSpec((bsz, tc * S5_WIDTH), lambda i: (0, nc - 1 - i))
    if need_out:
        y_shape = jax.ShapeDtypeStruct(u.shape, F32)
        out_specs = [fwd_blk, bwd_blk] + out_specs
        out_shape = [y_shape, y_shape] + out_shape
    return pl.pallas_call(
        functools.partial(_s5_kernel, tc=tc, bsz=bsz, need_out=need_out),
        grid=(nc,),
        in_specs=[fwd_blk, bwd_blk,
                  pl.BlockSpec(wbf.shape, const3), pl.BlockSpec(wbb.shape, const3),
                  pl.BlockSpec(lamf.shape, const2), pl.BlockSpec(lamb.shape, const2),
                  pl.BlockSpec(cf.shape, const4), pl.BlockSpec(cb.shape, const4),
                  st_spec, st_spec],
        out_specs=out_specs,
        out_shape=out_shape,
        scratch_shapes=[pltpu.VMEM((tc * bsz, 2 * S5_COLS), F32),
                        pltpu.VMEM((tc * bsz, 2 * S5_COLS), F32),
                        pltpu.VMEM((bsz, 2 * S5_COLS), F32),
                        pltpu.VMEM((bsz, 2 * S5_COLS), F32)],
        compiler_params=_params(("arbitrary",)),
        name="s5_scan",
    )(u, u, wbf, wbb, lamf, lamb, cf, cb, h0f, h0b)


def _dn_kernel(act_scr, ab_ref, abt_ref, acoef_ref, dtb_ref, acoeft_ref, dtbt_ref, s0_ref,
               *rest, blk_len, rev, need_out):
    if need_out:
        o_ref, sfin_ref, s_scr = rest
    else:
        sfin_ref, s_scr = rest
        o_ref = None
    cs = DN_CHUNK
    nch = blk_len // cs
    goff = 2 * DN_HEADS if rev else 0
    boff = goff + DN_HEADS

    @pl.when(pl.program_id(1) == 0)
    def _():
        s_scr[...] = s0_ref[...]

    ab = ab_ref[...]
    g_cols = acoef_ref[...] * _softplus(ab + dtb_ref[...])
    beta_cols = _sigmoid(ab)
    g_rows = acoeft_ref[...] * _softplus(abt_ref[...] + dtbt_ref[...])

    ri = lax.broadcasted_iota(jnp.int32, (cs, cs), 0)
    ci = lax.broadcasted_iota(jnp.int32, (cs, cs), 1)
    incl = (ri <= ci) if rev else (ri >= ci)
    strict = incl & (ri != ci)
    tri = incl.astype(F32)
    eye = (ri == ci).astype(F32)
    last = 0 if rev else cs - 1
    scale = DN_HEAD_DIM ** -0.5

    order = list(range(nch - 1, -1, -1)) if rev else list(range(nch))
    units = [(c, h) for c in order for h in range(DN_HEADS)]
    nu = len(units)
    gc_cols = {}
    gc_rows = {}
    for c in order:
        rows = slice(c * cs, (c + 1) * cs)
        gc_cols[c] = jnp.dot(tri, g_cols[rows, :], precision=lax.Precision.HIGHEST, preferred_element_type=F32)
        gc_rows[c] = lax.dot_general(g_rows[:, rows], tri, (((1,), (1,)), ((), ())),
                                     precision=lax.Precision.HIGHEST, preferred_element_type=F32)

    def head_block(base, c, h):
        return act_scr[c * cs:(c + 1) * cs, base + h * DN_HEAD_DIM: base + (h + 1) * DN_HEAD_DIM]

    def l2n(t):
        return t * lax.rsqrt(jnp.sum(t * t, axis=-1, keepdims=True) + 1e-6)

    gch = [gc_cols[c][:, goff + h: goff + h + 1] for c, h in units]
    gcr = [gc_rows[c][goff + h: goff + h + 1, :] for c, h in units]
    dec = [jnp.where(incl, jnp.exp(jnp.minimum(gch[u] - gcr[u], 0.0)), 0.0) for u in range(nu)]
    bh = [beta_cols[c * cs:(c + 1) * cs, boff + h: boff + h + 1] for c, h in units]
    k = [l2n(head_block(0, c, h)) for c, h in units]
    kbf = [t.astype(BF16) for t in k]
    kb = [k[u] * bh[u] for u in range(nu)]
    kkt = [_dot_nt(kb[u].astype(BF16), kbf[u]) for u in range(nu)]
    a = [jnp.where(strict, kkt[u] * dec[u], 0.0) for u in range(nu)]
    abf = [t.astype(BF16) for t in a]
    m = [_dot(t, t) for t in abf]
    p = [eye - t for t in a]
    for lvl in range(5):
        mb = [t.astype(BF16) for t in m]
        if lvl < 4:
            r = [_dot(jnp.concatenate([m[u], p[u]], axis=0).astype(BF16), mb[u]) for u in range(nu)]
            m = [t[:cs] for t in r]
            p = [p[u] + r[u][cs:] for u in range(nu)]
        else:
            r = [_dot(p[u].astype(BF16), mb[u]) for u in range(nu)]
            p = [p[u] + r[u] for u in range(nu)]
    eg = [jnp.exp(t) for t in gch]
    rhs = [jnp.concatenate([head_block(DN_WIDTH, c, h) * bh[u], kb[u] * eg[u]], axis=-1).astype(BF16)
           for u, (c, h) in enumerate(units)]
    uw = [_dot(p[u].astype(BF16), rhs[u]) for u in range(nu)]
    gl = [t[last:last + 1, :] for t in gch]
    kd = [(k[u] * jnp.exp(gl[u] - gch[u])).astype(BF16) for u in range(nu)]
    if need_out:
        q = [l2n(head_block(2 * DN_WIDTH, c, h)) * scale for c, h in units]
        qkt = [_dot_nt(q[u].astype(BF16), kbf[u]) for u in range(nu)]
        at = [jnp.where(incl, qkt[u] * dec[u], 0.0).astype(BF16) for u in range(nu)]
        qd = [(q[u] * eg[u]).astype(BF16) for u in range(nu)]

    for ci, c in enumerate(order):
        us = range(ci * DN_HEADS, (ci + 1) * DN_HEADS)
        s_old = [s_scr[h] for h in range(DN_HEADS)]
        s_bf = [t.astype(BF16) for t in s_old]
        ws = [_dot(uw[u][:, DN_HEAD_DIM:].astype(BF16), s_bf[h]) for h, u in enumerate(us)]
        if need_out:
            qs = [_dot(qd[u], s_bf[h]) for h, u in enumerate(us)]
        vn = [(uw[u][:, :DN_HEAD_DIM] - ws[h]).astype(BF16) for h, u in enumerate(us)]
        if need_out:
            av = [_dot(at[u], vn[h]) for h, u in enumerate(us)]
        kv = [_dot_tn(kd[u], vn[h]) for h, u in enumerate(us)]
        for h, u in enumerate(us):
            if need_out:
                o_ref[c * cs:(c + 1) * cs, h * DN_HEAD_DIM:(h + 1) * DN_HEAD_DIM] = qs[h] + av[h]
            s_scr[h] = s_old[h] * jnp.exp(gl[u]) + kv[h]
    sfin_ref[...] = s_scr[...]


def _deltanet(kvq, ab, abt, acoef, dtb, acoeft, dtbt, s0, blk_len, rev, need_out):
    bsz, t_len, ncols = kvq.shape
    nb = t_len // blk_len
    bi = (lambda b, i: (b, nb - 1 - i, 0)) if rev else (lambda b, i: (b, i, 0))
    bit = (lambda b, i: (b, 0, nb - 1 - i)) if rev else (lambda b, i: (b, 0, i))
    c2 = lambda b, i: (0, 0)
    st_spec = pl.BlockSpec((None, DN_HEADS, DN_HEAD_DIM, DN_HEAD_DIM), lambda b, i: (b, 0, 0, 0))
    st_shape = jax.ShapeDtypeStruct((bsz, DN_HEADS, DN_HEAD_DIM, DN_HEAD_DIM), F32)
    out_specs = [st_spec]
    out_shape = [st_shape]
    if need_out:
        out_specs = [pl.BlockSpec((None, blk_len, DN_WIDTH), bi)] + out_specs
        out_shape = [jax.ShapeDtypeStruct((bsz, t_len, DN_WIDTH), F32)] + out_shape
    return pl.pallas_call(
        functools.partial(_dn_kernel, blk_len=blk_len, rev=rev, need_out=need_out),
        grid=(bsz, nb),
        in_specs=[pl.BlockSpec((None, blk_len, ncols), bi),
                  pl.BlockSpec((None, blk_len, LANES), bi),
                  pl.BlockSpec((None, DN_GATE_COLS, blk_len), bit),
                  pl.BlockSpec((1, LANES), c2), pl.BlockSpec((1, LANES), c2),
                  pl.BlockSpec((DN_GATE_COLS, 1), c2), pl.BlockSpec((DN_GATE_COLS, 1), c2),
                  st_spec],
        out_specs=out_specs,
        out_shape=out_shape,
        scratch_shapes=[pltpu.VMEM((DN_HEADS, DN_HEAD_DIM, DN_HEAD_DIM), F32)],
        compiler_params=_params(("arbitrary", "arbitrary")),
        name="deltanet_bwd" if rev else "deltanet_fwd",
    )(kvq, ab, abt, acoef, dtb, acoeft, dtbt, s0)


def _post_kernel(x_ref, mod_ref, yf_ref, yb_ref, u_ref, of_ref, ob_ref, z_ref, g5_ref, gd_ref,
                 s5d_ref, glw_ref, glb_ref, dnw_ref, wb5_ref, wbd_ref, wo_ref, gain_ref, o_ref):
    y5 = _gelu(yf_ref[...] + yb_ref[...] + s5d_ref[...] * u_ref[...])
    y5 = y5 * _sigmoid(_dot(y5.astype(BF16), glw_ref[...]) + glb_ref[...])
    o = of_ref[...] + ob_ref[...]
    parts = []
    for h in range(DN_HEADS):
        oh = o[:, h * DN_HEAD_DIM:(h + 1) * DN_HEAD_DIM]
        parts.append(_rms(oh) * dnw_ref[...])
    y_dn = jnp.concatenate(parts, axis=-1) * _silu(z_ref[...])
    merged = (_sigmoid(g5_ref[...]) * _dot(y5.astype(BF16), wb5_ref[...])
              + _sigmoid(gd_ref[...]) * _dot(y_dn.astype(BF16), wbd_ref[...]))
    x_mix = _dot(merged.astype(BF16), wo_ref[...])
    o_ref[...] = x_ref[...] + mod_ref[2:3, :] * (_rms(x_mix) * gain_ref[...])


def _post(x, mods, yf, yb, u, o_f, o_b, gates, s5d, glw, glb, dnw, wb5, wbd, wo, gain, tm):
    bsz, t_len, d = x.shape
    tok = lambda w: pl.BlockSpec((None, tm, w), lambda b, i: (b, i, 0))
    colblk = lambda k: pl.BlockSpec((None, tm, d), lambda b, i, k=k: (b, i, k))
    full = lambda a: pl.BlockSpec(a.shape, lambda b, i: (0,) * a.ndim)
    return pl.pallas_call(
        _post_kernel,
        grid=(bsz, t_len // tm),
        in_specs=[tok(d), pl.BlockSpec((None, 6, d), lambda b, i: (b, 0, 0)),
                  tok(S5_WIDTH), tok(S5_WIDTH), tok(S5_WIDTH), tok(d), tok(d),
                  colblk(0), colblk(1), colblk(2),
                  full(s5d), full(glw), full(glb), full(dnw), full(wb5), full(wbd), full(wo), full(gain)],
        out_specs=tok(d),
        out_shape=jax.ShapeDtypeStruct(x.shape, F32),
        compiler_params=_params(("arbitrary", "arbitrary")),
        name="mixer_out",
    )(x, mods, yf, yb, u, o_f, o_b, gates, gates, gates, s5d, glw, glb, dnw, wb5, wbd, wo, gain)


_CAND_COUNTS = [PEER_TOPK // (a + 1) for a in range(PEER_TOPK)]
_N_CAND = sum(_CAND_COUNTS)
_CAND_ROWS = -(-_N_CAND // 8) * 8


def _extract_topk(work, k, out_vals_ref=None, want_rank=False, break_ties=True):
    nrows = work.shape[0]
    iota = lax.broadcasted_iota(jnp.int32, work.shape, 0)
    rank = jnp.full(work.shape, float(k), F32) if want_rank else None
    vals = []
    for r in range(k):
        m = jnp.max(work, axis=0, keepdims=True)
        if break_ties:
            idx = jnp.min(jnp.where(work == m, iota, nrows), axis=0, keepdims=True)
            sel = iota == idx
        else:
            sel = work == m
        work = jnp.where(sel, -jnp.inf, work)
        if want_rank:
            rank = jnp.where(sel, float(r), rank)
        vals.append(m)
        if out_vals_ref is not None:
            out_vals_ref[r:r + 1, :] = m
    return vals, work, rank


def _route_kernel(x_ref, mod_ref, gain_ref, wq_ref, keys_ref,
                  ht_ref, c1_ref, e1_ref, r2_ref, p2_ref, q_scr, *staging):
    h = _rms(x_ref[...]) * gain_ref[...]
    h = h * (1.0 + mod_ref[4:5, :]) + mod_ref[3:4, :]
    hb = h.astype(BF16)
    ht_ref[...] = h.T.astype(BF16)
    q_scr[...] = _dot(hb, wq_ref[...]).astype(BF16)
    topk = float(PEER_TOPK)

    def route_head(hd, scores, break_ties, sv_scr, cand_scr):
        svals = []
        ranks = []
        for j in range(2):
            vals, _, rank = _extract_topk(scores[j], PEER_TOPK, sv_scr.at[j], want_rank=(break_ties or j == 1),
                                          break_ties=break_ties)
            svals.append(vals)
            ranks.append(rank)
        if break_ties:
            is_rank = [ranks[0] == float(a) for a in range(PEER_TOPK)]
            member = [r < topk for r in ranks]
        else:
            is_rank = [scores[0] == svals[0][a] for a in range(PEER_TOPK)]
            member = [scores[0] >= svals[0][PEER_TOPK - 1], ranks[1] < topk]
        cand_scr[...] = jnp.full(cand_scr.shape, -jnp.inf, F32)
        off = 0
        for a in range(PEER_TOPK):
            nb = _CAND_COUNTS[a]
            cand_scr[off: off + nb, :] = svals[0][a] + sv_scr[1, 0:nb, :]
            off += nb
        top, left, _ = _extract_topk(cand_scr[...], PEER_TOPK, break_ties=break_ties)
        m0 = top[0]
        z = jnp.exp(top[0] - m0)
        for r in range(1, PEER_TOPK):
            z = z + jnp.exp(top[r] - m0)
        taken = (left == -jnp.inf).astype(F32)
        c1 = jnp.zeros(scores[0].shape, F32)
        n_taken = jnp.zeros_like(m0)
        off = 0
        for a in range(PEER_TOPK):
            nb = _CAND_COUNTS[a]
            cnt = jnp.sum(taken[off: off + nb, :], axis=0, keepdims=True)
            c1 = jnp.where(is_rank[a], cnt, c1)
            n_taken = n_taken + cnt
            off += nb
        e1 = jnp.where(member[0], jnp.exp(scores[0] - svals[0][0]), 0.0)
        for g in range(PEER_N_KEYS // PEER_SUB):
            c1_ref[hd, g] = c1[g * PEER_SUB:(g + 1) * PEER_SUB]
            e1_ref[hd, g] = e1[g * PEER_SUB:(g + 1) * PEER_SUB]
        r2_ref[hd] = ranks[1]
        p2_ref[hd] = jnp.where(member[1], jnp.exp(scores[1] - svals[1][0]), 0.0) / z
        n_member = [jnp.sum(mb.astype(F32), axis=0, keepdims=True) for mb in member]
        return jnp.abs(n_member[0] - topk) + jnp.abs(n_member[1] - topk) + jnp.abs(n_taken - topk)

    per_trip = len(staging) // 2
    buffers = [(staging[2 * k], staging[2 * k + 1]) for k in range(per_trip)]

    def head_group(grp, carry):
        work = []
        for k, (sv_scr, cand_scr) in enumerate(buffers):
            hd = grp * per_trip + k
            scores = []
            for j in range(2):
                col = pl.multiple_of((hd * 2 + j) * PEER_KEY_DIM, PEER_KEY_DIM)
                scores.append(_dot_nt(keys_ref[hd, j], q_scr[:, pl.ds(col, PEER_KEY_DIM)]))
            work.append((hd, scores, sv_scr, cand_scr))
        excess = [jnp.max(route_head(hd, scores, False, sv_scr, cand_scr)) for hd, scores, sv_scr, cand_scr in work]
        for (hd, scores, sv_scr, cand_scr), ex in zip(work, excess):
            @pl.when(ex > 0.0)
            def _(hd=hd, scores=scores, sv_scr=sv_scr, cand_scr=cand_scr):
                route_head(hd, scores, True, sv_scr, cand_scr)

        return carry

    lax.fori_loop(0, PEER_HEADS // per_trip, head_group, 0)


def _route(x1, mods, gain, wq, keys, tm):
    bsz, t_len, d = x1.shape
    n_tok = bsz * t_len
    nt = t_len // tm
    tile = lambda b, i: (0, 0, b * nt + i)
    hk = pl.BlockSpec((PEER_HEADS, PEER_N_KEYS, tm), tile)
    hk_shape = jax.ShapeDtypeStruct((PEER_HEADS, PEER_N_KEYS, n_tok), F32)
    nblk = PEER_N_KEYS // PEER_SUB
    sub = pl.BlockSpec((PEER_HEADS, nblk, PEER_SUB, tm), lambda b, i: (0, 0, 0, b * nt + i))
    sub_shape = jax.ShapeDtypeStruct((PEER_HEADS, nblk, PEER_SUB, n_tok), F32)
    return pl.pallas_call(
        _route_kernel,
        grid=(bsz, nt),
        in_specs=[pl.BlockSpec((None, tm, d), lambda b, i: (b, i, 0)),
                  pl.BlockSpec((None, 6, d), lambda b, i: (b, 0, 0)),
                  pl.BlockSpec((1, d), lambda b, i: (0, 0)),
                  pl.BlockSpec(wq.shape, lambda b, i: (0, 0)),
                  pl.BlockSpec(keys.shape, lambda b, i: (0, 0, 0, 0))],
        out_specs=[pl.BlockSpec((d, tm), lambda b, i: (0, b * nt + i)),
                   sub, sub, hk, hk],
        out_shape=[jax.ShapeDtypeStruct((d, n_tok), BF16),
                   sub_shape, sub_shape, hk_shape, hk_shape],
        scratch_shapes=[pltpu.VMEM((tm, wq.shape[1]), BF16)]
        + [pltpu.VMEM((2, PEER_TOPK, tm), F32), pltpu.VMEM((_CAND_ROWS, tm), F32)] * ROUTE_HEADS_PER_TRIP,
        compiler_params=_params(("arbitrary", "arbitrary")),
        name="peer_route",
    )(x1, mods, gain, wq, keys)


def _peer_kernel(ht_ref, u_ref, vt_ref, c1_ref, e1_ref, r2_ref, p2_ref, x1_ref, mod_ref, gain_ref,
                 o_ref, acc, act_scr, a_scr, *, nsub):
    e = pl.program_id(2)

    @pl.when(e == 0)
    def _():
        acc[...] = jnp.zeros(acc.shape, F32)

    ncol = act_scr.shape[1] // LANES
    tr = 32
    nrt = PEER_N_KEYS // tr

    act_scr[...] = _dot(u_ref[...], ht_ref[...])
    for ct in range(ncol):
        cols = slice(ct * LANES, (ct + 1) * LANES)
        for rt in range(nrt):
            krows = slice(rt * tr, (rt + 1) * tr)
            ws = [None] * nsub
            for hd in range(PEER_HEADS):
                r2t = r2_ref[hd, krows, cols]
                p2t = p2_ref[hd, krows, cols]
                for ii in range(nsub):
                    contrib = jnp.where(r2t < c1_ref[hd, ii:ii + 1, cols], p2t * e1_ref[hd, ii:ii + 1, cols], 0.0)
                    ws[ii] = contrib if ws[ii] is None else ws[ii] + contrib
            for ii in range(nsub):
                rows = slice(ii * PEER_N_KEYS + rt * tr, ii * PEER_N_KEYS + (rt + 1) * tr)
                a_scr[rows, cols] = (_gelu(act_scr[rows, cols]) * ws[ii]).astype(BF16)
    acc[...] += _dot(vt_ref[...], a_scr[...])

    @pl.when(e == pl.num_programs(2) - 1)
    def _():
        out = acc[...].T
        o_ref[...] = x1_ref[...] + mod_ref[5:6, :] * (_rms(out) * gain_ref[...])


def _peer(ht, u_tab, v_tab, c1, e1, r2, p2, x1, mods, gain, tn, eb):
    bsz, t_len, d = x1.shape
    nt = t_len // tn
    n_exp = u_tab.shape[0]
    nsub = eb // PEER_N_KEYS
    ne = n_exp // eb
    n_tok = bsz * t_len
    vt_tab = v_tab.reshape(ne, eb, d).transpose(0, 2, 1)
    assert c1.shape[1:3] == (ne, nsub)
    sub = pl.BlockSpec((PEER_HEADS, None, nsub, tn), lambda b, i, e: (0, e, 0, b * nt + i))
    hk = pl.BlockSpec((PEER_HEADS, PEER_N_KEYS, tn), lambda b, i, e: (0, 0, b * nt + i))
    return pl.pallas_call(
        functools.partial(_peer_kernel, nsub=nsub),
        grid=(bsz, nt, ne),
        in_specs=[pl.BlockSpec((d, tn), lambda b, i, e: (0, b * nt + i)),
                  pl.BlockSpec((eb, d), lambda b, i, e: (e, 0)),
                  pl.BlockSpec((None, d, eb), lambda b, i, e: (e, 0, 0)),
                  sub, sub, hk, hk,
                  pl.BlockSpec((None, tn, d), lambda b, i, e: (b, i, 0)),
                  pl.BlockSpec((None, 6, d), lambda b, i, e: (b, 0, 0)),
                  pl.BlockSpec((1, d), lambda b, i, e: (0, 0))],
        out_specs=pl.BlockSpec((None, tn, d), lambda b, i, e: (b, i, 0)),
        out_shape=jax.ShapeDtypeStruct(x1.shape, F32),
        scratch_shapes=[pltpu.VMEM((d, tn), F32), pltpu.VMEM((eb, tn), F32), pltpu.VMEM((eb, tn), BF16)],
        compiler_params=_params(("arbitrary", "arbitrary", "arbitrary")),
        name="peer_experts",
    )(ht, u_tab, vt_tab, c1, e1, r2, p2, x1, mods, gain)


def _s5_coeffs(lam_re, lam_im, log_step, b_re, b_im, c_re, c_im):
    step = jnp.exp(log_step)[:, None]
    mag = jnp.exp(lam_re * step)
    ab_re, ab_im = mag * jnp.cos(lam_im * step), mag * jnp.sin(lam_im * step)
    den = lam_re * lam_re + lam_im * lam_im
    f_re = ((ab_re - 1.0) * lam_re + ab_im * lam_im) / den
    f_im = (ab_im * lam_re - (ab_re - 1.0) * lam_im) / den
    bb_re = f_re[..., None] * b_re - f_im[..., None] * b_im
    bb_im = f_re[..., None] * b_im + f_im[..., None] * b_re
    lam = jnp.stack([ab_re.reshape(-1), ab_im.reshape(-1)])
    eye8 = jnp.eye(8, dtype=F32)

    def in_blockdiag(bb):
        t = bb.reshape(4, 8, S5_STATE, S5_GROUP).transpose(0, 1, 3, 2)
        return jnp.einsum('jghp,gk->jghkp', t, eye8).reshape(4, LANES, 8 * S5_STATE)

    def out_blockdiag(cc):
        t = cc.reshape(4, 8, S5_GROUP, S5_STATE).transpose(0, 1, 3, 2)
        return jnp.einsum('jgph,gk->jgpkh', t, eye8).reshape(4, 8 * S5_STATE, LANES)

    wb = jnp.concatenate([in_blockdiag(bb_re), in_blockdiag(bb_im)], axis=-1).astype(BF16)
    cmat = jnp.stack([out_blockdiag(c_re), -out_blockdiag(c_im)], axis=1).astype(BF16)
    return wb, lam, cmat


def _gate_rows(a_log, dt_bias):
    neg_a = -jnp.exp(a_log)
    fwd = slice(0, DN_HEADS)
    bwd = slice(2 * DN_HEADS, 3 * DN_HEADS)
    acoef = jnp.zeros((LANES,), F32).at[fwd].set(neg_a[0]).at[bwd].set(neg_a[1])
    dtb = jnp.zeros((LANES,), F32).at[fwd].set(dt_bias[0]).at[bwd].set(dt_bias[1])
    return acoef[None, :], dtb[None, :], acoef[:DN_GATE_COLS, None], dtb[:DN_GATE_COLS, None]


def kernel(x, c, ctx, c_ctx, w_mod, b_mod, norm_pre_mix, norm_post_mix, norm_pre_ffn, norm_post_ffn, w_in, s5_lam_re, s5_lam_im, s5_log_step, s5_b_re, s5_b_im, s5_c_re, s5_c_im, s5_d, s5_glu_w, s5_glu_b, dn_conv_w, dn_a_log, dn_dt_bias, dn_norm_w, w_branch_s5, w_branch_dn, w_out, peer_w_q, peer_sub_keys, peer_u, peer_v):
    bsz, t_len, d = x.shape
    ctx_len = ctx.shape[1]
    l = 0

    pad = (-(bsz + 1)) % 8
    cv = jnp.concatenate([c, c_ctx[None, :], jnp.zeros((pad, d), F32)], axis=0)
    mod_all = _adaln(cv, w_mod[l].astype(BF16), b_mod[l][None, :])
    mods = mod_all[:bsz].reshape(bsz, 6, d)
    mods_ctx = jnp.broadcast_to(mod_all[bsz].reshape(1, 6, d), (bsz, 6, d))

    w = w_in[l]
    w_kvq = jnp.concatenate([w[:, COL_K:COL_AB], w[:, COL_Q:COL_Z]], axis=1).astype(BF16)
    w_gates = w[:, COL_Z:].astype(BF16)
    w_ctx = w[:, COL_K:COL_AB].astype(BF16)
    w_u = w[:, COL_U:COL_K].astype(BF16)
    w_ab32 = w[:, COL_AB:COL_Q]
    w_ab = jnp.pad(w_ab32, ((0, 0), (0, LANES - 4 * DN_HEADS))).astype(BF16)
    w_abt = w_ab32.T.astype(BF16)
    gain_pre = norm_pre_mix[l][None, :]
    conv_w = dn_conv_w[l]

    kv_c, _, u_c, ab_c, abt_c = _inproj(ctx, mods_ctx, gain_pre, w_ctx, conv_w[:, :2 * DN_WIDTH], None,
                                        w_u, w_ab, w_abt, tm=ctx_len, tn=2 * DN_WIDTH, row_len=ctx_len)
    kvq_x, gates_x, u_x, ab_x, abt_x = _inproj(x, mods, gain_pre, w_kvq, conv_w, w_gates, w_u, w_ab, w_abt,
                                               tm=min(INPROJ_TOKENS, t_len), tn=INPROJ_COLS, row_len=GRID_W)

    wbf, lamf, cf = _s5_coeffs(s5_lam_re[l, 0], s5_lam_im[l, 0], s5_log_step[l, 0], s5_b_re[l, 0], s5_b_im[l, 0],
                               s5_c_re[l, 0], s5_c_im[l, 0])
    wbb, lamb, cb = _s5_coeffs(s5_lam_re[l, 1], s5_lam_im[l, 1], s5_log_step[l, 1], s5_b_re[l, 1], s5_b_im[l, 1],
                               s5_c_re[l, 1], s5_c_im[l, 1])
    zeros_h = jnp.zeros((bsz, 2 * S5_COLS), F32)
    tc = S5_TIME_CHUNK
    hf_c, hb_c = _s5(u_c.reshape(bsz, ctx_len * S5_WIDTH), wbf, wbb, lamf, lamb, cf, cb, zeros_h, zeros_h, tc, False)
    yf, yb, _, _ = _s5(u_x.reshape(bsz, t_len * S5_WIDTH), wbf, wbb, lamf, lamb, cf, cb, hf_c, hb_c, tc, True)
    yf = yf.reshape(bsz, t_len, S5_WIDTH)
    yb = yb.reshape(bsz, t_len, S5_WIDTH)

    acoef, dtb, acoeft, dtbt = _gate_rows(dn_a_log[l], dn_dt_bias[l])
    zeros_s = jnp.zeros((bsz, DN_HEADS, DN_HEAD_DIM, DN_HEAD_DIM), F32)
    dn_args = (acoef, dtb, acoeft, dtbt)
    (sf_c,) = _deltanet(kv_c, ab_c, abt_c, *dn_args, zeros_s, ctx_len, False, False)
    (sb_c,) = _deltanet(kv_c, ab_c, abt_c, *dn_args, zeros_s, ctx_len, True, False)
    blk_len = min(DN_BLOCK_TOKENS, t_len)
    o_f, _ = _deltanet(kvq_x, ab_x, abt_x, *dn_args, sf_c, blk_len, False, True)
    o_b, _ = _deltanet(kvq_x, ab_x, abt_x, *dn_args, sb_c, blk_len, True, True)

    x1 = _post(x, mods, yf, yb, u_x, o_f, o_b, gates_x,
               s5_d[l][None, :], s5_glu_w[l].astype(BF16), s5_glu_b[l][None, :], dn_norm_w[l][None, :],
               w_branch_s5[l].astype(BF16), w_branch_dn[l].astype(BF16), w_out[l].astype(BF16),
               norm_post_mix[l][None, :], tm=min(MIXER_TOKENS, t_len))

    ht, c1, e1, r2, p2 = _route(x1, mods, norm_pre_ffn[l][None, :], peer_w_q[l].astype(BF16),
                                peer_sub_keys[l].astype(BF16), tm=min(ROUTE_TOKENS, t_len))
    out = _peer(ht, peer_u[l].astype(BF16), peer_v[l].astype(BF16), c1, e1, r2, p2, x1, mods,
                norm_post_ffn[l][None, :], tn=min(PEER_TOKENS, t_len), eb=PEER_EXPERT_BLOCK)
    return out
```
